```python
import jax, jax.numpy as jnp
from jax import lax
import numpy as np

D_MODEL = 1024
BATCH = 2
SEQ = 8192
DEPTH = 1

HEAD_DIM = 64
N_HEADS = D_MODEL // HEAD_DIM
N_HEADS_A = N_HEADS // 4
N_HEADS_B = N_HEADS - N_HEADS_A
WIDTH_A = N_HEADS_A * HEAD_DIM
WIDTH_B = N_HEADS_B * HEAD_DIM
CHUNK = 128
BLOCK = 128
DILATED_BRANCHES = ((128, 1), (512, 4), (2048, 16))
ROPE_THETA = 10000.0
D_FF = -(-8 * D_MODEL // (3 * 256)) * 256
PLE_DIM = 256
IN_COLS = 2 * WIDTH_A + 3 * WIDTH_B
EPS = 1e-6

kernel_name = "hybrid_sgu_dilated_attn_block"


def rmsnorm(x, g):
    xf = x.astype(jnp.float32)
    y = xf * lax.rsqrt(jnp.mean(xf * xf, axis=-1, keepdims=True) + EPS)
    return (y * g.astype(jnp.float32)).astype(x.dtype)


def rope(t, pos):
    half = t.shape[-1] // 2
    inv = ROPE_THETA ** (-jnp.arange(half, dtype=jnp.float32) / half)
    ang = pos[:, None] * inv[None, :]
    cos = jnp.cos(ang)[None, :, None, :]
    sin = jnp.sin(ang)[None, :, None, :]
    t = t.astype(jnp.float32)
    t1, t2 = t[..., :half], t[..., half:]
    return jnp.concatenate([t1 * cos - t2 * sin, t1 * sin + t2 * cos], axis=-1)


def chunked_sgu(u, v, w_s, b_s, norm_g):
    b, s, _ = u.shape
    u = jax.nn.gelu(u.astype(jnp.float32))
    vf = jax.nn.gelu(v.astype(jnp.float32))
    mu = jnp.mean(vf, axis=-1, keepdims=True)
    var = jnp.mean(jnp.square(vf - mu), axis=-1, keepdims=True)
    vf = (vf - mu) * lax.rsqrt(var + EPS) * norm_g.astype(jnp.float32)
    vf = vf.reshape(b, s // CHUNK, CHUNK, N_HEADS_A, HEAD_DIM)
    causal = jnp.tril(jnp.ones((CHUNK, CHUNK), jnp.float32))
    w = w_s.astype(jnp.float32) * causal[None]
    mixed = jnp.einsum('hij,bnjhd->bnihd', w, vf) + b_s.astype(jnp.float32).T[None, None, :, :, None]
    return u * mixed.reshape(b, s, WIDTH_A)


def dilated_branch(q, k, v, window, dilation):
    b, h, s, dh = q.shape
    n_back = window // dilation
    span = dilation * BLOCK
    s_pad = -(-s // span) * span
    sub_len = s_pad // dilation
    nb = sub_len // BLOCK

    def to_sub(t):
        t = jnp.pad(t, ((0, 0), (0, 0), (0, s_pad - s), (0, 0)))
        t = t.reshape(b, h, sub_len, dilation, dh)
        t = jnp.swapaxes(t, 2, 3)
        return t.reshape(b, h, dilation, nb, BLOCK, dh)

    qb, kb, vb = to_sub(q), to_sub(k), to_sub(v)
    shift = ((0, 0), (0, 0), (0, 0), (1, 0), (0, 0), (0, 0))
    kw = jnp.concatenate([jnp.pad(kb[:, :, :, :-1], shift), kb], axis=4)
    vw = jnp.concatenate([jnp.pad(vb[:, :, :, :-1], shift), vb], axis=4)
    scores = jnp.einsum('bhrnqd,bhrnkd->bhrnqk', qb, kw) * (dh ** -0.5)
    qi = jnp.arange(BLOCK)[:, None]
    kc = jnp.arange(2 * BLOCK)[None, :]
    dist = BLOCK + qi - kc
    band = (dist >= 0) & (dist <= n_back)
    blk = jnp.arange(nb)[:, None, None]
    valid = band[None] & ((blk > 0) | (kc[None] >= BLOCK))
    scores = jnp.where(valid, scores, -jnp.inf)
    m = jnp.max(scores, axis=-1, keepdims=True)
    pr = jnp.exp(scores - m)
    l = jnp.sum(pr, axis=-1, keepdims=True)
    o = jnp.einsum('bhrnqk,bhrnkd->bhrnqd', pr, vw) / l
    lse = (m + jnp.log(l))[..., 0]
    o = jnp.swapaxes(o.reshape(b, h, dilation, sub_len, dh), 2, 3).reshape(b, h, s_pad, dh)[:, :, :s]
    lse = jnp.swapaxes(lse.reshape(b, h, dilation, sub_len), 2, 3).reshape(b, h, s_pad)[:, :, :s]
    return o, lse


def dilated_mixture_attention(q, k, v):
    outs, lses = [], []
    for window, dilation in DILATED_BRANCHES:
        o, lse = dilated_branch(q, k, v, window, dilation)
        outs.append(o)
        lses.append(lse)
    o = jnp.stack(outs, axis=0)
    wts = jax.nn.softmax(jnp.stack(lses, axis=0), axis=0)
    return jnp.sum(wts[..., None] * o, axis=0)


def setup_inputs(seed: int = 0) -> dict:
    key = jax.random.key(seed)
    ks = jax.random.split(key, 20)
    f32 = jnp.float32

    def nrm(k, shape, fan_in):
        return jax.random.normal(k, shape, f32) * (fan_in ** -0.5)

    def gain(k, shape):
        return 1.0 + 0.05 * jax.random.normal(k, shape, f32)

    return {
        "x": jax.random.normal(ks[0], (BATCH, SEQ, D_MODEL), f32),
        "p": jax.random.normal(ks[1], (DEPTH, BATCH, SEQ, PLE_DIM), f32),
        "mix_norm_g": gain(ks[2], (DEPTH, D_MODEL)),
        "w_in": nrm(ks[3], (DEPTH, D_MODEL, IN_COLS), D_MODEL),
        "sgu_w": nrm(ks[4], (DEPTH, N_HEADS_A, CHUNK, CHUNK), CHUNK),
        "sgu_b": 1.0 + 0.1 * jax.random.normal(ks[5], (DEPTH, N_HEADS_A, CHUNK), f32),
        "sgu_norm_g": gain(ks[6], (DEPTH, WIDTH_A)),
        "out_norm_a": gain(ks[7], (DEPTH, WIDTH_A)),
        "out_norm_b": gain(ks[8], (DEPTH, WIDTH_B)),
        "w_out": nrm(ks[9], (DEPTH, D_MODEL, D_MODEL), D_MODEL),
        "ffn_norm_g": gain(ks[10], (DEPTH, D_MODEL)),
        "w_gate": nrm(ks[11], (DEPTH, D_MODEL, D_FF), D_MODEL),
        "w_up": nrm(ks[12], (DEPTH, D_MODEL, D_FF), D_MODEL),
        "w_down": nrm(ks[13], (DEPTH, D_FF, D_MODEL), D_FF),
        "ple_norm_g": gain(ks[14], (DEPTH, D_MODEL)),
        "w_ple_gate": nrm(ks[15], (DEPTH, D_MODEL, D_MODEL), D_MODEL),
        "w_ple_proj": nrm(ks[16], (DEPTH, PLE_DIM, D_MODEL), PLE_DIM),
        "final_norm_g": gain(ks[17], (D_MODEL,)),
    }


def reference(x, p, mix_norm_g, w_in, sgu_w, sgu_b, sgu_norm_g, out_norm_a, out_norm_b,
              w_out, ffn_norm_g, w_gate, w_up, w_down, ple_norm_g, w_ple_gate,
              w_ple_proj, final_norm_g):
    b, s, _ = x.shape
    pos = jnp.arange(s, dtype=jnp.float32)
    h = x
    for i in range(DEPTH):
        hn = rmsnorm(h, mix_norm_g[i])
        proj = hn @ w_in[i]
        u_a = proj[..., :WIDTH_A]
        v_a = proj[..., WIDTH_A:2 * WIDTH_A]
        qkv = proj[..., 2 * WIDTH_A:].reshape(b, s, 3, N_HEADS_B, HEAD_DIM)
        y_a = chunked_sgu(u_a, v_a, sgu_w[i], sgu_b[i], sgu_norm_g[i])
        q = jnp.transpose(rope(qkv[:, :, 0], pos), (0, 2, 1, 3))
        k = jnp.transpose(rope(qkv[:, :, 1], pos), (0, 2, 1, 3))
        v = jnp.transpose(qkv[:, :, 2].astype(jnp.float32), (0, 2, 1, 3))
        y_b = dilated_mixture_attention(q, k, v)
        y_b = jnp.transpose(y_b, (0, 2, 1, 3)).reshape(b, s, WIDTH_B)
        y = jnp.concatenate([rmsnorm(y_a, out_norm_a[i]), rmsnorm(y_b, out_norm_b[i])], axis=-1)
        h = h + (y.astype(h.dtype) @ w_out[i])
        hn = rmsnorm(h, ffn_norm_g[i])
        h = h + (jax.nn.silu(hn @ w_gate[i]) * (hn @ w_up[i])) @ w_down[i]
        gate = jax.nn.sigmoid(rmsnorm(h, ple_norm_g[i]) @ w_ple_gate[i])
        h = h + gate * (p[i] @ w_ple_proj[i])
    return rmsnorm(h, final_norm_g)
```

```python
import functools

import jax
import jax.numpy as jnp
from jax import lax
from jax.experimental import pallas as pl
from jax.experimental.pallas import tpu as pltpu

D_MODEL = 1024
HEAD_DIM = 64
N_HEADS_A = 4
WIDTH_A = 256
WIDTH_B = 768
CHUNK = 128
BLOCK = 128
DILATIONS = (1, 4, 16)
N_RES = 16
TILE = N_RES * BLOCK
D_FF = 2816
PLE_DIM = 256
EPS = 1e-6
ROPE_THETA = 10000.0
LANES = 128
SUBLANES = 8

SGU_ROWS = 512
RES_PER_STEP = 4
FF_CHUNK = 256
PAIRS_PER_STEP = 2

F32 = jnp.float32
BF16 = jnp.bfloat16


def _rms(x, g):
    return x * lax.rsqrt(jnp.mean(x * x, axis=-1, keepdims=True) + EPS) * g


def _dot(a, b):
    return jnp.dot(a, b, preferred_element_type=F32)


def _resident(shape):
    return pl.BlockSpec(shape, lambda *_: (0,) * len(shape), pipeline_mode=pl.Buffered(1))


def _sgu_kernel(x_ref, g_ref, wuv_ref, sw_ref, sb_ref, sg_ref, na_ref, o_ref):
    hn = _rms(x_ref[...], g_ref[...]).astype(BF16)
    uv = _dot(hn, wuv_ref[...])
    u = jax.nn.gelu(uv[:, :WIDTH_A])
    v = jax.nn.gelu(uv[:, WIDTH_A:])
    mu = jnp.mean(v, axis=-1, keepdims=True)
    d = v - mu
    var = jnp.mean(d * d, axis=-1, keepdims=True)
    vf = d * lax.rsqrt(var + EPS) * sg_ref[...]

    row = lax.broadcasted_iota(jnp.int32, (CHUNK, CHUNK), 0)
    col = lax.broadcasted_iota(jnp.int32, (CHUNK, CHUNK), 1)
    w4 = jnp.concatenate(
        [jnp.where(col <= row, sw_ref[h], 0.0) for h in range(N_HEADS_A)], axis=0).astype(BF16)
    lane = lax.broadcasted_iota(jnp.int32, (CHUNK, WIDTH_A), 1)
    for c in range(SGU_ROWS // CHUNK):
        rows = slice(c * CHUNK, (c + 1) * CHUNK)
        mm = _dot(w4, vf[rows].astype(BF16))
        mixed = mm[3 * CHUNK:]
        for h in (2, 1, 0):
            mixed = jnp.where(lane < (h + 1) * HEAD_DIM, mm[h * CHUNK:(h + 1) * CHUNK], mixed)
        ya = u[rows] * (mixed + sb_ref[...])
        o_ref[rows, :] = _rms(ya, na_ref[...]).astype(o_ref.dtype)


def _qkv_kernel(x_ref, g_ref, cos_ref, sin_ref, wq_ref, wk_ref, wv_ref,
                q_ref, k_ref, v_ref, hn_ref):
    for r in range(RES_PER_STEP):
        xr = x_ref[:, r * D_MODEL:(r + 1) * D_MODEL]
        hn_ref[r * BLOCK:(r + 1) * BLOCK, :] = _rms(xr, g_ref[...]).astype(BF16)
    hn = hn_ref[...]
    lane = lax.broadcasted_iota(jnp.int32, (BLOCK, LANES), 1)
    first_half = (lane % HEAD_DIM) < (HEAD_DIM // 2)

    def rope_store(t, out_ref, scale):
        for r in range(RES_PER_STEP):
            cos = cos_ref[:, r * LANES:(r + 1) * LANES]
            sin = sin_ref[:, r * LANES:(r + 1) * LANES]
            for c in range(WIDTH_B // LANES):
                tc = t[r * BLOCK:(r + 1) * BLOCK, c * LANES:(c + 1) * LANES]
                rot = jnp.where(first_half,
                                pltpu.roll(tc, LANES - HEAD_DIM // 2, 1),
                                pltpu.roll(tc, HEAD_DIM // 2, 1))
                val = tc * cos + rot * sin
                if scale != 1.0:
                    val = val * scale
                out_ref[r * BLOCK:(r + 1) * BLOCK, c * LANES:(c + 1) * LANES] = val

    rope_store(_dot(hn, wq_ref[...]), q_ref, HEAD_DIM ** -0.5)
    rope_store(_dot(hn, wk_ref[...]), k_ref, 1.0)
    v_ref[...] = _dot(hn, wv_ref[...])


def _branch_chunks(dilation, base, idx):
    if dilation == 16:
        return [(base + idx * BLOCK, BLOCK)]
    if dilation == 4:
        r4, jj = idx % 4, idx // 4
        return [(base + (4 * c + r4) * BLOCK + 32 * jj, 32) for c in range(4)]
    return [(base + r * BLOCK + SUBLANES * idx, SUBLANES) for r in range(N_RES)]


def _prev_block(dilation, idx, cur, prv):
    if dilation == 16:
        return prv, idx
    n_per_res = N_RES // dilation
    if dilation == 4:
        r4, jj = idx % 4, idx // 4
        base = jnp.where(jj > 0, cur, prv)
        return base, r4 + 4 * ((jj + n_per_res - 1) % n_per_res)
    base = jnp.where(idx > 0, cur, prv)
    return base, (idx + n_per_res - 1) % n_per_res


def _sub_index(dilation, a):
    if dilation == 16:
        return a
    if dilation == 4:
        return 4 * (a % 32) + a // 32
    return N_RES * (a % SUBLANES) + a // SUBLANES


def _load_rows(ref, chunks, lanes):
    parts = [ref[pl.ds(pl.multiple_of(s, SUBLANES), n), lanes] for s, n in chunks]
    return parts[0] if len(parts) == 1 else jnp.concatenate(parts, axis=0)


def _store_rows(ref, chunks, lanes, val):
    off = 0
    for s, n in chunks:
        ref[pl.ds(pl.multiple_of(s, SUBLANES), n), lanes] = val[off:off + n]
        off += n


def _attn_kernel(q_ref, k_ref, v_ref, o_ref, kk_ref, vv_ref, acc_ref, m_ref, l_ref, bias_ref):
    j = pl.program_id(2)
    slot = j % 2
    cur = pl.multiple_of(slot * TILE, TILE)
    prv = pl.multiple_of((1 - slot) * TILE, TILE)

    @pl.when(j == 0)
    def _():
        kk_ref[pl.ds(prv, TILE), :] = jnp.zeros((TILE, kk_ref.shape[1]), F32)
        vv_ref[pl.ds(prv, TILE), :] = jnp.zeros((TILE, vv_ref.shape[1]), F32)

    kk_ref[pl.ds(cur, TILE), :] = k_ref[...]
    vv_ref[pl.ds(cur, TILE), :] = v_ref[...]

    qa = lax.broadcasted_iota(jnp.int32, (BLOCK, 2 * BLOCK), 0)
    kc = lax.broadcasted_iota(jnp.int32, (BLOCK, 2 * BLOCK), 1)
    for bi, dilation in enumerate(DILATIONS):
        dist = (BLOCK + _sub_index(dilation, qa)
                - _sub_index(dilation, kc % BLOCK) - BLOCK * (kc // BLOCK))
        band = (dist >= 0) & (dist <= BLOCK)
        bias_ref[bi] = jnp.where(band, 0.0, -jnp.inf)

    def run_branch(bi, order):
        dilation = DILATIONS[bi]

        def body(idx, carry):
            if dilation == 16:
                first = j == 0
            elif dilation == 4:
                first = (j == 0) & (idx // 4 == 0)
            else:
                first = (j == 0) & (idx == 0)
            bias = bias_ref[bi]
            bias = jnp.concatenate([jnp.where(first, -jnp.inf, bias[:, :BLOCK]), bias[:, BLOCK:]],
                                   axis=1)
            low = lax.broadcasted_iota(jnp.int32, (BLOCK, LANES), 1) < HEAD_DIM
            q_chunks = _branch_chunks(dilation, 0, idx)
            kc_chunks = _branch_chunks(dilation, cur, idx)
            pbase, pidx = _prev_block(dilation, idx, cur, prv)
            kp_chunks = _branch_chunks(dilation, pbase, pidx)
            for p in range(PAIRS_PER_STEP):
                lanes = slice(p * LANES, (p + 1) * LANES)
                q = _load_rows(q_ref, q_chunks, lanes)
                lhs = jnp.concatenate([jnp.where(low, q, 0.0), jnp.where(low, 0.0, q)],
                                      axis=0).astype(BF16)
                kb = jnp.concatenate([_load_rows(kk_ref, kp_chunks, lanes),
                                      _load_rows(kk_ref, kc_chunks, lanes)], axis=0).astype(BF16)
                vb = jnp.concatenate([_load_rows(vv_ref, kp_chunks, lanes),
                                      _load_rows(vv_ref, kc_chunks, lanes)], axis=0).astype(BF16)
                s = lax.dot_general(lhs, kb, (((1,), (1,)), ((), ())),
                                    preferred_element_type=F32)
                ps, ms, ls = [], [], []
                for h in range(2):
                    sh = s[h * BLOCK:(h + 1) * BLOCK] + bias
                    mh = jnp.max(sh, axis=-1, keepdims=True)
                    ph = jnp.exp(sh - mh)
                    ls.append(jnp.sum(ph, axis=-1, keepdims=True))
                    ms.append(mh)
                    ps.append(ph.astype(BF16))
                pv = _dot(jnp.concatenate(ps, axis=0), vb)
                out = jnp.where(low, pv[:BLOCK], pv[BLOCK:])
                m_b = jnp.where(low, ms[0], ms[1])
                l_b = jnp.where(low, ls[0], ls[1])
                if order == "first":
                    _store_rows(acc_ref, q_chunks, lanes, out)
                    _store_rows(m_ref, q_chunks, lanes, m_b)
                    _store_rows(l_ref, q_chunks, lanes, l_b)
                else:
                    m_o = _load_rows(m_ref, q_chunks, lanes)
                    m_n = jnp.maximum(m_o, m_b)
                    a_o = jnp.exp(m_o - m_n)
                    a_b = jnp.exp(m_b - m_n)
                    acc = a_o * _load_rows(acc_ref, q_chunks, lanes) + a_b * out
                    l_n = a_o * _load_rows(l_ref, q_chunks, lanes) + a_b * l_b
                    if order == "last":
                        _store_rows(o_ref, q_chunks, lanes, acc / l_n)
                    else:
                        _store_rows(acc_ref, q_chunks, lanes, acc)
                        _store_rows(m_ref, q_chunks, lanes, m_n)
                        _store_rows(l_ref, q_chunks, lanes, l_n)
            return carry

        lax.fori_loop(0, TILE // BLOCK, body, 0)

    run_branch(0, "first")
    run_branch(1, "middle")
    run_branch(2, "last")


def _post_kernel(x_ref, p_ref, ya_ref, yb_ref, nb_ref, woa_ref, wob_ref, gf_ref, wg_ref, wu_ref,
                 wd_ref, gp_ref, wpg_ref, wpp_ref, gz_ref, o_ref, xs_ref, yas_ref, ps_ref, act_ref):
    for r in range(RES_PER_STEP):
        rows = slice(r * BLOCK, (r + 1) * BLOCK)
        xs_ref[rows, :] = x_ref[:, r * D_MODEL:(r + 1) * D_MODEL]
        yas_ref[rows, :] = ya_ref[:, r * WIDTH_A:(r + 1) * WIDTH_A]
        ps_ref[rows, :] = p_ref[:, r * PLE_DIM:(r + 1) * PLE_DIM].astype(BF16)
    ybn = _rms(yb_ref[...], nb_ref[...]).astype(BF16)
    h = xs_ref[...] + (_dot(yas_ref[...], woa_ref[...]) + _dot(ybn, wob_ref[...]))

    hn = _rms(h, gf_ref[...]).astype(BF16)
    for c in range(D_FF // FF_CHUNK):
        cols = slice(c * FF_CHUNK, (c + 1) * FF_CHUNK)
        gate = _dot(hn, wg_ref[:, cols])
        act_ref[:, cols] = (jax.nn.silu(gate) * _dot(hn, wu_ref[:, cols])).astype(BF16)
    h = h + _dot(act_ref[...], wd_ref[...])

    gate = jax.nn.sigmoid(_dot(_rms(h, gp_ref[...]).astype(BF16), wpg_ref[...]))
    h = h + gate * _dot(ps_ref[...], wpp_ref[...])
    out = _rms(h, gz_ref[...])
    for r in range(RES_PER_STEP):
        o_ref[:, r * D_MODEL:(r + 1) * D_MODEL] = out[r * BLOCK:(r + 1) * BLOCK]


def _rope_tables(seq):
    half = HEAD_DIM // 2
    inv = ROPE_THETA ** (-jnp.arange(half, dtype=F32) / half)
    ang = jnp.arange(seq, dtype=F32)[:, None] * inv[None, :]
    cos, sin = jnp.cos(ang), jnp.sin(ang)
    reps = LANES // HEAD_DIM
    cos_t = jnp.tile(cos, (1, 2 * reps))
    sin_t = jnp.tile(jnp.concatenate([-sin, sin], axis=1), (1, reps))
    return cos_t, sin_t


def _params(vmem_mib, n_axes):
    return pltpu.CompilerParams(dimension_semantics=("arbitrary",) * n_axes,
                                vmem_limit_bytes=vmem_mib * 1024 * 1024)


def kernel(x, p, mix_norm_g, w_in, sgu_w, sgu_b, sgu_norm_g, out_norm_a, out_norm_b, w_out,
           ffn_norm_g, w_gate, w_up, w_down, ple_norm_g, w_ple_gate, w_ple_proj, final_norm_g):
    b, s, d = x.shape
    assert d == D_MODEL and s % TILE == 0 and w_in.shape[0] == 1
    n_tiles = s // TILE
    n_tok = b * s
    row = lambda a: a.reshape(1, -1)

    w_in0 = w_in[0].astype(BF16)
    w_uv = w_in0[:, :2 * WIDTH_A]
    w_q = w_in0[:, 2 * WIDTH_A:2 * WIDTH_A + WIDTH_B]
    w_k = w_in0[:, 2 * WIDTH_A + WIDTH_B:2 * WIDTH_A + 2 * WIDTH_B]
    w_v = w_in0[:, 2 * WIDTH_A + 2 * WIDTH_B:]
    g_mix = row(mix_norm_g[0])

    sgu_bias = jnp.repeat(sgu_b[0].T, HEAD_DIM, axis=1)
    y_a = pl.pallas_call(
        _sgu_kernel,
        grid=(n_tok // SGU_ROWS,),
        in_specs=[
            pl.BlockSpec((SGU_ROWS, D_MODEL), lambda i: (i, 0)),
            _resident((1, D_MODEL)),
            _resident((D_MODEL, 2 * WIDTH_A)),
            _resident((N_HEADS_A, CHUNK, CHUNK)),
            _resident((CHUNK, WIDTH_A)),
            _resident((1, WIDTH_A)),
            _resident((1, WIDTH_A)),
        ],
        out_specs=pl.BlockSpec((SGU_ROWS, WIDTH_A), lambda i: (i, 0)),
        out_shape=jax.ShapeDtypeStruct((n_tok, WIDTH_A), BF16),
        compiler_params=_params(32, 1),
    )(x.reshape(n_tok, D_MODEL), g_mix, w_uv, sgu_w[0], sgu_bias, row(sgu_norm_g[0]),
      row(out_norm_a[0]))

    x_rm = x.reshape(b, n_tiles, BLOCK, N_RES * D_MODEL)
    cos_t, sin_t = _rope_tables(s)
    cos_rm = cos_t.reshape(n_tiles, BLOCK, N_RES * LANES)
    sin_rm = sin_t.reshape(n_tiles, BLOCK, N_RES * LANES)
    n_rb = N_RES // RES_PER_STEP
    step_rows = RES_PER_STEP * BLOCK

    qkv_shape = jax.ShapeDtypeStruct((b, n_tiles, TILE, WIDTH_B), F32)
    qkv_spec = pl.BlockSpec((None, None, step_rows, WIDTH_B), lambda bi, ji, ri: (bi, ji, ri, 0))
    tab_spec = pl.BlockSpec((None, BLOCK, RES_PER_STEP * LANES), lambda bi, ji, ri: (ji, 0, ri))
    x_spec = pl.BlockSpec((None, None, BLOCK, RES_PER_STEP * D_MODEL),
                          lambda bi, ji, ri: (bi, ji, 0, ri))
    q, k, v = pl.pallas_call(
        _qkv_kernel,
        grid=(b, n_tiles, n_rb),
        in_specs=[x_spec, _resident((1, D_MODEL)), tab_spec, tab_spec,
                  _resident((D_MODEL, WIDTH_B)), _resident((D_MODEL, WIDTH_B)),
                  _resident((D_MODEL, WIDTH_B))],
        out_specs=[qkv_spec, qkv_spec, qkv_spec],
        out_shape=[qkv_shape, qkv_shape, qkv_shape],
        scratch_shapes=[pltpu.VMEM((step_rows, D_MODEL), BF16)],
        compiler_params=_params(40, 3),
    )(x_rm, g_mix, cos_rm, sin_rm, w_q, w_k, w_v)

    step_lanes = PAIRS_PER_STEP * LANES
    att_spec = pl.BlockSpec((None, None, TILE, step_lanes), lambda bi, ci, ji: (bi, ji, 0, ci))
    y_b = pl.pallas_call(
        _attn_kernel,
        grid=(b, WIDTH_B // step_lanes, n_tiles),
        in_specs=[att_spec, att_spec, att_spec],
        out_specs=att_spec,
        out_shape=qkv_shape,
        scratch_shapes=[pltpu.VMEM((2 * TILE, step_lanes), F32), pltpu.VMEM((2 * TILE, step_lanes), F32),
                        pltpu.VMEM((TILE, step_lanes), F32), pltpu.VMEM((TILE, step_lanes), F32),
                        pltpu.VMEM((TILE, step_lanes), F32),
                        pltpu.VMEM((len(DILATIONS), BLOCK, 2 * BLOCK), F32)],
        compiler_params=_params(48, 3),
    )(q, k, v)

    w_out0 = w_out[0].astype(BF16)
    rm_spec = lambda width: pl.BlockSpec((None, None, BLOCK, RES_PER_STEP * width),
                                         lambda bi, ji, ri: (bi, ji, 0, ri))
    out = pl.pallas_call(
        _post_kernel,
        grid=(b, n_tiles, n_rb),
        in_specs=[
            rm_spec(D_MODEL), rm_spec(PLE_DIM), rm_spec(WIDTH_A), qkv_spec,
            _resident((1, WIDTH_B)),
            _resident((WIDTH_A, D_MODEL)), _resident((WIDTH_B, D_MODEL)),
            _resident((1, D_MODEL)),
            _resident((D_MODEL, D_FF)), _resident((D_MODEL, D_FF)), _resident((D_FF, D_MODEL)),
            _resident((1, D_MODEL)),
            _resident((D_MODEL, D_MODEL)), _resident((PLE_DIM, D_MODEL)),
            _resident((1, D_MODEL)),
        ],
        out_specs=rm_spec(D_MODEL),
        out_shape=jax.ShapeDtypeStruct((b, n_tiles, BLOCK, N_RES * D_MODEL), x.dtype),
        scratch_shapes=[pltpu.VMEM((step_rows, D_MODEL), F32), pltpu.VMEM((step_rows, WIDTH_A), BF16),
                        pltpu.VMEM((step_rows, PLE_DIM), BF16), pltpu.VMEM((step_rows, D_FF), BF16)],
        compiler_params=_params(56, 3),
    )(x_rm, p[0].reshape(b, n_tiles, BLOCK, N_RES * PLE_DIM),
      y_a.reshape(b, n_tiles, BLOCK, N_RES * WIDTH_A), y_b,
      row(out_norm_b[0]), w_out0[:WIDTH_A], w_out0[WIDTH_A:], row(ffn_norm_g[0]),
      w_gate[0].astype(BF16), w_up[0].astype(BF16), w_down[0].astype(BF16),
      row(ple_norm_g[0]), w_ple_gate[0].astype(BF16), w_ple_proj[0].astype(BF16),
      row(final_norm_g))
    return out.reshape(b, s, D_MODEL)
```

```python
import jax
import jax.numpy as jnp
from jax import lax
from jax.experimental import pallas as pl
from jax.experimental.pallas import tpu as pltpu

D_MODEL = 1024
HEAD_DIM = 64
N_HEADS_A = 4
WIDTH_A = 256
WIDTH_B = 768
CHUNK = 128
BLOCK = 128
DILATIONS = (1, 4, 16)
N_RES = 16
TILE = N_RES * BLOCK
D_FF = 2816
PLE_DIM = 256
EPS = 1e-6
ROPE_THETA = 10000.0
LANES = 128
SUBLANES = 8

ROWS = 512
M_PER_STEP = ROWS // N_RES
FF_CHUNK = 256
PAIRS_PER_STEP = 2
ATTN_UNROLL = 2

F32 = jnp.float32
BF16 = jnp.bfloat16


def _rms(x, g):
    return x * lax.rsqrt(jnp.mean(x * x, axis=-1, keepdims=True) + EPS) * g


def _dot(a, b):
    return jnp.dot(a, b, preferred_element_type=F32)


def _resident(shape):
    return pl.BlockSpec(shape, lambda *_: (0,) * len(shape), pipeline_mode=pl.Buffered(1))


def _sgu_kernel(x_ref, g_ref, wuv_ref, sw_ref, sb_ref, sg_ref, na_ref, o_ref):
    hn = _rms(x_ref[...], g_ref[...]).astype(BF16)
    uv = _dot(hn, wuv_ref[...])
    u = jax.nn.gelu(uv[:, :WIDTH_A])
    v = jax.nn.gelu(uv[:, WIDTH_A:])
    mu = jnp.mean(v, axis=-1, keepdims=True)
    d = v - mu
    var = jnp.mean(d * d, axis=-1, keepdims=True)
    vf = d * lax.rsqrt(var + EPS) * sg_ref[...]

    row = lax.broadcasted_iota(jnp.int32, (CHUNK, CHUNK), 0)
    col = lax.broadcasted_iota(jnp.int32, (CHUNK, CHUNK), 1)
    w4 = jnp.concatenate(
        [jnp.where(col <= row, sw_ref[h], 0.0) for h in range(N_HEADS_A)], axis=0).astype(BF16)
    lane = lax.broadcasted_iota(jnp.int32, (CHUNK, WIDTH_A), 1)
    for c in range(ROWS // CHUNK):
        rows = slice(c * CHUNK, (c + 1) * CHUNK)
        mm = _dot(w4, vf[rows].astype(BF16))
        mixed = mm[3 * CHUNK:]
        for h in (2, 1, 0):
            mixed = jnp.where(lane < (h + 1) * HEAD_DIM, mm[h * CHUNK:(h + 1) * CHUNK], mixed)
        ya = u[rows] * (mixed + sb_ref[...])
        o_ref[rows, :] = _rms(ya, na_ref[...]).astype(o_ref.dtype)


def _to_residue_major(slab_ref, out_ref):
    for r in range(N_RES):
        for c in range(WIDTH_B // LANES):
            out_ref[r, :, c * LANES:(c + 1) * LANES] = slab_ref[c, pl.ds(r, M_PER_STEP, stride=N_RES), :]


def _qkv_kernel(x_ref, g_ref, inv_ref, wq_ref, wk_ref, wv_ref, q_ref, k_ref, v_ref,
                cu_ref, su_ref, qs_ref, ks_ref, vs_ref):
    i = pl.program_id(1)
    lane = lax.broadcasted_iota(jnp.int32, (1, LANES), 1)
    sign = jnp.where((lane % HEAD_DIM) < (HEAD_DIM // 2), -1.0, 1.0)

    @pl.when((pl.program_id(0) == 0) & (i == 0))
    def _():
        u = lax.broadcasted_iota(jnp.int32, (ROWS, LANES), 0).astype(F32)
        ang = u * inv_ref[...]
        cu_ref[...] = jnp.cos(ang)
        su_ref[...] = jnp.sin(ang) * sign

    ang0 = (i * ROWS).astype(F32) * inv_ref[...]
    ct = jnp.cos(ang0)
    st = jnp.sin(ang0) * sign
    cos = ct * cu_ref[...] - st * su_ref[...]
    sin = st * cu_ref[...] + ct * su_ref[...]
    first_half = (lax.broadcasted_iota(jnp.int32, (ROWS, LANES), 1) % HEAD_DIM) < (HEAD_DIM // 2)

    hn = _rms(x_ref[...], g_ref[...]).astype(BF16)

    def project(w_ref, slab_ref, out_ref, rope, scale):
        t = _dot(hn, w_ref[...])
        for c in range(WIDTH_B // LANES):
            tc = t[:, c * LANES:(c + 1) * LANES]
            if rope:
                rot = jnp.where(first_half,
                                pltpu.roll(tc, LANES - HEAD_DIM // 2, 1),
                                pltpu.roll(tc, HEAD_DIM // 2, 1))
                tc = tc * cos + rot * sin
            if scale != 1.0:
                tc = tc * scale
            slab_ref[c] = tc
        _to_residue_major(slab_ref, out_ref)

    project(wq_ref, qs_ref, q_ref, True, HEAD_DIM ** -0.5)
    project(wk_ref, ks_ref, k_ref, True, 1.0)
    project(wv_ref, vs_ref, v_ref, False, 1.0)


def _branch_chunks(dilation, base, idx):
    if dilation == 16:
        return [(base + idx * BLOCK, BLOCK)]
    if dilation == 4:
        r4, jj = idx % 4, idx // 4
        return [(base + (4 * c + r4) * BLOCK + 32 * jj, 32) for c in range(4)]
    return [(base + r * BLOCK + SUBLANES * idx, SUBLANES) for r in range(N_RES)]


def _prev_block(dilation, idx, cur, prv):
    if dilation == 16:
        return prv, idx
    n_per_res = N_RES // dilation
    if dilation == 4:
        r4, jj = idx % 4, idx // 4
        base = jnp.where(jj > 0, cur, prv)
        return base, r4 + 4 * ((jj + n_per_res - 1) % n_per_res)
    base = jnp.where(idx > 0, cur, prv)
    return base, (idx + n_per_res - 1) % n_per_res


def _sub_index(dilation, a):
    if dilation == 16:
        return a
    if dilation == 4:
        return 4 * (a % 32) + a // 32
    return N_RES * (a % SUBLANES) + a // SUBLANES


def _load_rows(ref, chunks, lanes):
    parts = [ref[pl.ds(pl.multiple_of(s, SUBLANES), n), lanes] for s, n in chunks]
    return parts[0] if len(parts) == 1 else jnp.concatenate(parts, axis=0)


def _store_rows(ref, chunks, lanes, val):
    off = 0
    for s, n in chunks:
        ref[pl.ds(pl.multiple_of(s, SUBLANES), n), lanes] = val[off:off + n]
        off += n


def _attn_kernel(q_ref, k_ref, v_ref, o_ref, kk_ref, vv_ref, acc_ref, m_ref, l_ref, bias_ref):
    j = pl.program_id(2)
    slot = j % 2
    cur = pl.multiple_of(slot * TILE, TILE)
    prv = pl.multiple_of((1 - slot) * TILE, TILE)

    @pl.when(j == 0)
    def _():
        kk_ref[pl.ds(prv, TILE), :] = jnp.zeros((TILE, kk_ref.shape[1]), F32)
        vv_ref[pl.ds(prv, TILE), :] = jnp.zeros((TILE, vv_ref.shape[1]), F32)

    kk_ref[pl.ds(cur, TILE), :] = k_ref[...]
    vv_ref[pl.ds(cur, TILE), :] = v_ref[...]

    qa = lax.broadcasted_iota(jnp.int32, (BLOCK, 2 * BLOCK), 0)
    kc = lax.broadcasted_iota(jnp.int32, (BLOCK, 2 * BLOCK), 1)
    for bi, dilation in enumerate(DILATIONS):
        dist = (BLOCK + _sub_index(dilation, qa)
                - _sub_index(dilation, kc % BLOCK) - BLOCK * (kc // BLOCK))
        band = (dist >= 0) & (dist <= BLOCK)
        bias_ref[bi] = jnp.where(band, 0.0, -jnp.inf)

    def run_branch(bi, order):
        dilation = DILATIONS[bi]

        def body(idx, carry):
            if dilation == 16:
                first = j == 0
            elif dilation == 4:
                first = (j == 0) & (idx // 4 == 0)
            else:
                first = (j == 0) & (idx == 0)
            bias = bias_ref[bi]
            bias = jnp.concatenate([jnp.where(first, -jnp.inf, bias[:, :BLOCK]), bias[:, BLOCK:]],
                                   axis=1)
            low = lax.broadcasted_iota(jnp.int32, (BLOCK, LANES), 1) < HEAD_DIM
            q_chunks = _branch_chunks(dilation, 0, idx)
            kc_chunks = _branch_chunks(dilation, cur, idx)
            pbase, pidx = _prev_block(dilation, idx, cur, prv)
            kp_chunks = _branch_chunks(dilation, pbase, pidx)
            for p in range(PAIRS_PER_STEP):
                lanes = slice(p * LANES, (p + 1) * LANES)
                q = _load_rows(q_ref, q_chunks, lanes)
                lhs = jnp.concatenate([jnp.where(low, q, 0.0), jnp.where(low, 0.0, q)],
                                      axis=0).astype(BF16)
                kb = jnp.concatenate([_load_rows(kk_ref, kp_chunks, lanes),
                                      _load_rows(kk_ref, kc_chunks, lanes)], axis=0).astype(BF16)
                vb = jnp.concatenate([_load_rows(vv_ref, kp_chunks, lanes),
                                      _load_rows(vv_ref, kc_chunks, lanes)], axis=0).astype(BF16)
                s = lax.dot_general(lhs, kb, (((1,), (1,)), ((), ())),
                                    preferred_element_type=F32)
                ps, ms, ls = [], [], []
                for h in range(2):
                    sh = s[h * BLOCK:(h + 1) * BLOCK] + bias
                    mh = jnp.max(sh, axis=-1, keepdims=True)
                    ph = jnp.exp(sh - mh)
                    ls.append(jnp.sum(ph, axis=-1, keepdims=True))
                    ms.append(mh)
                    ps.append(ph.astype(BF16))
                pv = _dot(jnp.concatenate(ps, axis=0), vb)
                out = jnp.where(low, pv[:BLOCK], pv[BLOCK:])
                m_b = jnp.where(low, ms[0], ms[1])
                l_b = jnp.where(low, ls[0], ls[1])
                if order == "first":
                    _store_rows(acc_ref, q_chunks, lanes, out)
                    _store_rows(m_ref, q_chunks, lanes, m_b)
                    _store_rows(l_ref, q_chunks, lanes, l_b)
                else:
                    m_o = _load_rows(m_ref, q_chunks, lanes)
                    m_n = jnp.maximum(m_o, m_b)
                    a_o = jnp.exp(m_o - m_n)
                    a_b = jnp.exp(m_b - m_n)
                    acc = a_o * _load_rows(acc_ref, q_chunks, lanes) + a_b * out
                    l_n = a_o * _load_rows(l_ref, q_chunks, lanes) + a_b * l_b
                    if order == "last":
                        _store_rows(o_ref, q_chunks, lanes, acc / l_n)
                    else:
                        _store_rows(acc_ref, q_chunks, lanes, acc)
                        _store_rows(m_ref, q_chunks, lanes, m_n)
                        _store_rows(l_ref, q_chunks, lanes, l_n)
            return carry

        lax.fori_loop(0, TILE // BLOCK, body, 0, unroll=ATTN_UNROLL)

    run_branch(0, "first")
    run_branch(1, "middle")
    run_branch(2, "last")


def _post_kernel(x_ref, p_ref, ya_ref, yb_ref, nb_ref, woa_ref, wob_ref, gf_ref, wg_ref, wu_ref,
                 wd_ref, gp_ref, wpg_ref, wpp_ref, gz_ref, o_ref, slab_ref, act_ref):
    for r in range(N_RES):
        for c in range(WIDTH_B // LANES):
            slab_ref[c, pl.ds(r, M_PER_STEP, stride=N_RES), :] = yb_ref[r, :, c * LANES:(c + 1) * LANES]
    yb = jnp.concatenate([slab_ref[c] for c in range(WIDTH_B // LANES)], axis=1)
    ybn = _rms(yb, nb_ref[...]).astype(BF16)
    h = x_ref[...] + (_dot(ya_ref[...], woa_ref[...]) + _dot(ybn, wob_ref[...]))

    hn = _rms(h, gf_ref[...]).astype(BF16)
    for c in range(D_FF // FF_CHUNK):
        cols = slice(c * FF_CHUNK, (c + 1) * FF_CHUNK)
        gate = _dot(hn, wg_ref[:, cols])
        act_ref[:, cols] = (jax.nn.silu(gate) * _dot(hn, wu_ref[:, cols])).astype(BF16)
    h = h + _dot(act_ref[...], wd_ref[...])

    gate = jax.nn.sigmoid(_dot(_rms(h, gp_ref[...]).astype(BF16), wpg_ref[...]))
    h = h + gate * _dot(p_ref[...].astype(BF16), wpp_ref[...])
    o_ref[...] = _rms(h, gz_ref[...])


def _params(vmem_mib, n_axes):
    return pltpu.CompilerParams(dimension_semantics=("arbitrary",) * n_axes,
                                vmem_limit_bytes=vmem_mib * 1024 * 1024)


def kernel(x, p, mix_norm_g, w_in, sgu_w, sgu_b, sgu_norm_g, out_norm_a, out_norm_b, w_out,
           ffn_norm_g, w_gate, w_up, w_down, ple_norm_g, w_ple_gate, w_ple_proj, final_norm_g):
    b, s, d = x.shape
    assert d == D_MODEL and s % TILE == 0 and w_in.shape[0] == 1
    n_tiles = s // TILE
    n_tok = b * s
    steps_per_seq = s // ROWS
    steps_per_tile = TILE // ROWS
    row = lambda a: a.reshape(1, -1)

    w_in0 = w_in[0].astype(BF16)
    w_uv = w_in0[:, :2 * WIDTH_A]
    w_q = w_in0[:, 2 * WIDTH_A:2 * WIDTH_A + WIDTH_B]
    w_k = w_in0[:, 2 * WIDTH_A + WIDTH_B:2 * WIDTH_A + 2 * WIDTH_B]
    w_v = w_in0[:, 2 * WIDTH_A + 2 * WIDTH_B:]
    g_mix = row(mix_norm_g[0])
    x2 = x.reshape(n_tok, D_MODEL)
    tok_spec = lambda width: pl.BlockSpec((ROWS, width), lambda i: (i, 0))

    sgu_bias = jnp.repeat(sgu_b[0].T, HEAD_DIM, axis=1)
    y_a = pl.pallas_call(
        _sgu_kernel,
        grid=(n_tok // ROWS,),
        in_specs=[
            tok_spec(D_MODEL),
            _resident((1, D_MODEL)),
            _resident((D_MODEL, 2 * WIDTH_A)),
            _resident((N_HEADS_A, CHUNK, CHUNK)),
            _resident((CHUNK, WIDTH_A)),
            _resident((1, WIDTH_A)),
            _resident((1, WIDTH_A)),
        ],
        out_specs=tok_spec(WIDTH_A),
        out_shape=jax.ShapeDtypeStruct((n_tok, WIDTH_A), BF16),
        compiler_params=_params(32, 1),
    )(x2, g_mix, w_uv, sgu_w[0], sgu_bias, row(sgu_norm_g[0]), row(out_norm_a[0]))

    half = HEAD_DIM // 2
    inv = ROPE_THETA ** (-(jnp.arange(LANES) % half).astype(F32) / half)
    rm_shape = jax.ShapeDtypeStruct((b, n_tiles, N_RES, BLOCK, WIDTH_B), F32)
    slab = pltpu.VMEM((WIDTH_B // LANES, ROWS, LANES), F32)
    q, k, v = pl.pallas_call(
        _qkv_kernel,
        grid=(b, steps_per_seq),
        in_specs=[pl.BlockSpec((ROWS, D_MODEL), lambda bi, i: (bi * steps_per_seq + i, 0)),
                  _resident((1, D_MODEL)), _resident((1, LANES)),
                  _resident((D_MODEL, WIDTH_B)), _resident((D_MODEL, WIDTH_B)),
                  _resident((D_MODEL, WIDTH_B))],
        out_specs=[pl.BlockSpec((None, None, N_RES, M_PER_STEP, WIDTH_B),
                                lambda bi, i: (bi, i // steps_per_tile, 0, i % steps_per_tile, 0))] * 3,
        out_shape=[rm_shape] * 3,
        scratch_shapes=[pltpu.VMEM((ROWS, LANES), F32), pltpu.VMEM((ROWS, LANES), F32), slab, slab, slab],
        compiler_params=_params(40, 2),
    )(x2, g_mix, row(inv), w_q, w_k, w_v)

    step_lanes = PAIRS_PER_STEP * LANES
    tile_shape = (b, n_tiles, TILE, WIDTH_B)
    att_spec = pl.BlockSpec((None, None, TILE, step_lanes), lambda bi, ci, ji: (bi, ji, 0, ci))
    y_b = pl.pallas_call(
        _attn_kernel,
        grid=(b, WIDTH_B // step_lanes, n_tiles),
        in_specs=[att_spec, att_spec, att_spec],
        out_specs=att_spec,
        out_shape=jax.ShapeDtypeStruct(tile_shape, F32),
        scratch_shapes=[pltpu.VMEM((2 * TILE, step_lanes), F32), pltpu.VMEM((2 * TILE, step_lanes), F32),
                        pltpu.VMEM((TILE, step_lanes), F32), pltpu.VMEM((TILE, step_lanes), F32),
                        pltpu.VMEM((TILE, step_lanes), F32),
                        pltpu.VMEM((len(DILATIONS), BLOCK, 2 * BLOCK), F32)],
        compiler_params=_params(48, 3),
    )(q.reshape(tile_shape), k.reshape(tile_shape), v.reshape(tile_shape))

    w_out0 = w_out[0].astype(BF16)
    out = pl.pallas_call(
        _post_kernel,
        grid=(n_tok // ROWS,),
        in_specs=[
            tok_spec(D_MODEL), tok_spec(PLE_DIM), tok_spec(WIDTH_A),
            pl.BlockSpec((None, None, N_RES, M_PER_STEP, WIDTH_B),
                         lambda i: (i // steps_per_seq, (i % steps_per_seq) // steps_per_tile, 0,
                                    i % steps_per_tile, 0)),
            _resident((1, WIDTH_B)),
            _resident((WIDTH_A, D_MODEL)), _resident((WIDTH_B, D_MODEL)),
            _resident((1, D_MODEL)),
            _resident((D_MODEL, D_FF)), _resident((D_MODEL, D_FF)), _resident((D_FF, D_MODEL)),
            _resident((1, D_MODEL)),
            _resident((D_MODEL, D_MODEL)), _resident((PLE_DIM, D_MODEL)),
            _resident((1, D_MODEL)),
        ],
        out_specs=tok_spec(D_MODEL),
        out_shape=jax.ShapeDtypeStruct((n_tok, D_MODEL), x.dtype),
        scratch_shapes=[slab, pltpu.VMEM((ROWS, D_FF), BF16)],
        compiler_params=_params(56, 1),
    )(x2, p.reshape(n_tok, PLE_DIM), y_a, y_b.reshape(b, n_tiles, N_RES, BLOCK, WIDTH_B),
      row(out_norm_b[0]), w_out0[:WIDTH_A], w_out0[WIDTH_A:], row(ffn_norm_g[0]),
      w_gate[0].astype(BF16), w_up[0].astype(BF16), w_down[0].astype(BF16),
      row(ple_norm_g[0]), w_ple_gate[0].astype(BF16), w_ple_proj[0].astype(BF16),
      row(final_norm_g))
    return out.reshape(b, s, D_MODEL)
```

```python
import jax
import jax.numpy as jnp
from jax import lax
from jax.experimental import pallas as pl
from jax.experimental.pallas import tpu as pltpu

D_MODEL = 1024
HEAD_DIM = 64
N_HEADS_A = 4
WIDTH_A = 256
WIDTH_B = 768
CHUNK = 128
BLOCK = 128
DILATIONS = (1, 4, 16)
N_RES = 16
TILE = N_RES * BLOCK
D_FF = 2816
PLE_DIM = 256
EPS = 1e-6
ROPE_THETA = 10000.0
LANES = 128
SUBLANES = 8

ROWS = 512
M_PER_STEP = ROWS // N_RES
FF_CHUNK = 256
PAIRS_PER_STEP = 2
PIECE = 32
LOG2_E = 1.4426950408889634

F32 = jnp.float32
BF16 = jnp.bfloat16


def _rms(x, g):
    return x * lax.rsqrt(jnp.mean(x * x, axis=-1, keepdims=True) + EPS) * g


def _dot(a, b):
    return jnp.dot(a, b, preferred_element_type=F32)


def _resident(shape):
    return pl.BlockSpec(shape, lambda *_: (0,) * len(shape), pipeline_mode=pl.Buffered(1))


def _sgu_kernel(x_ref, g_ref, wuv_ref, sw_ref, sb_ref, sg_ref, na_ref, o_ref):
    hn = _rms(x_ref[...], g_ref[...]).astype(BF16)
    uv = _dot(hn, wuv_ref[...])
    u = jax.nn.gelu(uv[:, :WIDTH_A])
    v = jax.nn.gelu(uv[:, WIDTH_A:])
    mu = jnp.mean(v, axis=-1, keepdims=True)
    d = v - mu
    var = jnp.mean(d * d, axis=-1, keepdims=True)
    vf = d * lax.rsqrt(var + EPS) * sg_ref[...]

    row = lax.broadcasted_iota(jnp.int32, (CHUNK, CHUNK), 0)
    col = lax.broadcasted_iota(jnp.int32, (CHUNK, CHUNK), 1)
    w4 = jnp.concatenate(
        [jnp.where(col <= row, sw_ref[h], 0.0) for h in range(N_HEADS_A)], axis=0).astype(BF16)
    lane = lax.broadcasted_iota(jnp.int32, (CHUNK, WIDTH_A), 1)
    for c in range(ROWS // CHUNK):
        rows = slice(c * CHUNK, (c + 1) * CHUNK)
        mm = _dot(w4, vf[rows].astype(BF16))
        mixed = mm[3 * CHUNK:]
        for h in (2, 1, 0):
            mixed = jnp.where(lane < (h + 1) * HEAD_DIM, mm[h * CHUNK:(h + 1) * CHUNK], mixed)
        ya = u[rows] * (mixed + sb_ref[...])
        o_ref[rows, :] = _rms(ya, na_ref[...]).astype(o_ref.dtype)


def _to_residue_major(slab_ref, out_ref):
    for r in range(N_RES):
        for c in range(WIDTH_B // LANES):
            out_ref[r, :, c * LANES:(c + 1) * LANES] = slab_ref[c, pl.ds(r, M_PER_STEP, stride=N_RES), :]


def _qkv_kernel(x_ref, g_ref, inv_ref, wq_ref, wk_ref, wv_ref, q_ref, k_ref, v_ref,
                cu_ref, su_ref, qs_ref, ks_ref, vs_ref):
    i = pl.program_id(1)
    lane = lax.broadcasted_iota(jnp.int32, (1, LANES), 1)
    sign = jnp.where((lane % HEAD_DIM) < (HEAD_DIM // 2), -1.0, 1.0)

    @pl.when((pl.program_id(0) == 0) & (i == 0))
    def _():
        u = lax.broadcasted_iota(jnp.int32, (ROWS, LANES), 0).astype(F32)
        ang = u * inv_ref[...]
        cu_ref[...] = jnp.cos(ang)
        su_ref[...] = jnp.sin(ang) * sign

    ang0 = (i * ROWS).astype(F32) * inv_ref[...]
    ct = jnp.cos(ang0)
    st = jnp.sin(ang0) * sign
    cos = ct * cu_ref[...] - st * su_ref[...]
    sin = st * cu_ref[...] + ct * su_ref[...]
    first_half = (lax.broadcasted_iota(jnp.int32, (ROWS, LANES), 1) % HEAD_DIM) < (HEAD_DIM // 2)

    hn = _rms(x_ref[...], g_ref[...]).astype(BF16)

    def project(w_ref, slab_ref, out_ref, rope, scale):
        t = _dot(hn, w_ref[...])
        for c in range(WIDTH_B // LANES):
            tc = t[:, c * LANES:(c + 1) * LANES]
            if rope:
                rot = jnp.where(first_half,
                                pltpu.roll(tc, LANES - HEAD_DIM // 2, 1),
                                pltpu.roll(tc, HEAD_DIM // 2, 1))
                tc = tc * cos + rot * sin
            if scale != 1.0:
                tc = tc * scale
            slab_ref[c] = tc
        _to_residue_major(slab_ref, out_ref)

    project(wq_ref, qs_ref, q_ref, True, HEAD_DIM ** -0.5 * LOG2_E)
    project(wk_ref, ks_ref, k_ref, True, 1.0)
    project(wv_ref, vs_ref, v_ref, False, 1.0)


def _branch_chunks(dilation, base, idx):
    if dilation == 16:
        return [(base + idx * BLOCK, BLOCK)]
    if dilation == 4:
        r4, jj = idx % 4, idx // 4
        return [(base + (4 * c + r4) * BLOCK + 32 * jj, 32) for c in range(4)]
    return [(base + r * BLOCK + SUBLANES * idx, SUBLANES) for r in range(N_RES)]


def _prev_block(dilation, idx, cur, prv):
    if dilation == 16:
        return prv, idx
    n_per_res = N_RES // dilation
    if dilation == 4:
        r4, jj = idx % 4, idx // 4
        base = jnp.where(jj > 0, cur, prv)
        return base, r4 + 4 * ((jj + n_per_res - 1) % n_per_res)
    base = jnp.where(idx > 0, cur, prv)
    return base, (idx + n_per_res - 1) % n_per_res


def _sub_index(dilation, a):
    if dilation == 16:
        return a
    if dilation == 4:
        return 4 * (a % 32) + a // 32
    return N_RES * (a % SUBLANES) + a // SUBLANES


def _rows(start, n):
    return pl.ds(start if isinstance(start, int) else pl.multiple_of(start, SUBLANES), n)


def _load_rows(ref, chunks, lanes):
    parts = [ref[_rows(s, n), lanes] for s, n in chunks]
    return parts[0] if len(parts) == 1 else jnp.concatenate(parts, axis=0)


def _store_rows(ref, chunks, lanes, val):
    off = 0
    for s, n in chunks:
        ref[_rows(s, n), lanes] = val[off:off + n]
        off += n


def _row_pieces(chunks, piece):
    flat = []
    for s, n in chunks:
        step = min(n, piece)
        flat += [(s + o, step) for o in range(0, n, step)]
    per_group = piece // flat[0][1]
    return [flat[g:g + per_group] for g in range(0, len(flat), per_group)]


def _attn_kernel(q_ref, k_ref, v_ref, o_ref, kk_ref, vv_ref, acc_ref, m_ref, l_ref, bias_ref,
                 s_ref, p_ref, mb_ref, lb_ref):
    j = pl.program_id(2)
    ring = j % 2
    cur = pl.multiple_of(ring * TILE, TILE)
    prv = pl.multiple_of((1 - ring) * TILE, TILE)

    @pl.when(j == 0)
    def _():
        kk_ref[pl.ds(prv, TILE), :] = jnp.zeros((TILE, kk_ref.shape[1]), F32)
        vv_ref[pl.ds(prv, TILE), :] = jnp.zeros((TILE, vv_ref.shape[1]), F32)

    kk_ref[pl.ds(cur, TILE), :] = k_ref[...]
    vv_ref[pl.ds(cur, TILE), :] = v_ref[...]

    qa = lax.broadcasted_iota(jnp.int32, (BLOCK, 2 * BLOCK), 0)
    kc = lax.broadcasted_iota(jnp.int32, (BLOCK, 2 * BLOCK), 1)
    for bi, dilation in enumerate(DILATIONS):
        dist = (BLOCK + _sub_index(dilation, qa)
                - _sub_index(dilation, kc % BLOCK) - BLOCK * (kc // BLOCK))
        band = (dist >= 0) & (dist <= BLOCK)
        bias_ref[2 * bi] = jnp.where(band, 0.0, -jnp.inf)
        bias_ref[2 * bi + 1] = jnp.where(band & (kc >= BLOCK), 0.0, -jnp.inf)

    n_blk = TILE // BLOCK
    n_units = len(DILATIONS) * n_blk
    pair_lanes = [slice(p * LANES, (p + 1) * LANES) for p in range(PAIRS_PER_STEP)]

    n_pieces = BLOCK // PIECE

    def key_chunks(bi, idx):
        dilation = DILATIONS[bi]
        pbase, pidx = _prev_block(dilation, idx, cur, prv)
        return _branch_chunks(dilation, pbase, pidx), _branch_chunks(dilation, cur, idx)

    def scores(bi, idx, slot):
        q_chunks = _branch_chunks(DILATIONS[bi], 0, idx)
        kp_chunks, kc_chunks = key_chunks(bi, idx)
        low = lax.broadcasted_iota(jnp.int32, (BLOCK, LANES), 1) < HEAD_DIM
        for p, lanes in enumerate(pair_lanes):
            q = _load_rows(q_ref, q_chunks, lanes)
            lhs = jnp.concatenate([jnp.where(low, q, 0.0), jnp.where(low, 0.0, q)],
                                  axis=0).astype(BF16)
            kb = jnp.concatenate([_load_rows(kk_ref, kp_chunks, lanes),
                                  _load_rows(kk_ref, kc_chunks, lanes)], axis=0).astype(BF16)
            s_ref[slot, p] = lax.dot_general(lhs, kb, (((1,), (1,)), ((), ())),
                                             preferred_element_type=F32)

    def softmax(bi, idx, slot):
        dilation = DILATIONS[bi]
        if dilation == 16:
            first = j == 0
        elif dilation == 4:
            first = (j == 0) & (idx // 4 == 0)
        else:
            first = (j == 0) & (idx == 0)
        table = 2 * bi + first.astype(jnp.int32)
        low = lax.broadcasted_iota(jnp.int32, (PIECE, LANES), 1) < HEAD_DIM
        for p in range(PAIRS_PER_STEP):
            for c in range(n_pieces):
                rows = slice(c * PIECE, (c + 1) * PIECE)
                bias = bias_ref[table, rows, :]
                ms, ls = [], []
                for h in range(2):
                    head_rows = slice(h * BLOCK + c * PIECE, h * BLOCK + (c + 1) * PIECE)
                    sh = s_ref[slot, p, head_rows, :] + bias
                    mh = jnp.max(sh, axis=-1, keepdims=True)
                    ph = jnp.exp2(sh - mh)
                    ls.append(jnp.sum(ph, axis=-1, keepdims=True))
                    ms.append(mh)
                    p_ref[slot, p, head_rows, :] = ph.astype(BF16)
                mb_ref[slot, p, rows, :] = jnp.where(low, ms[0], ms[1])
                lb_ref[slot, p, rows, :] = jnp.where(low, ls[0], ls[1])

    def combine(bi, idx, slot):
        row_groups = _row_pieces(_branch_chunks(DILATIONS[bi], 0, idx), PIECE)
        kp_chunks, kc_chunks = key_chunks(bi, idx)
        low = lax.broadcasted_iota(jnp.int32, (PIECE, LANES), 1) < HEAD_DIM
        for p, lanes in enumerate(pair_lanes):
            vb = jnp.concatenate([_load_rows(vv_ref, kp_chunks, lanes),
                                  _load_rows(vv_ref, kc_chunks, lanes)], axis=0).astype(BF16)
            pv = _dot(p_ref[slot, p], vb)
            for c, chunks in enumerate(row_groups):
                rows = slice(c * PIECE, (c + 1) * PIECE)
                out = jnp.where(low, pv[c * PIECE:(c + 1) * PIECE],
                                pv[BLOCK + c * PIECE:BLOCK + (c + 1) * PIECE])
                m_b = mb_ref[slot, p, rows, :]
                l_b = lb_ref[slot, p, rows, :]
                if bi == 0:
                    _store_rows(acc_ref, chunks, lanes, out)
                    _store_rows(m_ref, chunks, lanes, m_b)
                    _store_rows(l_ref, chunks, lanes, l_b)
                    continue
                m_o = _load_rows(m_ref, chunks, lanes)
                m_n = jnp.maximum(m_o, m_b)
                a_o = jnp.exp2(m_o - m_n)
                a_b = jnp.exp2(m_b - m_n)
                acc = a_o * _load_rows(acc_ref, chunks, lanes) + a_b * out
                l_n = a_o * _load_rows(l_ref, chunks, lanes) + a_b * l_b
                if bi == len(DILATIONS) - 1:
                    _store_rows(o_ref, chunks, lanes, acc / l_n)
                else:
                    _store_rows(acc_ref, chunks, lanes, acc)
                    _store_rows(m_ref, chunks, lanes, m_n)
                    _store_rows(l_ref, chunks, lanes, l_n)

    def step(slot, bi_c, idx_c, bi_m, idx_m, bi_s, idx_s):
        combine(bi_c, idx_c, slot)
        if bi_m is not None:
            softmax(bi_m, idx_m, 1 - slot)
        if bi_s is not None:
            scores(bi_s, idx_s, slot)

    unit = lambda g: (g // n_blk, g % n_blk) if g < n_units else (None, None)
    scores(*unit(0), 0)
    softmax(*unit(0), 0)
    scores(*unit(1), 1)
    for bi in range(len(DILATIONS)):
        def body(i, carry, bi=bi):
            for slot in range(2):
                idx = 2 * i + slot
                step(slot, bi, idx, bi, idx + 1, bi, idx + 2)
            return carry

        lax.fori_loop(0, (n_blk - 2) // 2, body, 0)
        for g in range((bi + 1) * n_blk - 2, (bi + 1) * n_blk):
            step(g % 2, *unit(g), *unit(g + 1), *unit(g + 2))


def _post_kernel(x_ref, p_ref, ya_ref, yb_ref, nb_ref, woa_ref, wob_ref, gf_ref, wg_ref, wu_ref,
                 wd_ref, gp_ref, wpg_ref, wpp_ref, gz_ref, o_ref, slab_ref, act_ref):
    for r in range(N_RES):
        for c in range(WIDTH_B // LANES):
            slab_ref[c, pl.ds(r, M_PER_STEP, stride=N_RES), :] = yb_ref[r, :, c * LANES:(c + 1) * LANES]
    yb = jnp.concatenate([slab_ref[c] for c in range(WIDTH_B // LANES)], axis=1)
    ybn = _rms(yb, nb_ref[...]).astype(BF16)
    h = x_ref[...] + (_dot(ya_ref[...], woa_ref[...]) + _dot(ybn, wob_ref[...]))

    hn = _rms(h, gf_ref[...]).astype(BF16)
    for c in range(D_FF // FF_CHUNK):
        cols = slice(c * FF_CHUNK, (c + 1) * FF_CHUNK)
        gate = _dot(hn, wg_ref[:, cols])
        act_ref[:, cols] = (jax.nn.silu(gate) * _dot(hn, wu_ref[:, cols])).astype(BF16)
    h = h + _dot(act_ref[...], wd_ref[...])

    gate = jax.nn.sigmoid(_dot(_rms(h, gp_ref[...]).astype(BF16), wpg_ref[...]))
    h = h + gate * _dot(p_ref[...].astype(BF16), wpp_ref[...])
    o_ref[...] = _rms(h, gz_ref[...])


def _params(vmem_mib, n_axes):
    return pltpu.CompilerParams(dimension_semantics=("arbitrary",) * n_axes,
                                vmem_limit_bytes=vmem_mib * 1024 * 1024)


def kernel(x, p, mix_norm_g, w_in, sgu_w, sgu_b, sgu_norm_g, out_norm_a, out_norm_b, w_out,
           ffn_norm_g, w_gate, w_up, w_down, ple_norm_g, w_ple_gate, w_ple_proj, final_norm_g):
    b, s, d = x.shape
    assert d == D_MODEL and s % TILE == 0 and w_in.shape[0] == 1
    n_tiles = s // TILE
    n_tok = b * s
    steps_per_seq = s // ROWS
    steps_per_tile = TILE // ROWS
    row = lambda a: a.reshape(1, -1)

    w_in0 = w_in[0].astype(BF16)
    w_uv = w_in0[:, :2 * WIDTH_A]
    w_q = w_in0[:, 2 * WIDTH_A:2 * WIDTH_A + WIDTH_B]
    w_k = w_in0[:, 2 * WIDTH_A + WIDTH_B:2 * WIDTH_A + 2 * WIDTH_B]
    w_v = w_in0[:, 2 * WIDTH_A + 2 * WIDTH_B:]
    g_mix = row(mix_norm_g[0])
    x2 = x.reshape(n_tok, D_MODEL)
    tok_spec = lambda width: pl.BlockSpec((ROWS, width), lambda i: (i, 0))

    sgu_bias = jnp.repeat(sgu_b[0].T, HEAD_DIM, axis=1)
    y_a = pl.pallas_call(
        _sgu_kernel,
        grid=(n_tok // ROWS,),
        in_specs=[
            tok_spec(D_MODEL),
            _resident((1, D_MODEL)),
            _resident((D_MODEL, 2 * WIDTH_A)),
            _resident((N_HEADS_A, CHUNK, CHUNK)),
            _resident((CHUNK, WIDTH_A)),
            _resident((1, WIDTH_A)),
            _resident((1, WIDTH_A)),
        ],
        out_specs=tok_spec(WIDTH_A),
        out_shape=jax.ShapeDtypeStruct((n_tok, WIDTH_A), BF16),
        compiler_params=_params(32, 1),
    )(x2, g_mix, w_uv, sgu_w[0], sgu_bias, row(sgu_norm_g[0]), row(out_norm_a[0]))

    half = HEAD_DIM // 2
    inv = ROPE_THETA ** (-(jnp.arange(LANES) % half).astype(F32) / half)
    rm_shape = jax.ShapeDtypeStruct((b, n_tiles, N_RES, BLOCK, WIDTH_B), F32)
    slab = pltpu.VMEM((WIDTH_B // LANES, ROWS, LANES), F32)
    q, k, v = pl.pallas_call(
        _qkv_kernel,
        grid=(b, steps_per_seq),
        in_specs=[pl.BlockSpec((ROWS, D_MODEL), lambda bi, i: (bi * steps_per_seq + i, 0)),
                  _resident((1, D_MODEL)), _resident((1, LANES)),
                  _resident((D_MODEL, WIDTH_B)), _resident((D_MODEL, WIDTH_B)),
                  _resident((D_MODEL, WIDTH_B))],
        out_specs=[pl.BlockSpec((None, None, N_RES, M_PER_STEP, WIDTH_B),
                                lambda bi, i: (bi, i // steps_per_tile, 0, i % steps_per_tile, 0))] * 3,
        out_shape=[rm_shape] * 3,
        scratch_shapes=[pltpu.VMEM((ROWS, LANES), F32), pltpu.VMEM((ROWS, LANES), F32), slab, slab, slab],
        compiler_params=_params(40, 2),
    )(x2, g_mix, row(inv), w_q, w_k, w_v)

    step_lanes = PAIRS_PER_STEP * LANES
    tile_shape = (b, n_tiles, TILE, WIDTH_B)
    att_spec = pl.BlockSpec((None, None, TILE, step_lanes), lambda bi, ci, ji: (bi, ji, 0, ci))
    y_b = pl.pallas_call(
        _attn_kernel,
        grid=(b, WIDTH_B // step_lanes, n_tiles),
        in_specs=[att_spec, att_spec, att_spec],
        out_specs=att_spec,
        out_shape=jax.ShapeDtypeStruct(tile_shape, F32),
        scratch_shapes=[pltpu.VMEM((2 * TILE, step_lanes), F32), pltpu.VMEM((2 * TILE, step_lanes), F32),
                        pltpu.VMEM((TILE, step_lanes), F32), pltpu.VMEM((TILE, step_lanes), F32),
                        pltpu.VMEM((TILE, step_lanes), F32),
                        pltpu.VMEM((2 * len(DILATIONS), BLOCK, 2 * BLOCK), F32),
                        pltpu.VMEM((2, PAIRS_PER_STEP, 2 * BLOCK, 2 * BLOCK), F32),
                        pltpu.VMEM((2, PAIRS_PER_STEP, 2 * BLOCK, 2 * BLOCK), BF16),
                        pltpu.VMEM((2, PAIRS_PER_STEP, BLOCK, LANES), F32),
                        pltpu.VMEM((2, PAIRS_PER_STEP, BLOCK, LANES), F32)],
        compiler_params=_params(48, 3),
    )(q.reshape(tile_shape), k.reshape(tile_shape), v.reshape(tile_shape))

    w_out0 = w_out[0].astype(BF16)
    out = pl.pallas_call(
        _post_kernel,
        grid=(n_tok // ROWS,),
        in_specs=[
            tok_spec(D_MODEL), tok_spec(PLE_DIM), tok_spec(WIDTH_A),
            pl.BlockSpec((None, None, N_RES, M_PER_STEP, WIDTH_B),
                         lambda i: (i // steps_per_seq, (i % steps_per_seq) // steps_per_tile, 0,
                                    i % steps_per_tile, 0)),
            _resident((1, WIDTH_B)),
            _resident((WIDTH_A, D_MODEL)), _resident((WIDTH_B, D_MODEL)),
            _resident((1, D_MODEL)),
            _resident((D_MODEL, D_FF)), _resident((D_MODEL, D_FF)), _resident((D_FF, D_MODEL)),
            _resident((1, D_MODEL)),
            _resident((D_MODEL, D_MODEL)), _resident((PLE_DIM, D_MODEL)),
            _resident((1, D_MODEL)),
        ],
        out_specs=tok_spec(D_MODEL),
        out_shape=jax.ShapeDtypeStruct((n_tok, D_MODEL), x.dtype),
        scratch_shapes=[slab, pltpu.VMEM((ROWS, D_FF), BF16)],
        compiler_params=_params(56, 1),
    )(x2, p.reshape(n_tok, PLE_DIM), y_a, y_b.reshape(b, n_tiles, N_RES, BLOCK, WIDTH_B),
      row(out_norm_b[0]), w_out0[:WIDTH_A], w_out0[WIDTH_A:], row(ffn_norm_g[0]),
      w_gate[0].astype(BF16), w_up[0].astype(BF16), w_down[0].astype(BF16),
      row(ple_norm_g[0]), w_ple_gate[0].astype(BF16), w_ple_proj[0].astype(BF16),
      row(final_norm_g))
    return out.reshape(b, s, D_MODEL)
```

```python
import jax
import jax.numpy as jnp
from jax import lax
from jax.experimental import pallas as pl
from jax.experimental.pallas import tpu as pltpu

D_MODEL = 1024
HEAD_DIM = 64
N_HEADS_A = 4
WIDTH_A = 256
WIDTH_B = 768
CHUNK = 128
BLOCK = 128
DILATIONS = (1, 4, 16)
N_RES = 16
TILE = N_RES * BLOCK
D_FF = 2816
PLE_DIM = 256
EPS = 1e-6
ROPE_THETA = 10000.0
LANES = 128
SUBLANES = 8

ROWS = 512
M_PER_STEP = ROWS // N_RES
FF_CHUNK = 256
PAIRS_PER_STEP = 2
PIECE = 32
LOG2_E = 1.4426950408889634

F32 = jnp.float32
BF16 = jnp.bfloat16


def _rms(x, g):
    return x * lax.rsqrt(jnp.mean(x * x, axis=-1, keepdims=True) + EPS) * g


def _dot(a, b):
    return jnp.dot(a, b, preferred_element_type=F32)


def _resident(shape):
    return pl.BlockSpec(shape, lambda *_: (0,) * len(shape), pipeline_mode=pl.Buffered(1))


def _sgu_kernel(x_ref, g_ref, wuv_ref, sw_ref, sb_ref, sg_ref, na_ref, o_ref):
    hn = _rms(x_ref[...], g_ref[...]).astype(BF16)
    uv = _dot(hn, wuv_ref[...])
    u = jax.nn.gelu(uv[:, :WIDTH_A])
    v = jax.nn.gelu(uv[:, WIDTH_A:])
    mu = jnp.mean(v, axis=-1, keepdims=True)
    d = v - mu
    var = jnp.mean(d * d, axis=-1, keepdims=True)
    vf = d * lax.rsqrt(var + EPS) * sg_ref[...]

    row = lax.broadcasted_iota(jnp.int32, (CHUNK, CHUNK), 0)
    col = lax.broadcasted_iota(jnp.int32, (CHUNK, CHUNK), 1)
    w4 = jnp.concatenate(
        [jnp.where(col <= row, sw_ref[h], 0.0) for h in range(N_HEADS_A)], axis=0).astype(BF16)
    lane = lax.broadcasted_iota(jnp.int32, (CHUNK, WIDTH_A), 1)
    for c in range(ROWS // CHUNK):
        rows = slice(c * CHUNK, (c + 1) * CHUNK)
        mm = _dot(w4, vf[rows].astype(BF16))
        mixed = mm[3 * CHUNK:]
        for h in (2, 1, 0):
            mixed = jnp.where(lane < (h + 1) * HEAD_DIM, mm[h * CHUNK:(h + 1) * CHUNK], mixed)
        ya = u[rows] * (mixed + sb_ref[...])
        o_ref[rows, :] = _rms(ya, na_ref[...]).astype(o_ref.dtype)


def _to_residue_major(slab_ref, out_ref):
    for r in range(N_RES):
        for c in range(WIDTH_B // LANES):
            out_ref[r, :, c * LANES:(c + 1) * LANES] = slab_ref[c, pl.ds(r, M_PER_STEP, stride=N_RES), :]


def _qkv_kernel(x_ref, g_ref, inv_ref, wq_ref, wk_ref, wv_ref, q_ref, k_ref, v_ref,
                cu_ref, su_ref, qs_ref, ks_ref, vs_ref):
    i = pl.program_id(1)
    lane = lax.broadcasted_iota(jnp.int32, (1, LANES), 1)
    sign = jnp.where((lane % HEAD_DIM) < (HEAD_DIM // 2), -1.0, 1.0)

    @pl.when((pl.program_id(0) == 0) & (i == 0))
    def _():
        u = lax.broadcasted_iota(jnp.int32, (ROWS, LANES), 0).astype(F32)
        ang = u * inv_ref[...]
        cu_ref[...] = jnp.cos(ang)
        su_ref[...] = jnp.sin(ang) * sign

    ang0 = (i * ROWS).astype(F32) * inv_ref[...]
    ct = jnp.cos(ang0)
    st = jnp.sin(ang0) * sign
    cos = ct * cu_ref[...] - st * su_ref[...]
    sin = st * cu_ref[...] + ct * su_ref[...]
    first_half = (lax.broadcasted_iota(jnp.int32, (ROWS, LANES), 1) % HEAD_DIM) < (HEAD_DIM // 2)

    hn = _rms(x_ref[...], g_ref[...]).astype(BF16)

    def project(w_ref, slab_ref, out_ref, rope, scale):
        t = _dot(hn, w_ref[...])
        for c in range(WIDTH_B // LANES):
            tc = t[:, c * LANES:(c + 1) * LANES]
            if rope:
                rot = jnp.where(first_half,
                                pltpu.roll(tc, LANES - HEAD_DIM // 2, 1),
                                pltpu.roll(tc, HEAD_DIM // 2, 1))
                tc = tc * cos + rot * sin
            if scale != 1.0:
                tc = tc * scale
            slab_ref[c] = tc
        _to_residue_major(slab_ref, out_ref)

    project(wq_ref, qs_ref, q_ref, True, HEAD_DIM ** -0.5 * LOG2_E)
    project(wk_ref, ks_ref, k_ref, True, 1.0)
    project(wv_ref, vs_ref, v_ref, False, 1.0)


def _branch_chunks(dilation, base, idx):
    if dilation == 16:
        return [(base + idx * BLOCK, BLOCK)]
    if dilation == 4:
        r4, jj = idx % 4, idx // 4
        return [(base + (4 * c + r4) * BLOCK + 32 * jj, 32) for c in range(4)]
    return [(base + r * BLOCK + SUBLANES * idx, SUBLANES) for r in range(N_RES)]


def _prev_block(dilation, idx, cur, prv):
    if dilation == 16:
        return prv, idx
    n_per_res = N_RES // dilation
    if dilation == 4:
        r4, jj = idx % 4, idx // 4
        base = jnp.where(jj > 0, cur, prv)
        return base, r4 + 4 * ((jj + n_per_res - 1) % n_per_res)
    base = jnp.where(idx > 0, cur, prv)
    return base, (idx + n_per_res - 1) % n_per_res


def _sub_index(dilation, a):
    if dilation == 16:
        return a
    if dilation == 4:
        return 4 * (a % 32) + a // 32
    return N_RES * (a % SUBLANES) + a // SUBLANES


def _rows(start, n):
    return pl.ds(start if isinstance(start, int) else pl.multiple_of(start, SUBLANES), n)


def _load_rows(ref, chunks, lanes):
    parts = [ref[_rows(s, n), lanes] for s, n in chunks]
    return parts[0] if len(parts) == 1 else jnp.concatenate(parts, axis=0)


def _store_rows(ref, chunks, lanes, val):
    off = 0
    for s, n in chunks:
        ref[_rows(s, n), lanes] = val[off:off + n]
        off += n


def _row_pieces(chunks, piece):
    flat = []
    for s, n in chunks:
        step = min(n, piece)
        flat += [(s + o, step) for o in range(0, n, step)]
    per_group = piece // flat[0][1]
    return [flat[g:g + per_group] for g in range(0, len(flat), per_group)]


def _attn_kernel(q_ref, k_ref, v_ref, o_ref, kk_ref, vv_ref, acc_ref, m_ref, l_ref, bias_ref,
                 s_ref, p_ref, mb_ref, lb_ref):
    j = pl.program_id(2)
    ring = j % 2
    cur = pl.multiple_of(ring * TILE, TILE)
    prv = pl.multiple_of((1 - ring) * TILE, TILE)

    @pl.when(j == 0)
    def _():
        kk_ref[pl.ds(prv, TILE), :] = jnp.zeros((TILE, kk_ref.shape[1]), F32)
        vv_ref[pl.ds(prv, TILE), :] = jnp.zeros((TILE, vv_ref.shape[1]), F32)

    kk_ref[pl.ds(cur, TILE), :] = k_ref[...]
    vv_ref[pl.ds(cur, TILE), :] = v_ref[...]

    qa = lax.broadcasted_iota(jnp.int32, (BLOCK, 2 * BLOCK), 0)
    kc = lax.broadcasted_iota(jnp.int32, (BLOCK, 2 * BLOCK), 1)
    for bi, dilation in enumerate(DILATIONS):
        dist = (BLOCK + _sub_index(dilation, qa)
                - _sub_index(dilation, kc % BLOCK) - BLOCK * (kc // BLOCK))
        band = (dist >= 0) & (dist <= BLOCK)
        bias_ref[2 * bi] = jnp.where(band, 0.0, -jnp.inf)
        bias_ref[2 * bi + 1] = jnp.where(band & (kc >= BLOCK), 0.0, -jnp.inf)

    n_blk = TILE // BLOCK
    n_units = len(DILATIONS) * n_blk
    pair_lanes = [slice(p * LANES, (p + 1) * LANES) for p in range(PAIRS_PER_STEP)]

    n_pieces = BLOCK // PIECE

    def key_chunks(bi, idx):
        dilation = DILATIONS[bi]
        pbase, pidx = _prev_block(dilation, idx, cur, prv)
        return _branch_chunks(dilation, pbase, pidx), _branch_chunks(dilation, cur, idx)

    def scores(bi, idx, slot):
        q_chunks = _branch_chunks(DILATIONS[bi], 0, idx)
        kp_chunks, kc_chunks = key_chunks(bi, idx)
        low = lax.broadcasted_iota(jnp.int32, (BLOCK, LANES), 1) < HEAD_DIM
        for p, lanes in enumerate(pair_lanes):
            q = _load_rows(q_ref, q_chunks, lanes)
            lhs = jnp.concatenate([jnp.where(low, q, 0.0), jnp.where(low, 0.0, q)],
                                  axis=0).astype(BF16)
            kb = jnp.concatenate([_load_rows(kk_ref, kp_chunks, lanes),
                                  _load_rows(kk_ref, kc_chunks, lanes)], axis=0).astype(BF16)
            s_ref[slot, p] = lax.dot_general(lhs, kb, (((1,), (1,)), ((), ())),
                                             preferred_element_type=F32)

    def softmax(bi, idx, slot):
        dilation = DILATIONS[bi]
        if dilation == 16:
            first = j == 0
        elif dilation == 4:
            first = (j == 0) & (idx // 4 == 0)
        else:
            first = (j == 0) & (idx == 0)
        table = 2 * bi + first.astype(jnp.int32)
        low = lax.broadcasted_iota(jnp.int32, (PIECE, LANES), 1) < HEAD_DIM
        for p in range(PAIRS_PER_STEP):
            for c in range(n_pieces):
                rows = slice(c * PIECE, (c + 1) * PIECE)
                bias = bias_ref[table, rows, :]
                ms, ls = [], []
                for h in range(2):
                    head_rows = slice(h * BLOCK + c * PIECE, h * BLOCK + (c + 1) * PIECE)
                    sh = s_ref[slot, p, head_rows, :] + bias
                    mh = jnp.max(sh, axis=-1, keepdims=True)
                    ph = jnp.exp2(sh - mh)
                    ls.append(jnp.sum(ph, axis=-1, keepdims=True))
                    ms.append(mh)
                    p_ref[slot, p, head_rows, :] = ph.astype(BF16)
                mb_ref[slot, p, rows, :] = jnp.where(low, ms[0], ms[1])
                lb_ref[slot, p, rows, :] = jnp.where(low, ls[0], ls[1])

    def combine(bi, idx, slot):
        row_groups = _row_pieces(_branch_chunks(DILATIONS[bi], 0, idx), PIECE)
        kp_chunks, kc_chunks = key_chunks(bi, idx)
        low = lax.broadcasted_iota(jnp.int32, (PIECE, LANES), 1) < HEAD_DIM
        for p, lanes in enumerate(pair_lanes):
            vb = jnp.concatenate([_load_rows(vv_ref, kp_chunks, lanes),
                                  _load_rows(vv_ref, kc_chunks, lanes)], axis=0).astype(BF16)
            pv = _dot(p_ref[slot, p], vb)
            for c, chunks in enumerate(row_groups):
                rows = slice(c * PIECE, (c + 1) * PIECE)
                out = jnp.where(low, pv[c * PIECE:(c + 1) * PIECE],
                                pv[BLOCK + c * PIECE:BLOCK + (c + 1) * PIECE])
                m_b = mb_ref[slot, p, rows, :]
                l_b = lb_ref[slot, p, rows, :]
                if bi == 0:
                    _store_rows(acc_ref, chunks, lanes, out)
                    _store_rows(m_ref, chunks, lanes, m_b)
                    _store_rows(l_ref, chunks, lanes, l_b)
                    continue
                m_o = _load_rows(m_ref, chunks, lanes)
                m_n = jnp.maximum(m_o, m_b)
                a_o = jnp.exp2(m_o - m_n)
                a_b = jnp.exp2(m_b - m_n)
                acc = a_o * _load_rows(acc_ref, chunks, lanes) + a_b * out
                l_n = a_o * _load_rows(l_ref, chunks, lanes) + a_b * l_b
                if bi == len(DILATIONS) - 1:
                    _store_rows(o_ref, chunks, lanes, acc / l_n)
                else:
                    _store_rows(acc_ref, chunks, lanes, acc)
                    _store_rows(m_ref, chunks, lanes, m_n)
                    _store_rows(l_ref, chunks, lanes, l_n)

    def step(slot, bi_c, idx_c, bi_m, idx_m, bi_s, idx_s):
        combine(bi_c, idx_c, slot)
        if bi_m is not None:
            softmax(bi_m, idx_m, 1 - slot)
        if bi_s is not None:
            scores(bi_s, idx_s, slot)

    unit = lambda g: (g // n_blk, g % n_blk) if g < n_units else (None, None)
    scores(*unit(0), 0)
    softmax(*unit(0), 0)
    scores(*unit(1), 1)
    for g in range(n_units):
        step(g % 2, *unit(g), *unit(g + 1), *unit(g + 2))


def _post_kernel(x_ref, p_ref, ya_ref, yb_ref, nb_ref, woa_ref, wob_ref, gf_ref, wg_ref, wu_ref,
                 wd_ref, gp_ref, wpg_ref, wpp_ref, gz_ref, o_ref, slab_ref, act_ref):
    for r in range(N_RES):
        for c in range(WIDTH_B // LANES):
            slab_ref[c, pl.ds(r, M_PER_STEP, stride=N_RES), :] = yb_ref[r, :, c * LANES:(c + 1) * LANES]
    yb = jnp.concatenate([slab_ref[c] for c in range(WIDTH_B // LANES)], axis=1)
    ybn = _rms(yb, nb_ref[...]).astype(BF16)
    h = x_ref[...] + (_dot(ya_ref[...], woa_ref[...]) + _dot(ybn, wob_ref[...]))

    hn = _rms(h, gf_ref[...]).astype(BF16)
    for c in range(D_FF // FF_CHUNK):
        cols = slice(c * FF_CHUNK, (c + 1) * FF_CHUNK)
        gate = _dot(hn, wg_ref[:, cols])
        act_ref[:, cols] = (jax.nn.silu(gate) * _dot(hn, wu_ref[:, cols])).astype(BF16)
    h = h + _dot(act_ref[...], wd_ref[...])

    gate = jax.nn.sigmoid(_dot(_rms(h, gp_ref[...]).astype(BF16), wpg_ref[...]))
    h = h + gate * _dot(p_ref[...].astype(BF16), wpp_ref[...])
    o_ref[...] = _rms(h, gz_ref[...])


def _params(vmem_mib, n_axes):
    return pltpu.CompilerParams(dimension_semantics=("arbitrary",) * n_axes,
                                vmem_limit_bytes=vmem_mib * 1024 * 1024)


def kernel(x, p, mix_norm_g, w_in, sgu_w, sgu_b, sgu_norm_g, out_norm_a, out_norm_b, w_out,
           ffn_norm_g, w_gate, w_up, w_down, ple_norm_g, w_ple_gate, w_ple_proj, final_norm_g):
    b, s, d = x.shape
    assert d == D_MODEL and s % TILE == 0 and w_in.shape[0] == 1
    n_tiles = s // TILE
    n_tok = b * s
    steps_per_seq = s // ROWS
    steps_per_tile = TILE // ROWS
    row = lambda a: a.reshape(1, -1)

    w_in0 = w_in[0].astype(BF16)
    w_uv = w_in0[:, :2 * WIDTH_A]
    w_q = w_in0[:, 2 * WIDTH_A:2 * WIDTH_A + WIDTH_B]
    w_k = w_in0[:, 2 * WIDTH_A + WIDTH_B:2 * WIDTH_A + 2 * WIDTH_B]
    w_v = w_in0[:, 2 * WIDTH_A + 2 * WIDTH_B:]
    g_mix = row(mix_norm_g[0])
    x2 = x.reshape(n_tok, D_MODEL)
    tok_spec = lambda width: pl.BlockSpec((ROWS, width), lambda i: (i, 0))

    sgu_bias = jnp.repeat(sgu_b[0].T, HEAD_DIM, axis=1)
    y_a = pl.pallas_call(
        _sgu_kernel,
        grid=(n_tok // ROWS,),
        in_specs=[
            tok_spec(D_MODEL),
            _resident((1, D_MODEL)),
            _resident((D_MODEL, 2 * WIDTH_A)),
            _resident((N_HEADS_A, CHUNK, CHUNK)),
            _resident((CHUNK, WIDTH_A)),
            _resident((1, WIDTH_A)),
            _resident((1, WIDTH_A)),
        ],
        out_specs=tok_spec(WIDTH_A),
        out_shape=jax.ShapeDtypeStruct((n_tok, WIDTH_A), BF16),
        compiler_params=_params(32, 1),
    )(x2, g_mix, w_uv, sgu_w[0], sgu_bias, row(sgu_norm_g[0]), row(out_norm_a[0]))

    half = HEAD_DIM // 2
    inv = ROPE_THETA ** (-(jnp.arange(LANES) % half).astype(F32) / half)
    rm_shape = jax.ShapeDtypeStruct((b, n_tiles, N_RES, BLOCK, WIDTH_B), F32)
    slab = pltpu.VMEM((WIDTH_B // LANES, ROWS, LANES), F32)
    q, k, v = pl.pallas_call(
        _qkv_kernel,
        grid=(b, steps_per_seq),
        in_specs=[pl.BlockSpec((ROWS, D_MODEL), lambda bi, i: (bi * steps_per_seq + i, 0)),
                  _resident((1, D_MODEL)), _resident((1, LANES)),
                  _resident((D_MODEL, WIDTH_B)), _resident((D_MODEL, WIDTH_B)),
                  _resident((D_MODEL, WIDTH_B))],
        out_specs=[pl.BlockSpec((None, None, N_RES, M_PER_STEP, WIDTH_B),
                                lambda bi, i: (bi, i // steps_per_tile, 0, i % steps_per_tile, 0))] * 3,
        out_shape=[rm_shape] * 3,
        scratch_shapes=[pltpu.VMEM((ROWS, LANES), F32), pltpu.VMEM((ROWS, LANES), F32), slab, slab, slab],
        compiler_params=_params(40, 2),
    )(x2, g_mix, row(inv), w_q, w_k, w_v)

    step_lanes = PAIRS_PER_STEP * LANES
    tile_shape = (b, n_tiles, TILE, WIDTH_B)
    att_spec = pl.BlockSpec((None, None, TILE, step_lanes), lambda bi, ci, ji: (bi, ji, 0, ci))
    y_b = pl.pallas_call(
        _attn_kernel,
        grid=(b, WIDTH_B // step_lanes, n_tiles),
        in_specs=[att_spec, att_spec, att_spec],
        out_specs=att_spec,
        out_shape=jax.ShapeDtypeStruct(tile_shape, F32),
        scratch_shapes=[pltpu.VMEM((2 * TILE, step_lanes), F32), pltpu.VMEM((2 * TILE, step_lanes), F32),
                        pltpu.VMEM((TILE, step_lanes), F32), pltpu.VMEM((TILE, step_lanes), F32),
                        pltpu.VMEM((TILE, step_lanes), F32),
                        pltpu.VMEM((2 * len(DILATIONS), BLOCK, 2 * BLOCK), F32),
                        pltpu.VMEM((2, PAIRS_PER_STEP, 2 * BLOCK, 2 * BLOCK), F32),
                        pltpu.VMEM((2, PAIRS_PER_STEP, 2 * BLOCK, 2 * BLOCK), BF16),
                        pltpu.VMEM((2, PAIRS_PER_STEP, BLOCK, LANES), F32),
                        pltpu.VMEM((2, PAIRS_PER_STEP, BLOCK, LANES), F32)],
        compiler_params=_params(48, 3),
    )(q.reshape(tile_shape), k.reshape(tile_shape), v.reshape(tile_shape))

    w_out0 = w_out[0].astype(BF16)
    out = pl.pallas_call(
        _post_kernel,
        grid=(n_tok // ROWS,),
        in_specs=[
            tok_spec(D_MODEL), tok_spec(PLE_DIM), tok_spec(WIDTH_A),
            pl.BlockSpec((None, None, N_RES, M_PER_STEP, WIDTH_B),
                         lambda i: (i // steps_per_seq, (i % steps_per_seq) // steps_per_tile, 0,
                                    i % steps_per_tile, 0)),
            _resident((1, WIDTH_B)),
            _resident((WIDTH_A, D_MODEL)), _resident((WIDTH_B, D_MODEL)),
            _resident((1, D_MODEL)),
            _resident((D_MODEL, D_FF)), _resident((D_MODEL, D_FF)), _resident((D_FF, D_MODEL)),
            _resident((1, D_MODEL)),
            _resident((D_MODEL, D_MODEL)), _resident((PLE_DIM, D_MODEL)),
            _resident((1, D_MODEL)),
        ],
        out_specs=tok_spec(D_MODEL),
        out_shape=jax.ShapeDtypeStruct((n_tok, D_MODEL), x.dtype),
        scratch_shapes=[slab, pltpu.VMEM((ROWS, D_FF), BF16)],
        compiler_params=_params(56, 1),
    )(x2, p.reshape(n_tok, PLE_DIM), y_a, y_b.reshape(b, n_tiles, N_RES, BLOCK, WIDTH_B),
      row(out_norm_b[0]), w_out0[:WIDTH_A], w_out0[WIDTH_A:], row(ffn_norm_g[0]),
      w_gate[0].astype(BF16), w_up[0].astype(BF16), w_down[0].astype(BF16),
      row(ple_norm_g[0]), w_ple_gate[0].astype(BF16), w_ple_proj[0].astype(BF16),
      row(final_norm_g))
    return out.reshape(b, s, D_MODEL)
```

```python
import jax
import jax.numpy as jnp
from jax import lax
from jax.experimental import pallas as pl
from jax.experimental.pallas import tpu as pltpu

D_MODEL = 1024
HEAD_DIM = 64
N_HEADS_A = 4
WIDTH_A = 256
WIDTH_B = 768
CHUNK = 128
BLOCK = 128
DILATIONS = (1, 4, 16)
N_RES = 16
TILE = N_RES * BLOCK
D_FF = 2816
PLE_DIM = 256
EPS = 1e-6
ROPE_THETA = 10000.0
LANES = 128
SUBLANES = 8

ROWS = 512
M_PER_STEP = ROWS // N_RES
FF_CHUNK = 256
PAIRS_PER_STEP = 2
PIECE = 32
LOG2_E = 1.4426950408889634

F32 = jnp.float32
BF16 = jnp.bfloat16


def _rms(x, g):
    return x * lax.rsqrt(jnp.mean(x * x, axis=-1, keepdims=True) + EPS) * g


def _dot(a, b):
    return jnp.dot(a, b, preferred_element_type=F32)


def _resident(shape):
    return pl.BlockSpec(shape, lambda *_: (0,) * len(shape), pipeline_mode=pl.Buffered(1))


def _to_residue_major(t, c, slab_ref, out_ref):
    halves = t.reshape(M_PER_STEP, 2, SUBLANES, LANES)
    for a in range(2):
        slab_ref[a, c] = halves[:, a].reshape(M_PER_STEP * SUBLANES, LANES)
    for r in range(N_RES):
        out_ref[r, :, c * LANES:(c + 1) * LANES] = (
            slab_ref[r // SUBLANES, c, pl.ds(r % SUBLANES, M_PER_STEP, stride=SUBLANES), :])


def _pre_kernel(x_ref, g_ref, inv_ref, w_ref, sw_ref, sb_ref, sg_ref, na_ref,
                ya_ref, q_ref, k_ref, v_ref, cu_ref, su_ref, qs_ref, ks_ref, vs_ref):
    i = pl.program_id(1)
    lane = lax.broadcasted_iota(jnp.int32, (1, LANES), 1)
    sign = jnp.where((lane % HEAD_DIM) < (HEAD_DIM // 2), -1.0, 1.0)

    @pl.when((pl.program_id(0) == 0) & (i == 0))
    def _():
        u = lax.broadcasted_iota(jnp.int32, (ROWS, LANES), 0).astype(F32)
        ang = u * inv_ref[...]
        cu_ref[...] = jnp.cos(ang)
        su_ref[...] = jnp.sin(ang) * sign

    ang0 = (i * ROWS).astype(F32) * inv_ref[...]
    ct = jnp.cos(ang0)
    st = jnp.sin(ang0) * sign
    cos = ct * cu_ref[...] - st * su_ref[...]
    sin = st * cu_ref[...] + ct * su_ref[...]
    first_half = (lax.broadcasted_iota(jnp.int32, (ROWS, LANES), 1) % HEAD_DIM) < (HEAD_DIM // 2)

    hn = _rms(x_ref[...], g_ref[...]).astype(BF16)

    def project(col0, slab_ref, out_ref, rope, scale):
        t = _dot(hn, w_ref[:, col0:col0 + WIDTH_B])
        for c in range(WIDTH_B // LANES):
            tc = t[:, c * LANES:(c + 1) * LANES]
            if rope:
                rot = jnp.where(first_half,
                                pltpu.roll(tc, LANES - HEAD_DIM // 2, 1),
                                pltpu.roll(tc, HEAD_DIM // 2, 1))
                tc = tc * cos + rot * sin
            if scale != 1.0:
                tc = tc * scale
            _to_residue_major(tc, c, slab_ref, out_ref)

    project(2 * WIDTH_A, qs_ref, q_ref, True, HEAD_DIM ** -0.5 * LOG2_E)
    project(2 * WIDTH_A + WIDTH_B, ks_ref, k_ref, True, 1.0)
    project(2 * WIDTH_A + 2 * WIDTH_B, vs_ref, v_ref, False, 1.0)

    uv = _dot(hn, w_ref[:, :2 * WIDTH_A])
    u = jax.nn.gelu(uv[:, :WIDTH_A])
    v = jax.nn.gelu(uv[:, WIDTH_A:])
    mu = jnp.mean(v, axis=-1, keepdims=True)
    d = v - mu
    var = jnp.mean(d * d, axis=-1, keepdims=True)
    vf = d * lax.rsqrt(var + EPS) * sg_ref[...]

    row = lax.broadcasted_iota(jnp.int32, (CHUNK, CHUNK), 0)
    col = lax.broadcasted_iota(jnp.int32, (CHUNK, CHUNK), 1)
    w4 = jnp.concatenate(
        [jnp.where(col <= row, sw_ref[h], 0.0) for h in range(N_HEADS_A)], axis=0).astype(BF16)
    lane_a = lax.broadcasted_iota(jnp.int32, (CHUNK, WIDTH_A), 1)
    for c in range(ROWS // CHUNK):
        rows = slice(c * CHUNK, (c + 1) * CHUNK)
        mm = _dot(w4, vf[rows].astype(BF16))
        mixed = mm[3 * CHUNK:]
        for h in (2, 1, 0):
            mixed = jnp.where(lane_a < (h + 1) * HEAD_DIM, mm[h * CHUNK:(h + 1) * CHUNK], mixed)
        ya = u[rows] * (mixed + sb_ref[...])
        ya_ref[rows, :] = _rms(ya, na_ref[...]).astype(ya_ref.dtype)


def _branch_chunks(dilation, base, idx):
    if dilation == 16:
        return [(base + idx * BLOCK, BLOCK)]
    if dilation == 4:
        r4, jj = idx % 4, idx // 4
        return [(base + (4 * c + r4) * BLOCK + 32 * jj, 32) for c in range(4)]
    return [(base + r * BLOCK + SUBLANES * idx, SUBLANES) for r in range(N_RES)]


def _prev_block(dilation, idx, cur, prv):
    if dilation == 16:
        return prv, idx
    n_per_res = N_RES // dilation
    if dilation == 4:
        r4, jj = idx % 4, idx // 4
        base = jnp.where(jj > 0, cur, prv)
        return base, r4 + 4 * ((jj + n_per_res - 1) % n_per_res)
    base = jnp.where(idx > 0, cur, prv)
    return base, (idx + n_per_res - 1) % n_per_res


def _sub_index(dilation, a):
    if dilation == 16:
        return a
    if dilation == 4:
        return 4 * (a % 32) + a // 32
    return N_RES * (a % SUBLANES) + a // SUBLANES


def _rows(start, n):
    return pl.ds(start if isinstance(start, int) else pl.multiple_of(start, SUBLANES), n)


def _load_rows(ref, chunks, lanes):
    parts = [ref[_rows(s, n), lanes] for s, n in chunks]
    return parts[0] if len(parts) == 1 else jnp.concatenate(parts, axis=0)


def _store_rows(ref, chunks, lanes, val):
    off = 0
    for s, n in chunks:
        ref[_rows(s, n), lanes] = val[off:off + n]
        off += n


def _row_pieces(chunks, piece):
    flat = []
    for s, n in chunks:
        step = min(n, piece)
        flat += [(s + o, step) for o in range(0, n, step)]
    per_group = piece // flat[0][1]
    return [flat[g:g + per_group] for g in range(0, len(flat), per_group)]


def _attn_kernel(q_ref, k_ref, v_ref, o_ref, kk_ref, vv_ref, acc_ref, m_ref, l_ref, bias_ref,
                 s_ref, p_ref, mb_ref, lb_ref):
    j = pl.program_id(2)
    ring = j % 2
    cur = pl.multiple_of(ring * TILE, TILE)
    prv = pl.multiple_of((1 - ring) * TILE, TILE)

    @pl.when(j == 0)
    def _():
        kk_ref[pl.ds(prv, TILE), :] = jnp.zeros((TILE, kk_ref.shape[1]), F32)
        vv_ref[pl.ds(prv, TILE), :] = jnp.zeros((TILE, vv_ref.shape[1]), F32)

    kk_ref[pl.ds(cur, TILE), :] = k_ref[...]
    vv_ref[pl.ds(cur, TILE), :] = v_ref[...]

    qa = lax.broadcasted_iota(jnp.int32, (BLOCK, 2 * BLOCK), 0)
    kc = lax.broadcasted_iota(jnp.int32, (BLOCK, 2 * BLOCK), 1)
    for bi, dilation in enumerate(DILATIONS):
        dist = (BLOCK + _sub_index(dilation, qa)
                - _sub_index(dilation, kc % BLOCK) - BLOCK * (kc // BLOCK))
        band = (dist >= 0) & (dist <= BLOCK)
        bias_ref[2 * bi] = jnp.where(band, 0.0, -jnp.inf)
        bias_ref[2 * bi + 1] = jnp.where(band & (kc >= BLOCK), 0.0, -jnp.inf)

    n_blk = TILE // BLOCK
    n_units = len(DILATIONS) * n_blk
    pair_lanes = [slice(p * LANES, (p + 1) * LANES) for p in range(PAIRS_PER_STEP)]
    n_pieces = BLOCK // PIECE

    def key_chunks(bi, idx):
        dilation = DILATIONS[bi]
        pbase, pidx = _prev_block(dilation, idx, cur, prv)
        return _branch_chunks(dilation, pbase, pidx), _branch_chunks(dilation, cur, idx)

    def scores(bi, idx, slot):
        q_chunks = _branch_chunks(DILATIONS[bi], 0, idx)
        kp_chunks, kc_chunks = key_chunks(bi, idx)
        low = lax.broadcasted_iota(jnp.int32, (BLOCK, LANES), 1) < HEAD_DIM
        for p, lanes in enumerate(pair_lanes):
            q = _load_rows(q_ref, q_chunks, lanes)
            lhs = jnp.concatenate([jnp.where(low, q, 0.0), jnp.where(low, 0.0, q)],
                                  axis=0).astype(BF16)
            kb = jnp.concatenate([_load_rows(kk_ref, kp_chunks, lanes),
                                  _load_rows(kk_ref, kc_chunks, lanes)], axis=0).astype(BF16)
            s_ref[slot, p] = lax.dot_general(lhs, kb, (((1,), (1,)), ((), ())),
                                             preferred_element_type=F32)

    def softmax(bi, idx, slot):
        dilation = DILATIONS[bi]
        if dilation == 16:
            first = j == 0
        elif dilation == 4:
            first = (j == 0) & (idx // 4 == 0)
        else:
            first = (j == 0) & (idx == 0)
        table = 2 * bi + first.astype(jnp.int32)
        low = lax.broadcasted_iota(jnp.int32, (PIECE, LANES), 1) < HEAD_DIM
        for p in range(PAIRS_PER_STEP):
            for c in range(n_pieces):
                rows = slice(c * PIECE, (c + 1) * PIECE)
                bias = bias_ref[table, rows, :]
                ms, ls = [], []
                for h in range(2):
                    head_rows = slice(h * BLOCK + c * PIECE, h * BLOCK + (c + 1) * PIECE)
                    sh = s_ref[slot, p, head_rows, :] + bias
                    mh = jnp.max(sh, axis=-1, keepdims=True)
                    ph = jnp.exp2(sh - mh)
                    ls.append(jnp.sum(ph, axis=-1, keepdims=True))
                    ms.append(mh)
                    p_ref[slot, p, head_rows, :] = ph.astype(BF16)
                mb_ref[slot, p, rows, :] = jnp.where(low, ms[0], ms[1])
                lb_ref[slot, p, rows, :] = jnp.where(low, ls[0], ls[1])

    def combine(bi, idx, slot):
        row_groups = _row_pieces(_branch_chunks(DILATIONS[bi], 0, idx), PIECE)
        kp_chunks, kc_chunks = key_chunks(bi, idx)
        low = lax.broadcasted_iota(jnp.int32, (PIECE, LANES), 1) < HEAD_DIM
        for p, lanes in enumerate(pair_lanes):
            vb = jnp.concatenate([_load_rows(vv_ref, kp_chunks, lanes),
                                  _load_rows(vv_ref, kc_chunks, lanes)], axis=0).astype(BF16)
            pv = _dot(p_ref[slot, p], vb)
            for c, chunks in enumerate(row_groups):
                rows = slice(c * PIECE, (c + 1) * PIECE)
                out = jnp.where(low, pv[c * PIECE:(c + 1) * PIECE],
                                pv[BLOCK + c * PIECE:BLOCK + (c + 1) * PIECE])
                m_b = mb_ref[slot, p, rows, :]
                l_b = lb_ref[slot, p, rows, :]
                if bi == 0:
                    _store_rows(acc_ref, chunks, lanes, out)
                    _store_rows(m_ref, chunks, lanes, m_b)
                    _store_rows(l_ref, chunks, lanes, l_b)
                    continue
                m_o = _load_rows(m_ref, chunks, lanes)
                m_n = jnp.maximum(m_o, m_b)
                a_o = jnp.exp2(m_o - m_n)
                a_b = jnp.exp2(m_b - m_n)
                acc = a_o * _load_rows(acc_ref, chunks, lanes) + a_b * out
                l_n = a_o * _load_rows(l_ref, chunks, lanes) + a_b * l_b
                if bi == len(DILATIONS) - 1:
                    _store_rows(o_ref, chunks, lanes, acc / l_n)
                else:
                    _store_rows(acc_ref, chunks, lanes, acc)
                    _store_rows(m_ref, chunks, lanes, m_n)
                    _store_rows(l_ref, chunks, lanes, l_n)

    def step(slot, bi_c, idx_c, bi_m, idx_m, bi_s, idx_s):
        combine(bi_c, idx_c, slot)
        if bi_m is not None:
            softmax(bi_m, idx_m, 1 - slot)
        if bi_s is not None:
            scores(bi_s, idx_s, slot)

    unit = lambda g: (g // n_blk, g % n_blk) if g < n_units else (None, None)
    scores(*unit(0), 0)
    softmax(*unit(0), 0)
    scores(*unit(1), 1)
    for g in range(n_units):
        step(g % 2, *unit(g), *unit(g + 1), *unit(g + 2))


def _post_kernel(x_ref, p_ref, ya_ref, yb_ref, nb_ref, woa_ref, wob_ref, gf_ref, wg_ref, wu_ref,
                 wd_ref, gp_ref, wpg_ref, wpp_ref, gz_ref, o_ref, slab_ref, act_ref):
    for r in range(N_RES):
        for c in range(WIDTH_B // LANES):
            slab_ref[c, pl.ds(r, M_PER_STEP, stride=N_RES), :] = yb_ref[r, :, c * LANES:(c + 1) * LANES]
    yb = jnp.concatenate([slab_ref[c] for c in range(WIDTH_B // LANES)], axis=1)
    ybn = _rms(yb, nb_ref[...]).astype(BF16)
    h = x_ref[...] + (_dot(ya_ref[...], woa_ref[...]) + _dot(ybn, wob_ref[...]))

    hn = _rms(h, gf_ref[...]).astype(BF16)
    for c in range(D_FF // FF_CHUNK):
        cols = slice(c * FF_CHUNK, (c + 1) * FF_CHUNK)
        gate = _dot(hn, wg_ref[:, cols])
        act_ref[:, cols] = (jax.nn.silu(gate) * _dot(hn, wu_ref[:, cols])).astype(BF16)
    h = h + _dot(act_ref[...], wd_ref[...])

    gate = jax.nn.sigmoid(_dot(_rms(h, gp_ref[...]).astype(BF16), wpg_ref[...]))
    h = h + gate * _dot(p_ref[...].astype(BF16), wpp_ref[...])
    o_ref[...] = _rms(h, gz_ref[...])


def _params(vmem_mib, n_axes):
    return pltpu.CompilerParams(dimension_semantics=("arbitrary",) * n_axes,
                                vmem_limit_bytes=vmem_mib * 1024 * 1024)


def kernel(x, p, mix_norm_g, w_in, sgu_w, sgu_b, sgu_norm_g, out_norm_a, out_norm_b, w_out,
           ffn_norm_g, w_gate, w_up, w_down, ple_norm_g, w_ple_gate, w_ple_proj, final_norm_g):
    b, s, d = x.shape
    assert d == D_MODEL and s % TILE == 0 and w_in.shape[0] == 1
    n_tiles = s // TILE
    n_tok = b * s
    steps_per_seq = s // ROWS
    steps_per_tile = TILE // ROWS
    row = lambda a: a.reshape(1, -1)

    x2 = x.reshape(n_tok, D_MODEL)
    tok_spec = lambda width: pl.BlockSpec((ROWS, width), lambda i: (i, 0))

    half = HEAD_DIM // 2
    inv = ROPE_THETA ** (-(jnp.arange(LANES) % half).astype(F32) / half)
    sgu_bias = jnp.repeat(sgu_b[0].T, HEAD_DIM, axis=1)
    rm_shape = jax.ShapeDtypeStruct((b, n_tiles, N_RES, BLOCK, WIDTH_B), F32)
    rm_spec = pl.BlockSpec((None, None, N_RES, M_PER_STEP, WIDTH_B),
                           lambda bi, i: (bi, i // steps_per_tile, 0, i % steps_per_tile, 0))
    seq_spec = lambda width: pl.BlockSpec((ROWS, width), lambda bi, i: (bi * steps_per_seq + i, 0))
    slab = pltpu.VMEM((2, WIDTH_B // LANES, ROWS // 2, LANES), F32)
    n_in = w_in.shape[2]
    y_a, q, k, v = pl.pallas_call(
        _pre_kernel,
        grid=(b, steps_per_seq),
        in_specs=[seq_spec(D_MODEL), _resident((1, D_MODEL)), _resident((1, LANES)),
                  _resident((D_MODEL, n_in)), _resident((N_HEADS_A, CHUNK, CHUNK)),
                  _resident((CHUNK, WIDTH_A)), _resident((1, WIDTH_A)), _resident((1, WIDTH_A))],
        out_specs=[seq_spec(WIDTH_A), rm_spec, rm_spec, rm_spec],
        out_shape=[jax.ShapeDtypeStruct((n_tok, WIDTH_A), BF16), rm_shape, rm_shape, rm_shape],
        scratch_shapes=[pltpu.VMEM((ROWS, LANES), F32), pltpu.VMEM((ROWS, LANES), F32), slab, slab, slab],
        compiler_params=_params(48, 2),
    )(x2, row(mix_norm_g[0]), row(inv), w_in[0].astype(BF16), sgu_w[0], sgu_bias,
      row(sgu_norm_g[0]), row(out_norm_a[0]))

    step_lanes = PAIRS_PER_STEP * LANES
    tile_shape = (b, n_tiles, TILE, WIDTH_B)
    att_spec = pl.BlockSpec((None, None, TILE, step_lanes), lambda bi, ci, ji: (bi, ji, 0, ci))
    y_b = pl.pallas_call(
        _attn_kernel,
        grid=(b, WIDTH_B // step_lanes, n_tiles),
        in_specs=[att_spec, att_spec, att_spec],
        out_specs=att_spec,
        out_shape=jax.ShapeDtypeStruct(tile_shape, F32),
        scratch_shapes=[pltpu.VMEM((2 * TILE, step_lanes), F32), pltpu.VMEM((2 * TILE, step_lanes), F32),
                        pltpu.VMEM((TILE, step_lanes), F32), pltpu.VMEM((TILE, step_lanes), F32),
                        pltpu.VMEM((TILE, step_lanes), F32),
                        pltpu.VMEM((2 * len(DILATIONS), BLOCK, 2 * BLOCK), F32),
                        pltpu.VMEM((2, PAIRS_PER_STEP, 2 * BLOCK, 2 * BLOCK), F32),
                        pltpu.VMEM((2, PAIRS_PER_STEP, 2 * BLOCK, 2 * BLOCK), BF16),
                        pltpu.VMEM((2, PAIRS_PER_STEP, BLOCK, LANES), F32),
                        pltpu.VMEM((2, PAIRS_PER_STEP, BLOCK, LANES), F32)],
        compiler_params=_params(48, 3),
    )(q.reshape(tile_shape), k.reshape(tile_shape), v.reshape(tile_shape))

    w_out0 = w_out[0].astype(BF16)
    out = pl.pallas_call(
        _post_kernel,
        grid=(n_tok // ROWS,),
        in_specs=[
            tok_spec(D_MODEL), tok_spec(PLE_DIM), tok_spec(WIDTH_A),
            pl.BlockSpec((None, None, N_RES, M_PER_STEP, WIDTH_B),
                         lambda i: (i // steps_per_seq, (i % steps_per_seq) // steps_per_tile, 0,
                                    i % steps_per_tile, 0)),
            _resident((1, WIDTH_B)),
            _resident((WIDTH_A, D_MODEL)), _resident((WIDTH_B, D_MODEL)),
            _resident((1, D_MODEL)),
            _resident((D_MODEL, D_FF)), _resident((D_MODEL, D_FF)), _resident((D_FF, D_MODEL)),
            _resident((1, D_MODEL)),
            _resident((D_MODEL, D_MODEL)), _resident((PLE_DIM, D_MODEL)),
            _resident((1, D_MODEL)),
        ],
        out_specs=tok_spec(D_MODEL),
        out_shape=jax.ShapeDtypeStruct((n_tok, D_MODEL), x.dtype),
        scratch_shapes=[pltpu.VMEM((WIDTH_B // LANES, ROWS, LANES), F32), pltpu.VMEM((ROWS, D_FF), BF16)],
        compiler_params=_params(56, 1),
    )(x2, p.reshape(n_tok, PLE_DIM), y_a, y_b.reshape(b, n_tiles, N_RES, BLOCK, WIDTH_B),
      row(out_norm_b[0]), w_out0[:WIDTH_A], w_out0[WIDTH_A:], row(ffn_norm_g[0]),
      w_gate[0].astype(BF16), w_up[0].astype(BF16), w_down[0].astype(BF16),
      row(ple_norm_g[0]), w_ple_gate[0].astype(BF16), w_ple_proj[0].astype(BF16),
      row(final_norm_g))
    return out.reshape(b, s, D_MODEL)
```

```python
import jax
import jax.numpy as jnp
from jax import lax
from jax.experimental import pallas as pl
from jax.experimental.pallas import tpu as pltpu

D_MODEL = 1024
HEAD_DIM = 64
N_HEADS_A = 4
WIDTH_A = 256
WIDTH_B = 768
CHUNK = 128
BLOCK = 128
DILATIONS = (1, 4, 16)
N_RES = 16
TILE = N_RES * BLOCK
D_FF = 2816
PLE_DIM = 256
EPS = 1e-6
ROPE_THETA = 10000.0
LANES = 128
SUBLANES = 8

ROWS = 512
M_PER_STEP = ROWS // N_RES
FF_CHUNK = 256
PAIRS_PER_STEP = 2
PIECE = 32
LOG2_E = 1.4426950408889634

F32 = jnp.float32
BF16 = jnp.bfloat16


def _rms(x, g):
    return x * lax.rsqrt(jnp.mean(x * x, axis=-1, keepdims=True) + EPS) * g


def _dot(a, b):
    return jnp.dot(a, b, preferred_element_type=F32)


def _resident(shape):
    return pl.BlockSpec(shape, lambda *_: (0,) * len(shape), pipeline_mode=pl.Buffered(1))


def _to_residue_major(t, c, slab_ref, out_ref):
    halves = t.reshape(M_PER_STEP, 2, SUBLANES, LANES)
    for a in range(2):
        slab_ref[a, c] = halves[:, a].reshape(M_PER_STEP * SUBLANES, LANES)
    for r in range(N_RES):
        out_ref[r, :, c * LANES:(c + 1) * LANES] = (
            slab_ref[r // SUBLANES, c, pl.ds(r % SUBLANES, M_PER_STEP, stride=SUBLANES), :])


def _pre_kernel(x_ref, g_ref, inv_ref, w_ref, sw_ref, sb_ref, sg_ref, na_ref,
                ya_ref, q_ref, k_ref, v_ref, cu_ref, su_ref, qs_ref, ks_ref, vs_ref):
    i = pl.program_id(1)
    lane = lax.broadcasted_iota(jnp.int32, (1, LANES), 1)
    sign = jnp.where((lane % HEAD_DIM) < (HEAD_DIM // 2), -1.0, 1.0)

    @pl.when((pl.program_id(0) == 0) & (i == 0))
    def _():
        u = lax.broadcasted_iota(jnp.int32, (ROWS, LANES), 0).astype(F32)
        ang = u * inv_ref[...]
        cu_ref[...] = jnp.cos(ang)
        su_ref[...] = jnp.sin(ang) * sign

    ang0 = (i * ROWS).astype(F32) * inv_ref[...]
    ct = jnp.cos(ang0)
    st = jnp.sin(ang0) * sign
    cos = ct * cu_ref[...] - st * su_ref[...]
    sin = st * cu_ref[...] + ct * su_ref[...]
    first_half = (lax.broadcasted_iota(jnp.int32, (ROWS, LANES), 1) % HEAD_DIM) < (HEAD_DIM // 2)

    hn = _rms(x_ref[...], g_ref[...]).astype(BF16)

    def project(col0, slab_ref, out_ref, rope, scale):
        t = _dot(hn, w_ref[:, col0:col0 + WIDTH_B])
        for c in range(WIDTH_B // LANES):
            tc = t[:, c * LANES:(c + 1) * LANES]
            if rope:
                rot = jnp.where(first_half,
                                pltpu.roll(tc, LANES - HEAD_DIM // 2, 1),
                                pltpu.roll(tc, HEAD_DIM // 2, 1))
                tc = tc * cos + rot * sin
            if scale != 1.0:
                tc = tc * scale
            _to_residue_major(tc, c, slab_ref, out_ref)

    uv = _dot(hn, w_ref[:, :2 * WIDTH_A])
    u = jax.nn.gelu(uv[:, :WIDTH_A])
    v = jax.nn.gelu(uv[:, WIDTH_A:])
    mu = jnp.mean(v, axis=-1, keepdims=True)
    d = v - mu
    var = jnp.mean(d * d, axis=-1, keepdims=True)
    vf = d * lax.rsqrt(var + EPS) * sg_ref[...]

    project(2 * WIDTH_A, qs_ref, q_ref, True, HEAD_DIM ** -0.5 * LOG2_E)
    project(2 * WIDTH_A + WIDTH_B, ks_ref, k_ref, True, 1.0)

    row = lax.broadcasted_iota(jnp.int32, (CHUNK, CHUNK), 0)
    col = lax.broadcasted_iota(jnp.int32, (CHUNK, CHUNK), 1)
    w4 = jnp.concatenate(
        [jnp.where(col <= row, sw_ref[h], 0.0) for h in range(N_HEADS_A)], axis=0).astype(BF16)
    lane_a = lax.broadcasted_iota(jnp.int32, (CHUNK, WIDTH_A), 1)
    for c in range(ROWS // CHUNK):
        rows = slice(c * CHUNK, (c + 1) * CHUNK)
        mm = _dot(w4, vf[rows].astype(BF16))
        mixed = mm[3 * CHUNK:]
        for h in (2, 1, 0):
            mixed = jnp.where(lane_a < (h + 1) * HEAD_DIM, mm[h * CHUNK:(h + 1) * CHUNK], mixed)
        ya = u[rows] * (mixed + sb_ref[...])
        ya_ref[rows, :] = _rms(ya, na_ref[...]).astype(ya_ref.dtype)

    project(2 * WIDTH_A + 2 * WIDTH_B, vs_ref, v_ref, False, 1.0)


def _branch_chunks(dilation, base, idx):
    if dilation == 16:
        return [(base + idx * BLOCK, BLOCK)]
    if dilation == 4:
        r4, jj = idx % 4, idx // 4
        return [(base + (4 * c + r4) * BLOCK + 32 * jj, 32) for c in range(4)]
    return [(base + r * BLOCK + SUBLANES * idx, SUBLANES) for r in range(N_RES)]


def _prev_block(dilation, idx, cur, prv):
    if dilation == 16:
        return prv, idx
    n_per_res = N_RES // dilation
    if dilation == 4:
        r4, jj = idx % 4, idx // 4
        base = jnp.where(jj > 0, cur, prv)
        return base, r4 + 4 * ((jj + n_per_res - 1) % n_per_res)
    base = jnp.where(idx > 0, cur, prv)
    return base, (idx + n_per_res - 1) % n_per_res


def _sub_index(dilation, a):
    if dilation == 16:
        return a
    if dilation == 4:
        return 4 * (a % 32) + a // 32
    return N_RES * (a % SUBLANES) + a // SUBLANES


def _rows(start, n):
    return pl.ds(start if isinstance(start, int) else pl.multiple_of(start, SUBLANES), n)


def _load_rows(ref, chunks, lanes):
    parts = [ref[_rows(s, n), lanes] for s, n in chunks]
    return parts[0] if len(parts) == 1 else jnp.concatenate(parts, axis=0)


def _store_rows(ref, chunks, lanes, val):
    off = 0
    for s, n in chunks:
        ref[_rows(s, n), lanes] = val[off:off + n]
        off += n


def _row_pieces(chunks, piece):
    flat = []
    for s, n in chunks:
        step = min(n, piece)
        flat += [(s + o, step) for o in range(0, n, step)]
    per_group = piece // flat[0][1]
    return [flat[g:g + per_group] for g in range(0, len(flat), per_group)]


def _attn_kernel(q_ref, k_ref, v_ref, o_ref, kk_ref, vv_ref, acc_ref, m_ref, l_ref, bias_ref,
                 s_ref, p_ref, mb_ref):
    j = pl.program_id(2)
    ring = j % 2
    cur = pl.multiple_of(ring * TILE, TILE)
    prv = pl.multiple_of((1 - ring) * TILE, TILE)

    @pl.when(j == 0)
    def _():
        kk_ref[pl.ds(prv, TILE), :] = jnp.zeros((TILE, kk_ref.shape[1]), F32)
        vv_ref[pl.ds(prv, TILE), :] = jnp.zeros((TILE, vv_ref.shape[1]), F32)

    kk_ref[pl.ds(cur, TILE), :] = k_ref[...]
    vv_ref[pl.ds(cur, TILE), :] = v_ref[...]

    @pl.when((pl.program_id(0) == 0) & (pl.program_id(1) == 0) & (j == 0))
    def _():
        qa = lax.broadcasted_iota(jnp.int32, (BLOCK, 2 * BLOCK), 0)
        kc = lax.broadcasted_iota(jnp.int32, (BLOCK, 2 * BLOCK), 1)
        for bi, dilation in enumerate(DILATIONS):
            dist = (BLOCK + _sub_index(dilation, qa)
                    - _sub_index(dilation, kc % BLOCK) - BLOCK * (kc // BLOCK))
            band = (dist >= 0) & (dist <= BLOCK)
            bias_ref[2 * bi] = jnp.where(band, 0.0, -jnp.inf)
            bias_ref[2 * bi + 1] = jnp.where(band & (kc >= BLOCK), 0.0, -jnp.inf)

    n_blk = TILE // BLOCK
    n_units = len(DILATIONS) * n_blk
    pair_lanes = [slice(p * LANES, (p + 1) * LANES) for p in range(PAIRS_PER_STEP)]
    n_pieces = BLOCK // PIECE

    def key_chunks(bi, idx):
        dilation = DILATIONS[bi]
        pbase, pidx = _prev_block(dilation, idx, cur, prv)
        return _branch_chunks(dilation, pbase, pidx), _branch_chunks(dilation, cur, idx)

    def scores(bi, idx, slot):
        q_chunks = _branch_chunks(DILATIONS[bi], 0, idx)
        kp_chunks, kc_chunks = key_chunks(bi, idx)
        low = lax.broadcasted_iota(jnp.int32, (BLOCK, LANES), 1) < HEAD_DIM
        for p, lanes in enumerate(pair_lanes):
            q = _load_rows(q_ref, q_chunks, lanes)
            lhs = jnp.concatenate([jnp.where(low, q, 0.0), jnp.where(low, 0.0, q)],
                                  axis=0).astype(BF16)
            kb = jnp.concatenate([_load_rows(kk_ref, kp_chunks, lanes),
                                  _load_rows(kk_ref, kc_chunks, lanes)], axis=0).astype(BF16)
            s_ref[slot, p] = lax.dot_general(lhs, kb, (((1,), (1,)), ((), ())),
                                             preferred_element_type=F32)

    def softmax(bi, idx, slot):
        dilation = DILATIONS[bi]
        if dilation == 16:
            first = j == 0
        elif dilation == 4:
            first = (j == 0) & (idx // 4 == 0)
        else:
            first = (j == 0) & (idx == 0)
        table = 2 * bi + first.astype(jnp.int32)
        low = lax.broadcasted_iota(jnp.int32, (PIECE, LANES), 1) < HEAD_DIM
        for p in range(PAIRS_PER_STEP):
            for c in range(n_pieces):
                rows = slice(c * PIECE, (c + 1) * PIECE)
                bias = bias_ref[table, rows, :]
                ms = []
                for h in range(2):
                    head_rows = slice(h * BLOCK + c * PIECE, h * BLOCK + (c + 1) * PIECE)
                    sh = s_ref[slot, p, head_rows, :] + bias
                    mh = jnp.max(sh, axis=-1, keepdims=True)
                    ms.append(mh)
                    p_ref[slot, p, head_rows, :] = jnp.exp2(sh - mh).astype(BF16)
                mb_ref[slot, p, rows, :] = jnp.where(low, ms[0], ms[1])

    def combine(bi, idx, slot):
        row_groups = _row_pieces(_branch_chunks(DILATIONS[bi], 0, idx), PIECE)
        kp_chunks, kc_chunks = key_chunks(bi, idx)
        low = lax.broadcasted_iota(jnp.int32, (PIECE, 2 * LANES), 1) % LANES < HEAD_DIM
        for p, lanes in enumerate(pair_lanes):
            vb =jnp.concatenate([_load_rows(vv_ref, kp_chunks, lanes),
                                  _load_rows(vv_ref, kc_chunks, lanes)], axis=0).astype(BF16)
            pv = _dot(p_ref[slot, p], jnp.concatenate([vb, jnp.ones_like(vb)], axis=1))
            for c, chunks in enumerate(row_groups):
                rows = slice(c * PIECE, (c + 1) * PIECE)
                both = jnp.where(low, pv[c * PIECE:(c + 1) * PIECE],
                                 pv[BLOCK + c * PIECE:BLOCK + (c + 1) * PIECE])
                out, l_b = both[:, :LANES], both[:, LANES:]
                m_b = mb_ref[slot, p, rows, :]
                if bi == 0:
                    _store_rows(acc_ref, chunks, lanes, out)
                    _store_rows(m_ref, chunks, lanes, m_b)
                    _store_rows(l_ref, chunks, lanes, l_b)
                    continue
                m_o = _load_rows(m_ref, chunks, lanes)
                m_n = jnp.maximum(m_o, m_b)
                a_o = jnp.exp2(m_o - m_n)
                a_b = jnp.exp2(m_b - m_n)
                acc = a_o * _load_rows(acc_ref, chunks, lanes) + a_b * out
                l_n = a_o * _load_rows(l_ref, chunks, lanes) + a_b * l_b
                if bi == len(DILATIONS) - 1:
                    _store_rows(o_ref, chunks, lanes, acc / l_n)
                else:
                    _store_rows(acc_ref, chunks, lanes, acc)
                    _store_rows(m_ref, chunks, lanes, m_n)
                    _store_rows(l_ref, chunks, lanes, l_n)

    def step(slot, bi_c, idx_c, bi_m, idx_m, bi_s, idx_s):
        combine(bi_c, idx_c, slot)
        if bi_m is not None:
            softmax(bi_m, idx_m, 1 - slot)
        if bi_s is not None:
            scores(bi_s, idx_s, slot)

    unit = lambda g: (g // n_blk, g % n_blk) if g < n_units else (None, None)
    scores(*unit(0), 0)
    softmax(*unit(0), 0)
    scores(*unit(1), 1)
    for g in range(n_units):
        step(g % 2, *unit(g), *unit(g + 1), *unit(g + 2))


def _post_kernel(x_ref, p_ref, ya_ref, yb_ref, nb_ref, woa_ref, wob_ref, gf_ref, wg_ref, wu_ref,
                 wd_ref, gp_ref, wpg_ref, wpp_ref, gz_ref, o_ref, slab_ref, act_ref):
    for r in range(N_RES):
        for c in range(WIDTH_B // LANES):
            slab_ref[c, pl.ds(r, M_PER_STEP, stride=N_RES), :] = yb_ref[r, :, c * LANES:(c + 1) * LANES]
    yb = jnp.concatenate([slab_ref[c] for c in range(WIDTH_B // LANES)], axis=1)
    ybn = _rms(yb, nb_ref[...]).astype(BF16)
    h = x_ref[...] + (_dot(ya_ref[...], woa_ref[...]) + _dot(ybn, wob_ref[...]))

    hn = _rms(h, gf_ref[...]).astype(BF16)
    for c in range(D_FF // FF_CHUNK):
        cols = slice(c * FF_CHUNK, (c + 1) * FF_CHUNK)
        gate = _dot(hn, wg_ref[:, cols])
        act_ref[:, cols] = (jax.nn.silu(gate) * _dot(hn, wu_ref[:, cols])).astype(BF16)
    h = h + _dot(act_ref[...], wd_ref[...])

    gate = jax.nn.sigmoid(_dot(_rms(h, gp_ref[...]).astype(BF16), wpg_ref[...]))
    h = h + gate * _dot(p_ref[...].astype(BF16), wpp_ref[...])
    o_ref[...] = _rms(h, gz_ref[...])


def _params(vmem_mib, n_axes):
    return pltpu.CompilerParams(dimension_semantics=("arbitrary",) * n_axes,
                                vmem_limit_bytes=vmem_mib * 1024 * 1024)


def kernel(x, p, mix_norm_g, w_in, sgu_w, sgu_b, sgu_norm_g, out_norm_a, out_norm_b, w_out,
           ffn_norm_g, w_gate, w_up, w_down, ple_norm_g, w_ple_gate, w_ple_proj, final_norm_g):
    b, s, d = x.shape
    assert d == D_MODEL and s % TILE == 0 and w_in.shape[0] == 1
    n_tiles = s // TILE
    n_tok = b * s
    steps_per_seq = s // ROWS
    steps_per_tile = TILE // ROWS
    row = lambda a: a.reshape(1, -1)

    x2 = x.reshape(n_tok, D_MODEL)
    tok_spec = lambda width: pl.BlockSpec((ROWS, width), lambda i: (i, 0))

    half = HEAD_DIM // 2
    inv = ROPE_THETA ** (-(jnp.arange(LANES) % half).astype(F32) / half)
    sgu_bias = jnp.repeat(sgu_b[0].T, HEAD_DIM, axis=1)
    rm_shape = jax.ShapeDtypeStruct((b, n_tiles, N_RES, BLOCK, WIDTH_B), F32)
    rm_spec = pl.BlockSpec((None, None, N_RES, M_PER_STEP, WIDTH_B),
                           lambda bi, i: (bi, i // steps_per_tile, 0, i % steps_per_tile, 0))
    seq_spec = lambda width: pl.BlockSpec((ROWS, width), lambda bi, i: (bi * steps_per_seq + i, 0))
    slab = pltpu.VMEM((2, WIDTH_B // LANES, ROWS // 2, LANES), F32)
    n_in = w_in.shape[2]
    y_a, q, k, v = pl.pallas_call(
        _pre_kernel,
        grid=(b, steps_per_seq),
        in_specs=[seq_spec(D_MODEL), _resident((1, D_MODEL)), _resident((1, LANES)),
                  _resident((D_MODEL, n_in)), _resident((N_HEADS_A, CHUNK, CHUNK)),
                  _resident((CHUNK, WIDTH_A)), _resident((1, WIDTH_A)), _resident((1, WIDTH_A))],
        out_specs=[seq_spec(WIDTH_A), rm_spec, rm_spec, rm_spec],
        out_shape=[jax.ShapeDtypeStruct((n_tok, WIDTH_A), BF16), rm_shape, rm_shape, rm_shape],
        scratch_shapes=[pltpu.VMEM((ROWS, LANES), F32), pltpu.VMEM((ROWS, LANES), F32), slab, slab, slab],
        compiler_params=_params(48, 2),
    )(x2, row(mix_norm_g[0]), row(inv), w_in[0].astype(BF16), sgu_w[0], sgu_bias,
      row(sgu_norm_g[0]), row(out_norm_a[0]))

    step_lanes = PAIRS_PER_STEP * LANES
    tile_shape = (b, n_tiles, TILE, WIDTH_B)
    att_spec = pl.BlockSpec((None, None, TILE, step_lanes), lambda bi, ci, ji: (bi, ji, 0, ci))
    y_b = pl.pallas_call(
        _attn_kernel,
        grid=(b, WIDTH_B // step_lanes, n_tiles),
        in_specs=[att_spec, att_spec, att_spec],
        out_specs=att_spec,
        out_shape=jax.ShapeDtypeStruct(tile_shape, F32),
        scratch_shapes=[pltpu.VMEM((2 * TILE, step_lanes), F32), pltpu.VMEM((2 * TILE, step_lanes), F32),
                        pltpu.VMEM((TILE, step_lanes), F32), pltpu.VMEM((TILE, step_lanes), F32),
                        pltpu.VMEM((TILE, step_lanes), F32),
                        pltpu.VMEM((2 * len(DILATIONS), BLOCK, 2 * BLOCK), F32),
                        pltpu.VMEM((2, PAIRS_PER_STEP, 2 * BLOCK, 2 * BLOCK), F32),
                        pltpu.VMEM((2, PAIRS_PER_STEP, 2 * BLOCK, 2 * BLOCK), BF16),
                        pltpu.VMEM((2, PAIRS_PER_STEP, BLOCK, LANES), F32)],
        compiler_params=_params(48, 3),
    )(q.reshape(tile_shape), k.reshape(tile_shape), v.reshape(tile_shape))

    w_out0 = w_out[0].astype(BF16)
    out = pl.pallas_call(
        _post_kernel,
        grid=(n_tok // ROWS,),
        in_specs=[
            tok_spec(D_MODEL), tok_spec(PLE_DIM), tok_spec(WIDTH_A),
            pl.BlockSpec((None, None, N_RES, M_PER_STEP, WIDTH_B),
                         lambda i: (i // steps_per_seq, (i % steps_per_seq) // steps_per_tile, 0,
                                    i % steps_per_tile, 0)),
            _resident((1, WIDTH_B)),
            _resident((WIDTH_A, D_MODEL)), _resident((WIDTH_B, D_MODEL)),
            _resident((1, D_MODEL)),
            _resident((D_MODEL, D_FF)), _resident((D_MODEL, D_FF)), _resident((D_FF, D_MODEL)),
            _resident((1, D_MODEL)),
            _resident((D_MODEL, D_MODEL)), _resident((PLE_DIM, D_MODEL)),
            _resident((1, D_MODEL)),
        ],
        out_specs=tok_spec(D_MODEL),
        out_shape=jax.ShapeDtypeStruct((n_tok, D_MODEL), x.dtype),
        scratch_shapes=[pltpu.VMEM((WIDTH_B // LANES, ROWS, LANES), F32), pltpu.VMEM((ROWS, D_FF), BF16)],
        compiler_params=_params(56, 1),
    )(x2, p.reshape(n_tok, PLE_DIM), y_a, y_b.reshape(b, n_tiles, N_RES, BLOCK, WIDTH_B),
      row(out_norm_b[0]), w_out0[:WIDTH_A], w_out0[WIDTH_A:], row(ffn_norm_g[0]),
      w_gate[0].astype(BF16), w_up[0].astype(BF16), w_down[0].astype(BF16),
      row(ple_norm_g[0]), w_ple_gate[0].astype(BF16), w_ple_proj[0].astype(BF16),
      row(final_norm_g))
    return out.reshape(b, s, D_MODEL)
```

```python
import jax
import jax.numpy as jnp
from jax import lax
from jax.experimental import pallas as pl
from jax.experimental.pallas import tpu as pltpu

D_MODEL = 1024
HEAD_DIM = 64
N_HEADS_A = 4
WIDTH_A = 256
WIDTH_B = 768
CHUNK = 128
BLOCK = 128
DILATIONS = (1, 4, 16)
N_RES = 16
TILE = N_RES * BLOCK
D_FF = 2816
PLE_DIM = 256
EPS = 1e-6
ROPE_THETA = 10000.0
LANES = 128
SUBLANES = 8

ROWS = 512
M_PER_STEP = ROWS // N_RES
FF_CHUNK = 256
ROW_GROUPS = 2
PAIRS_PER_STEP = 2
PIECE = 32
LOG2_E = 1.4426950408889634

F32 = jnp.float32
BF16 = jnp.bfloat16


def _rms(x, g):
    return x * lax.rsqrt(jnp.mean(x * x, axis=-1, keepdims=True) + EPS) * g


def _dot(a, b):
    return jnp.dot(a, b, preferred_element_type=F32)


def _resident(shape):
    return pl.BlockSpec(shape, lambda *_: (0,) * len(shape), pipeline_mode=pl.Buffered(1))


def _to_residue_major(t, c, slab_ref, out_ref):
    halves = t.reshape(M_PER_STEP, 2, SUBLANES, LANES)
    for a in range(2):
        slab_ref[a, c] = halves[:, a].reshape(M_PER_STEP * SUBLANES, LANES)
    for r in range(N_RES):
        out_ref[r, :, c * LANES:(c + 1) * LANES] = (
            slab_ref[r // SUBLANES, c, pl.ds(r % SUBLANES, M_PER_STEP, stride=SUBLANES), :])


def _pre_kernel(x_ref, g_ref, inv_ref, w_ref, sw_ref, sb_ref, sg_ref, na_ref,
                ya_ref, q_ref, k_ref, v_ref, cu_ref, su_ref, qs_ref, ks_ref, vs_ref):
    i = pl.program_id(1)
    lane = lax.broadcasted_iota(jnp.int32, (1, LANES), 1)
    sign = jnp.where((lane % HEAD_DIM) < (HEAD_DIM // 2), -1.0, 1.0)

    @pl.when((pl.program_id(0) == 0) & (i == 0))
    def _():
        u = lax.broadcasted_iota(jnp.int32, (ROWS, LANES), 0).astype(F32)
        ang = u * inv_ref[...]
        cu_ref[...] = jnp.cos(ang)
        su_ref[...] = jnp.sin(ang) * sign

    ang0 = (i * ROWS).astype(F32) * inv_ref[...]
    ct = jnp.cos(ang0)
    st = jnp.sin(ang0) * sign
    cos = ct * cu_ref[...] - st * su_ref[...]
    sin = st * cu_ref[...] + ct * su_ref[...]
    first_half = (lax.broadcasted_iota(jnp.int32, (ROWS, LANES), 1) % HEAD_DIM) < (HEAD_DIM // 2)

    hn = _rms(x_ref[...], g_ref[...]).astype(BF16)

    def project(col0, slab_ref, out_ref, rope, scale):
        t = _dot(hn, w_ref[:, col0:col0 + WIDTH_B])
        for c in range(WIDTH_B // LANES):
            tc = t[:, c * LANES:(c + 1) * LANES]
            if rope:
                rot = jnp.where(first_half,
                                pltpu.roll(tc, LANES - HEAD_DIM // 2, 1),
                                pltpu.roll(tc, HEAD_DIM // 2, 1))
                tc = tc * cos + rot * sin
            if scale != 1.0:
                tc = tc * scale
            _to_residue_major(tc, c, slab_ref, out_ref)

    uv = _dot(hn, w_ref[:, :2 * WIDTH_A])
    u = jax.nn.gelu(uv[:, :WIDTH_A])
    v = jax.nn.gelu(uv[:, WIDTH_A:])
    mu = jnp.mean(v, axis=-1, keepdims=True)
    d = v - mu
    var = jnp.mean(d * d, axis=-1, keepdims=True)
    vf = d * lax.rsqrt(var + EPS) * sg_ref[...]

    project(2 * WIDTH_A, qs_ref, q_ref, True, HEAD_DIM ** -0.5 * LOG2_E)
    project(2 * WIDTH_A + WIDTH_B, ks_ref, k_ref, True, 1.0)

    row = lax.broadcasted_iota(jnp.int32, (CHUNK, CHUNK), 0)
    col = lax.broadcasted_iota(jnp.int32, (CHUNK, CHUNK), 1)
    w4 = jnp.concatenate(
        [jnp.where(col <= row, sw_ref[h], 0.0) for h in range(N_HEADS_A)], axis=0).astype(BF16)
    lane_a = lax.broadcasted_iota(jnp.int32, (CHUNK, WIDTH_A), 1)
    for c in range(ROWS // CHUNK):
        rows = slice(c * CHUNK, (c + 1) * CHUNK)
        mm = _dot(w4, vf[rows].astype(BF16))
        mixed = mm[3 * CHUNK:]
        for h in (2, 1, 0):
            mixed = jnp.where(lane_a < (h + 1) * HEAD_DIM, mm[h * CHUNK:(h + 1) * CHUNK], mixed)
        ya = u[rows] * (mixed + sb_ref[...])
        ya_ref[rows, :] = _rms(ya, na_ref[...]).astype(ya_ref.dtype)

    project(2 * WIDTH_A + 2 * WIDTH_B, vs_ref, v_ref, False, 1.0)


def _branch_chunks(dilation, base, idx):
    if dilation == 16:
        return [(base + idx * BLOCK, BLOCK)]
    if dilation == 4:
        r4, jj = idx % 4, idx // 4
        return [(base + (4 * c + r4) * BLOCK + 32 * jj, 32) for c in range(4)]
    return [(base + r * BLOCK + SUBLANES * idx, SUBLANES) for r in range(N_RES)]


def _prev_block(dilation, idx, cur, prv):
    if dilation == 16:
        return prv, idx
    n_per_res = N_RES // dilation
    if dilation == 4:
        r4, jj = idx % 4, idx // 4
        base = jnp.where(jj > 0, cur, prv)
        return base, r4 + 4 * ((jj + n_per_res - 1) % n_per_res)
    base = jnp.where(idx > 0, cur, prv)
    return base, (idx + n_per_res - 1) % n_per_res


def _sub_index(dilation, a):
    if dilation == 16:
        return a
    if dilation == 4:
        return 4 * (a % 32) + a // 32
    return N_RES * (a % SUBLANES) + a // SUBLANES


def _rows(start, n):
    return pl.ds(start if isinstance(start, int) else pl.multiple_of(start, SUBLANES), n)


def _load_rows(ref, chunks, lanes):
    parts = [ref[_rows(s, n), lanes] for s, n in chunks]
    return parts[0] if len(parts) == 1 else jnp.concatenate(parts, axis=0)


def _store_rows(ref, chunks, lanes, val):
    off = 0
    for s, n in chunks:
        ref[_rows(s, n), lanes] = val[off:off + n]
        off += n


def _row_pieces(chunks, piece):
    flat = []
    for s, n in chunks:
        step = min(n, piece)
        flat += [(s + o, step) for o in range(0, n, step)]
    per_group = piece // flat[0][1]
    return [flat[g:g + per_group] for g in range(0, len(flat), per_group)]


def _attn_kernel(q_ref, k_ref, v_ref, o_ref, kk_ref, vv_ref, acc_ref, m_ref, l_ref, bias_ref,
                 s_ref, p_ref, mb_ref):
    j = pl.program_id(2)
    ring = j % 2
    cur = pl.multiple_of(ring * TILE, TILE)
    prv = pl.multiple_of((1 - ring) * TILE, TILE)

    @pl.when(j == 0)
    def _():
        kk_ref[pl.ds(prv, TILE), :] = jnp.zeros((TILE, kk_ref.shape[1]), F32)
        vv_ref[pl.ds(prv, TILE), :] = jnp.zeros((TILE, vv_ref.shape[1]), F32)

    kk_ref[pl.ds(cur, TILE), :] = k_ref[...]
    vv_ref[pl.ds(cur, TILE), :] = v_ref[...]

    @pl.when((pl.program_id(0) == 0) & (pl.program_id(1) == 0) & (j == 0))
    def _():
        qa = lax.broadcasted_iota(jnp.int32, (BLOCK, 2 * BLOCK), 0)
        kc = lax.broadcasted_iota(jnp.int32, (BLOCK, 2 * BLOCK), 1)
        for bi, dilation in enumerate(DILATIONS):
            dist = (BLOCK + _sub_index(dilation, qa)
                    - _sub_index(dilation, kc % BLOCK) - BLOCK * (kc // BLOCK))
            band = (dist >= 0) & (dist <= BLOCK)
            bias_ref[2 * bi] = jnp.where(band, 0.0, -jnp.inf)
            bias_ref[2 * bi + 1] = jnp.where(band & (kc >= BLOCK), 0.0, -jnp.inf)

    n_blk = TILE // BLOCK
    n_units = len(DILATIONS) * n_blk
    pair_lanes = [slice(p * LANES, (p + 1) * LANES) for p in range(PAIRS_PER_STEP)]
    n_pieces = BLOCK // PIECE

    def key_chunks(bi, idx):
        dilation = DILATIONS[bi]
        pbase, pidx = _prev_block(dilation, idx, cur, prv)
        return _branch_chunks(dilation, pbase, pidx), _branch_chunks(dilation, cur, idx)

    def scores(bi, idx, slot):
        q_chunks = _branch_chunks(DILATIONS[bi], 0, idx)
        kp_chunks, kc_chunks = key_chunks(bi, idx)
        low = lax.broadcasted_iota(jnp.int32, (BLOCK, LANES), 1) < HEAD_DIM
        for p, lanes in enumerate(pair_lanes):
            q = _load_rows(q_ref, q_chunks, lanes)
            lhs = jnp.concatenate([jnp.where(low, q, 0.0), jnp.where(low, 0.0, q)],
                                  axis=0).astype(BF16)
            kb = jnp.concatenate([_load_rows(kk_ref, kp_chunks, lanes),
                                  _load_rows(kk_ref, kc_chunks, lanes)], axis=0).astype(BF16)
            s_ref[slot, p] = lax.dot_general(lhs, kb, (((1,), (1,)), ((), ())),
                                             preferred_element_type=F32)

    def softmax(bi, idx, slot):
        dilation = DILATIONS[bi]
        if dilation == 16:
            first = j == 0
        elif dilation == 4:
            first = (j == 0) & (idx // 4 == 0)
        else:
            first = (j == 0) & (idx == 0)
        table = 2 * bi + first.astype(jnp.int32)
        low = lax.broadcasted_iota(jnp.int32, (PIECE, LANES), 1) < HEAD_DIM
        for p in range(PAIRS_PER_STEP):
            for c in range(n_pieces):
                rows = slice(c * PIECE, (c + 1) * PIECE)
                bias = bias_ref[table, rows, :]
                ms = []
                for h in range(2):
                    head_rows = slice(h * BLOCK + c * PIECE, h * BLOCK + (c + 1) * PIECE)
                    sh = s_ref[slot, p, head_rows, :] + bias
                    mh = jnp.max(sh, axis=-1, keepdims=True)
                    ms.append(mh)
                    p_ref[slot, p, head_rows, :] = jnp.exp2(sh - mh).astype(BF16)
                mb_ref[slot, p, rows, :] = jnp.where(low, ms[0], ms[1])

    def combine(bi, idx, slot):
        row_groups = _row_pieces(_branch_chunks(DILATIONS[bi], 0, idx), PIECE)
        kp_chunks, kc_chunks = key_chunks(bi, idx)
        low = lax.broadcasted_iota(jnp.int32, (PIECE, 2 * LANES), 1) % LANES < HEAD_DIM
        for p, lanes in enumerate(pair_lanes):
            vb =jnp.concatenate([_load_rows(vv_ref, kp_chunks, lanes),
                                  _load_rows(vv_ref, kc_chunks, lanes)], axis=0).astype(BF16)
            pv = _dot(p_ref[slot, p], jnp.concatenate([vb, jnp.ones_like(vb)], axis=1))
            for c, chunks in enumerate(row_groups):
                rows = slice(c * PIECE, (c + 1) * PIECE)
                both = jnp.where(low, pv[c * PIECE:(c + 1) * PIECE],
                                 pv[BLOCK + c * PIECE:BLOCK + (c + 1) * PIECE])
                out, l_b = both[:, :LANES], both[:, LANES:]
                m_b = mb_ref[slot, p, rows, :]
                if bi == 0:
                    _store_rows(acc_ref, chunks, lanes, out)
                    _store_rows(m_ref, chunks, lanes, m_b)
                    _store_rows(l_ref, chunks, lanes, l_b)
                    continue
                m_o = _load_rows(m_ref, chunks, lanes)
                m_n = jnp.maximum(m_o, m_b)
                a_o = jnp.exp2(m_o - m_n)
                a_b = jnp.exp2(m_b - m_n)
                acc = a_o * _load_rows(acc_ref, chunks, lanes) + a_b * out
                l_n = a_o * _load_rows(l_ref, chunks, lanes) + a_b * l_b
                if bi == len(DILATIONS) - 1:
                    _store_rows(o_ref, chunks, lanes, acc / l_n)
                else:
                    _store_rows(acc_ref, chunks, lanes, acc)
                    _store_rows(m_ref, chunks, lanes, m_n)
                    _store_rows(l_ref, chunks, lanes, l_n)

    def step(slot, bi_c, idx_c, bi_m, idx_m, bi_s, idx_s):
        combine(bi_c, idx_c, slot)
        if bi_m is not None:
            softmax(bi_m, idx_m, 1 - slot)
        if bi_s is not None:
            scores(bi_s, idx_s, slot)

    unit = lambda g: (g // n_blk, g % n_blk) if g < n_units else (None, None)
    scores(*unit(0), 0)
    softmax(*unit(0), 0)
    scores(*unit(1), 1)
    for g in range(n_units):
        step(g % 2, *unit(g), *unit(g + 1), *unit(g + 2))


def _post_kernel(x_ref, p_ref, ya_ref, yb_ref, nb_ref, woa_ref, wob_ref, gf_ref, wg_ref, wu_ref,
                 wd_ref, gp_ref, wpg_ref, wpp_ref, gz_ref, o_ref, slab_ref, act_ref):
    for r in range(N_RES):
        for c in range(WIDTH_B // LANES):
            slab_ref[c, pl.ds(r, M_PER_STEP, stride=N_RES), :] = yb_ref[r, :, c * LANES:(c + 1) * LANES]
    groups = [slice(g * (ROWS // ROW_GROUPS), (g + 1) * (ROWS // ROW_GROUPS)) for g in range(ROW_GROUPS)]
    pe = [_dot(p_ref[rows, :].astype(BF16), wpp_ref[...]) for rows in groups]
    ha = [x_ref[rows, :] + _dot(ya_ref[rows, :], woa_ref[...]) for rows in groups]
    ybn = [_rms(jnp.concatenate([slab_ref[c, rows, :] for c in range(WIDTH_B // LANES)], axis=1),
                nb_ref[...]).astype(BF16) for rows in groups]
    h = [hg + _dot(y, wob_ref[...]) for hg, y in zip(ha, ybn)]

    hn = [_rms(hg, gf_ref[...]).astype(BF16) for hg in h]
    for rows, hg in zip(groups, hn):
        for c in range(D_FF // FF_CHUNK):
            cols = slice(c * FF_CHUNK, (c + 1) * FF_CHUNK)
            gate = _dot(hg, wg_ref[:, cols])
            act_ref[rows, cols] = (jax.nn.silu(gate) * _dot(hg, wu_ref[:, cols])).astype(BF16)
    h = [hg + _dot(act_ref[rows, :], wd_ref[...]) for rows, hg in zip(groups, h)]

    gate = [jax.nn.sigmoid(_dot(_rms(hg, gp_ref[...]).astype(BF16), wpg_ref[...])) for hg in h]
    for rows, hg, gg, pg in zip(groups, h, gate, pe):
        o_ref[rows, :] = _rms(hg + gg * pg, gz_ref[...])


def _params(vmem_mib, n_axes):
    return pltpu.CompilerParams(dimension_semantics=("arbitrary",) * n_axes,
                                vmem_limit_bytes=vmem_mib * 1024 * 1024)


def kernel(x, p, mix_norm_g, w_in, sgu_w, sgu_b, sgu_norm_g, out_norm_a, out_norm_b, w_out,
           ffn_norm_g, w_gate, w_up, w_down, ple_norm_g, w_ple_gate, w_ple_proj, final_norm_g):
    b, s, d = x.shape
    assert d == D_MODEL and s % TILE == 0 and w_in.shape[0] == 1
    n_tiles = s // TILE
    n_tok = b * s
    steps_per_seq = s // ROWS
    steps_per_tile = TILE // ROWS
    row = lambda a: a.reshape(1, -1)

    x2 = x.reshape(n_tok, D_MODEL)
    tok_spec = lambda width: pl.BlockSpec((ROWS, width), lambda i: (i, 0))

    half = HEAD_DIM // 2
    inv = ROPE_THETA ** (-(jnp.arange(LANES) % half).astype(F32) / half)
    sgu_bias = jnp.repeat(sgu_b[0].T, HEAD_DIM, axis=1)
    rm_shape = jax.ShapeDtypeStruct((b, n_tiles, N_RES, BLOCK, WIDTH_B), F32)
    rm_spec = pl.BlockSpec((None, None, N_RES, M_PER_STEP, WIDTH_B),
                           lambda bi, i: (bi, i // steps_per_tile, 0, i % steps_per_tile, 0))
    seq_spec = lambda width: pl.BlockSpec((ROWS, width), lambda bi, i: (bi * steps_per_seq + i, 0))
    slab = pltpu.VMEM((2, WIDTH_B // LANES, ROWS // 2, LANES), F32)
    n_in = w_in.shape[2]
    y_a, q, k, v = pl.pallas_call(
        _pre_kernel,
        grid=(b, steps_per_seq),
        in_specs=[seq_spec(D_MODEL), _resident((1, D_MODEL)), _resident((1, LANES)),
                  _resident((D_MODEL, n_in)), _resident((N_HEADS_A, CHUNK, CHUNK)),
                  _resident((CHUNK, WIDTH_A)), _resident((1, WIDTH_A)), _resident((1, WIDTH_A))],
        out_specs=[seq_spec(WIDTH_A), rm_spec, rm_spec, rm_spec],
        out_shape=[jax.ShapeDtypeStruct((n_tok, WIDTH_A), BF16), rm_shape, rm_shape, rm_shape],
        scratch_shapes=[pltpu.VMEM((ROWS, LANES), F32), pltpu.VMEM((ROWS, LANES), F32), slab, slab, slab],
        compiler_params=_params(48, 2),
    )(x2, row(mix_norm_g[0]), row(inv), w_in[0].astype(BF16), sgu_w[0], sgu_bias,
      row(sgu_norm_g[0]), row(out_norm_a[0]))

    step_lanes = PAIRS_PER_STEP * LANES
    tile_shape = (b, n_tiles, TILE, WIDTH_B)
    att_spec = pl.BlockSpec((None, None, TILE, step_lanes), lambda bi, ci, ji: (bi, ji, 0, ci))
    y_b = pl.pallas_call(
        _attn_kernel,
        grid=(b, WIDTH_B // step_lanes, n_tiles),
        in_specs=[att_spec, att_spec, att_spec],
        out_specs=att_spec,
        out_shape=jax.ShapeDtypeStruct(tile_shape, F32),
        scratch_shapes=[pltpu.VMEM((2 * TILE, step_lanes), F32), pltpu.VMEM((2 * TILE, step_lanes), F32),
                        pltpu.VMEM((TILE, step_lanes), F32), pltpu.VMEM((TILE, step_lanes), F32),
                        pltpu.VMEM((TILE, step_lanes), F32),
                        pltpu.VMEM((2 * len(DILATIONS), BLOCK, 2 * BLOCK), F32),
                        pltpu.VMEM((2, PAIRS_PER_STEP, 2 * BLOCK, 2 * BLOCK), F32),
                        pltpu.VMEM((2, PAIRS_PER_STEP, 2 * BLOCK, 2 * BLOCK), BF16),
                        pltpu.VMEM((2, PAIRS_PER_STEP, BLOCK, LANES), F32)],
        compiler_params=_params(48, 3),
    )(q.reshape(tile_shape), k.reshape(tile_shape), v.reshape(tile_shape))

    w_out0 = w_out[0].astype(BF16)
    out = pl.pallas_call(
        _post_kernel,
        grid=(n_tok // ROWS,),
        in_specs=[
            tok_spec(D_MODEL), tok_spec(PLE_DIM), tok_spec(WIDTH_A),
            pl.BlockSpec((None, None, N_RES, M_PER_STEP, WIDTH_B),
                         lambda i: (i // steps_per_seq, (i % steps_per_seq) // steps_per_tile, 0,
                                    i % steps_per_tile, 0)),
            _resident((1, WIDTH_B)),
            _resident((WIDTH_A, D_MODEL)), _resident((WIDTH_B, D_MODEL)),
            _resident((1, D_MODEL)),
            _resident((D_MODEL, D_FF)), _resident((D_MODEL, D_FF)), _resident((D_FF, D_MODEL)),
            _resident((1, D_MODEL)),
            _resident((D_MODEL, D_MODEL)), _resident((PLE_DIM, D_MODEL)),
            _resident((1, D_MODEL)),
        ],
        out_specs=tok_spec(D_MODEL),
        out_shape=jax.ShapeDtypeStruct((n_tok, D_MODEL), x.dtype),
        scratch_shapes=[pltpu.VMEM((WIDTH_B // LANES, ROWS, LANES), F32), pltpu.VMEM((ROWS, D_FF), BF16)],
        compiler_params=_params(56, 1),
    )(x2, p.reshape(n_tok, PLE_DIM), y_a, y_b.reshape(b, n_tiles, N_RES, BLOCK, WIDTH_B),
      row(out_norm_b[0]), w_out0[:WIDTH_A], w_out0[WIDTH_A:], row(ffn_norm_g[0]),
      w_gate[0].astype(BF16), w_up[0].astype(BF16), w_down[0].astype(BF16),
      row(ple_norm_g[0]), w_ple_gate[0].astype(BF16), w_ple_proj[0].astype(BF16),
      row(final_norm_g))
    return out.reshape(b, s, D_MODEL)
```

```python
import jax
import jax.numpy as jnp
from jax import lax
from jax.experimental import pallas as pl
from jax.experimental.pallas import tpu as pltpu

D_MODEL = 1024
HEAD_DIM = 64
N_HEADS_A = 4
WIDTH_A = 256
WIDTH_B = 768
CHUNK = 128
BLOCK = 128
DILATIONS = (1, 4, 16)
N_RES = 16
TILE = N_RES * BLOCK
D_FF = 2816
PLE_DIM = 256
EPS = 1e-6
ROPE_THETA = 10000.0
LANES = 128
SUBLANES = 8

ROWS = 512
M_PER_STEP = ROWS // N_RES
FF_CHUNK = 256
ROW_GROUPS = 2
PAIRS_PER_STEP = 2
PIECE = 32
LOG2_E = 1.4426950408889634
MASKED = -3e38

F32 = jnp.float32
BF16 = jnp.bfloat16


def _rms(x, g):
    return x * lax.rsqrt(jnp.mean(x * x, axis=-1, keepdims=True) + EPS) * g


def _dot(a, b):
    return jnp.dot(a, b, preferred_element_type=F32)


def _resident(shape):
    return pl.BlockSpec(shape, lambda *_: (0,) * len(shape), pipeline_mode=pl.Buffered(1))


def _to_residue_major(t, c, slab_ref, out_ref):
    halves = t.reshape(M_PER_STEP, 2, SUBLANES, LANES)
    for a in range(2):
        slab_ref[a, c] = halves[:, a].reshape(M_PER_STEP * SUBLANES, LANES)
    for r in range(N_RES):
        out_ref[r, :, c * LANES:(c + 1) * LANES] = (
            slab_ref[r // SUBLANES, c, pl.ds(r % SUBLANES, M_PER_STEP, stride=SUBLANES), :])


def _pre_kernel(x_ref, g_ref, inv_ref, w_ref, sw_ref, sb_ref, sg_ref, na_ref,
                ya_ref, q_ref, k_ref, v_ref, cu_ref, su_ref, qs_ref, ks_ref, vs_ref):
    i = pl.program_id(1)
    lane = lax.broadcasted_iota(jnp.int32, (1, LANES), 1)
    sign = jnp.where((lane % HEAD_DIM) < (HEAD_DIM // 2), -1.0, 1.0)

    @pl.when((pl.program_id(0) == 0) & (i == 0))
    def _():
        u = lax.broadcasted_iota(jnp.int32, (ROWS, LANES), 0).astype(F32)
        ang = u * inv_ref[...]
        cu_ref[...] = jnp.cos(ang)
        su_ref[...] = jnp.sin(ang) * sign

    ang0 = (i * ROWS).astype(F32) * inv_ref[...]
    ct = jnp.cos(ang0)
    st = jnp.sin(ang0) * sign
    cos = ct * cu_ref[...] - st * su_ref[...]
    sin = st * cu_ref[...] + ct * su_ref[...]
    first_half = (lax.broadcasted_iota(jnp.int32, (ROWS, LANES), 1) % HEAD_DIM) < (HEAD_DIM // 2)

    hn = _rms(x_ref[...], g_ref[...]).astype(BF16)

    def project(col0, slab_ref, out_ref, rope, scale):
        t = _dot(hn, w_ref[:, col0:col0 + WIDTH_B])
        for c in range(WIDTH_B // LANES):
            tc = t[:, c * LANES:(c + 1) * LANES]
            if rope:
                rot = jnp.where(first_half,
                                pltpu.roll(tc, LANES - HEAD_DIM // 2, 1),
                                pltpu.roll(tc, HEAD_DIM // 2, 1))
                tc = tc * cos + rot * sin
            if scale != 1.0:
                tc = tc * scale
            _to_residue_major(tc, c, slab_ref, out_ref)

    uv = _dot(hn, w_ref[:, :2 * WIDTH_A])
    u = jax.nn.gelu(uv[:, :WIDTH_A])
    v = jax.nn.gelu(uv[:, WIDTH_A:])
    mu = jnp.mean(v, axis=-1, keepdims=True)
    d = v - mu
    var = jnp.mean(d * d, axis=-1, keepdims=True)
    vf = d * lax.rsqrt(var + EPS) * sg_ref[...]

    project(2 * WIDTH_A, qs_ref, q_ref, True, HEAD_DIM ** -0.5 * LOG2_E)
    project(2 * WIDTH_A + WIDTH_B, ks_ref, k_ref, True, 1.0)

    row = lax.broadcasted_iota(jnp.int32, (CHUNK, CHUNK), 0)
    col = lax.broadcasted_iota(jnp.int32, (CHUNK, CHUNK), 1)
    w4 = jnp.concatenate(
        [jnp.where(col <= row, sw_ref[h], 0.0) for h in range(N_HEADS_A)], axis=0).astype(BF16)
    lane_a = lax.broadcasted_iota(jnp.int32, (CHUNK, WIDTH_A), 1)
    for c in range(ROWS // CHUNK):
        rows = slice(c * CHUNK, (c + 1) * CHUNK)
        mm = _dot(w4, vf[rows].astype(BF16))
        mixed = mm[3 * CHUNK:]
        for h in (2, 1, 0):
            mixed = jnp.where(lane_a < (h + 1) * HEAD_DIM, mm[h * CHUNK:(h + 1) * CHUNK], mixed)
        ya = u[rows] * (mixed + sb_ref[...])
        ya_ref[rows, :] = _rms(ya, na_ref[...]).astype(ya_ref.dtype)

    project(2 * WIDTH_A + 2 * WIDTH_B, vs_ref, v_ref, False, 1.0)


def _branch_chunks(dilation, base, idx):
    if dilation == 16:
        return [(base + idx * BLOCK, BLOCK)]
    if dilation == 4:
        r4, jj = idx % 4, idx // 4
        return [(base + (4 * c + r4) * BLOCK + 32 * jj, 32) for c in range(4)]
    return [(base + r * BLOCK + SUBLANES * idx, SUBLANES) for r in range(N_RES)]


def _prev_block(dilation, idx, cur, prv):
    if dilation == 16:
        return prv, idx
    n_per_res = N_RES // dilation
    if dilation == 4:
        r4, jj = idx % 4, idx // 4
        base = jnp.where(jj > 0, cur, prv)
        return base, r4 + 4 * ((jj + n_per_res - 1) % n_per_res)
    base = jnp.where(idx > 0, cur, prv)
    return base, (idx + n_per_res - 1) % n_per_res


def _sub_index(dilation, a):
    if dilation == 16:
        return a
    if dilation == 4:
        return 4 * (a % 32) + a // 32
    return N_RES * (a % SUBLANES) + a // SUBLANES


def _rows(start, n):
    return pl.ds(start if isinstance(start, int) else pl.multiple_of(start, SUBLANES), n)


def _load_rows(ref, chunks, lanes):
    parts = [ref[_rows(s, n), lanes] for s, n in chunks]
    return parts[0] if len(parts) == 1 else jnp.concatenate(parts, axis=0)


def _store_rows(ref, chunks, lanes, val):
    off = 0
    for s, n in chunks:
        ref[_rows(s, n), lanes] = val[off:off + n]
        off += n


def _row_pieces(chunks, piece):
    flat = []
    for s, n in chunks:
        step = min(n, piece)
        flat += [(s + o, step) for o in range(0, n, step)]
    per_group = piece // flat[0][1]
    return [flat[g:g + per_group] for g in range(0, len(flat), per_group)]


def _attn_kernel(q_ref, k_ref, v_ref, o_ref, kk_ref, vv_ref, acc_ref, m_ref, l_ref, bias_ref,
                 eye_ref, s_ref, p_ref, mb_ref):
    j = pl.program_id(2)
    ring = j % 2
    cur = pl.multiple_of(ring * TILE, TILE)
    prv = pl.multiple_of((1 - ring) * TILE, TILE)

    @pl.when(j == 0)
    def _():
        kk_ref[pl.ds(prv, TILE), :] = jnp.zeros((TILE, kk_ref.shape[1]), F32)
        vv_ref[pl.ds(prv, TILE), :] = jnp.zeros((TILE, vv_ref.shape[1]), F32)

    kk_ref[pl.ds(cur, TILE), :] = k_ref[...]
    vv_ref[pl.ds(cur, TILE), :] = v_ref[...]

    @pl.when((pl.program_id(0) == 0) & (pl.program_id(1) == 0) & (j == 0))
    def _():
        kc = lax.broadcasted_iota(jnp.int32, (2 * BLOCK, BLOCK), 0)
        qa = lax.broadcasted_iota(jnp.int32, (2 * BLOCK, BLOCK), 1)
        eye_ref[...] = jnp.where(kc % BLOCK == qa, 1.0, 0.0).astype(BF16)
        for bi, dilation in enumerate(DILATIONS):
            dist = (BLOCK + _sub_index(dilation, qa)
                    - _sub_index(dilation, kc % BLOCK) - BLOCK * (kc // BLOCK))
            band = (dist >= 0) & (dist <= BLOCK)
            bias_ref[2 * bi] = jnp.where(band, 0.0, MASKED).astype(BF16)
            bias_ref[2 * bi + 1] = jnp.where(band & (kc >= BLOCK), 0.0, MASKED).astype(BF16)

    n_blk = TILE // BLOCK
    n_units = len(DILATIONS) * n_blk
    pair_lanes = [slice(p * LANES, (p + 1) * LANES) for p in range(PAIRS_PER_STEP)]
    n_pieces = BLOCK // PIECE

    def key_chunks(bi, idx):
        dilation = DILATIONS[bi]
        pbase, pidx = _prev_block(dilation, idx, cur, prv)
        return _branch_chunks(dilation, pbase, pidx), _branch_chunks(dilation, cur, idx)

    def scores(bi, idx, slot):
        dilation = DILATIONS[bi]
        if dilation == 16:
            first = j == 0
        elif dilation == 4:
            first = (j == 0) & (idx // 4 == 0)
        else:
            first = (j == 0) & (idx == 0)
        mask_t = bias_ref[2 * bi + first.astype(jnp.int32)]
        q_chunks = _branch_chunks(dilation, 0, idx)
        kp_chunks, kc_chunks = key_chunks(bi, idx)
        low = lax.broadcasted_iota(jnp.int32, (BLOCK, LANES), 1) < HEAD_DIM
        for p, lanes in enumerate(pair_lanes):
            q = _load_rows(q_ref, q_chunks, lanes)
            lhs = jnp.concatenate([jnp.where(low, q, 0.0), jnp.where(low, 0.0, q)],
                                  axis=0).astype(BF16)
            kb = jnp.concatenate([_load_rows(kk_ref, kp_chunks, lanes),
                                  _load_rows(kk_ref, kc_chunks, lanes)], axis=0).astype(BF16)
            s_ref[slot, p] = lax.dot_general(
                jnp.concatenate([lhs, eye_ref[...]], axis=1), jnp.concatenate([kb, mask_t], axis=1),
                (((1,), (1,)), ((), ())), preferred_element_type=F32)

    def softmax(bi, idx, slot):
        low = lax.broadcasted_iota(jnp.int32, (PIECE, LANES), 1) < HEAD_DIM
        for p in range(PAIRS_PER_STEP):
            for c in range(n_pieces):
                rows = slice(c * PIECE, (c + 1) * PIECE)
                ms = []
                for h in range(2):
                    head_rows = slice(h * BLOCK + c * PIECE, h * BLOCK + (c + 1) * PIECE)
                    sh = s_ref[slot, p, head_rows, :]
                    mh = jnp.max(sh, axis=-1, keepdims=True)
                    ms.append(mh)
                    p_ref[slot, p, head_rows, :] = jnp.exp2(sh - mh).astype(BF16)
                mb_ref[slot, p, rows, :] = jnp.where(low, ms[0], ms[1])

    def combine(bi, idx, slot):
        row_groups = _row_pieces(_branch_chunks(DILATIONS[bi], 0, idx), PIECE)
        kp_chunks, kc_chunks = key_chunks(bi, idx)
        low = lax.broadcasted_iota(jnp.int32, (PIECE, 2 * LANES), 1) % LANES < HEAD_DIM
        for p, lanes in enumerate(pair_lanes):
            vb =jnp.concatenate([_load_rows(vv_ref, kp_chunks, lanes),
                                  _load_rows(vv_ref, kc_chunks, lanes)], axis=0).astype(BF16)
            pv = _dot(p_ref[slot, p], jnp.concatenate([vb, jnp.ones_like(vb)], axis=1))
            for c, chunks in enumerate(row_groups):
                rows = slice(c * PIECE, (c + 1) * PIECE)
                both = jnp.where(low, pv[c * PIECE:(c + 1) * PIECE],
                                 pv[BLOCK + c * PIECE:BLOCK + (c + 1) * PIECE])
                out, l_b = both[:, :LANES], both[:, LANES:]
                m_b = mb_ref[slot, p, rows, :]
                if bi == 0:
                    _store_rows(acc_ref, chunks, lanes, out)
                    _store_rows(m_ref, chunks, lanes, m_b)
                    _store_rows(l_ref, chunks, lanes, l_b)
                    continue
                m_o = _load_rows(m_ref, chunks, lanes)
                m_n = jnp.maximum(m_o, m_b)
                a_o = jnp.exp2(m_o - m_n)
                a_b = jnp.exp2(m_b - m_n)
                acc = a_o * _load_rows(acc_ref, chunks, lanes) + a_b * out
                l_n = a_o * _load_rows(l_ref, chunks, lanes) + a_b * l_b
                if bi == len(DILATIONS) - 1:
                    _store_rows(o_ref, chunks, lanes, acc / l_n)
                else:
                    _store_rows(acc_ref, chunks, lanes, acc)
                    _store_rows(m_ref, chunks, lanes, m_n)
                    _store_rows(l_ref, chunks, lanes, l_n)

    def step(slot, bi_c, idx_c, bi_m, idx_m, bi_s, idx_s):
        combine(bi_c, idx_c, slot)
        if bi_s is not None:
            scores(bi_s, idx_s, slot)
        if bi_m is not None:
            softmax(bi_m, idx_m, 1 - slot)

    unit = lambda g: (g // n_blk, g % n_blk) if g < n_units else (None, None)
    scores(*unit(0), 0)
    softmax(*unit(0), 0)
    scores(*unit(1), 1)
    for g in range(n_units):
        step(g % 2, *unit(g), *unit(g + 1), *unit(g + 2))


def _post_kernel(x_ref, p_ref, ya_ref, yb_ref, nb_ref, woa_ref, wob_ref, gf_ref, wg_ref, wu_ref,
                 wd_ref, gp_ref, wpg_ref, wpp_ref, gz_ref, o_ref, slab_ref, act_ref):
    for r in range(N_RES):
        for c in range(WIDTH_B // LANES):
            slab_ref[c, pl.ds(r, M_PER_STEP, stride=N_RES), :] = yb_ref[r, :, c * LANES:(c + 1) * LANES]
    groups = [slice(g * (ROWS // ROW_GROUPS), (g + 1) * (ROWS // ROW_GROUPS)) for g in range(ROW_GROUPS)]
    pe = [_dot(p_ref[rows, :].astype(BF16), wpp_ref[...]) for rows in groups]
    ha = [x_ref[rows, :] + _dot(ya_ref[rows, :], woa_ref[...]) for rows in groups]
    ybn = [_rms(jnp.concatenate([slab_ref[c, rows, :] for c in range(WIDTH_B // LANES)], axis=1),
                nb_ref[...]).astype(BF16) for rows in groups]
    h = [hg + _dot(y, wob_ref[...]) for hg, y in zip(ha, ybn)]

    hn = [_rms(hg, gf_ref[...]).astype(BF16) for hg in h]
    for rows, hg in zip(groups, hn):
        for c in range(D_FF // FF_CHUNK):
            cols = slice(c * FF_CHUNK, (c + 1) * FF_CHUNK)
            gate = _dot(hg, wg_ref[:, cols])
            act_ref[rows, cols] = (jax.nn.silu(gate) * _dot(hg, wu_ref[:, cols])).astype(BF16)
    h = [hg + _dot(act_ref[rows, :], wd_ref[...]) for rows, hg in zip(groups, h)]

    gate = [jax.nn.sigmoid(_dot(_rms(hg, gp_ref[...]).astype(BF16), wpg_ref[...])) for hg in h]
    for rows, hg, gg, pg in zip(groups, h, gate, pe):
        o_ref[rows, :] = _rms(hg + gg * pg, gz_ref[...])


def _params(vmem_mib, n_axes):
    return pltpu.CompilerParams(dimension_semantics=("arbitrary",) * n_axes,
                                vmem_limit_bytes=vmem_mib * 1024 * 1024)


def kernel(x, p, mix_norm_g, w_in, sgu_w, sgu_b, sgu_norm_g, out_norm_a, out_norm_b, w_out,
           ffn_norm_g, w_gate, w_up, w_down, ple_norm_g, w_ple_gate, w_ple_proj, final_norm_g):
    b, s, d = x.shape
    assert d == D_MODEL and s % TILE == 0 and w_in.shape[0] == 1
    n_tiles = s // TILE
    n_tok = b * s
    steps_per_seq = s // ROWS
    steps_per_tile = TILE // ROWS
    row = lambda a: a.reshape(1, -1)

    x2 = x.reshape(n_tok, D_MODEL)
    tok_spec = lambda width: pl.BlockSpec((ROWS, width), lambda i: (i, 0))

    half = HEAD_DIM // 2
    inv = ROPE_THETA ** (-(jnp.arange(LANES) % half).astype(F32) / half)
    sgu_bias = jnp.repeat(sgu_b[0].T, HEAD_DIM, axis=1)
    rm_shape = jax.ShapeDtypeStruct((b, n_tiles, N_RES, BLOCK, WIDTH_B), F32)
    rm_spec = pl.BlockSpec((None, None, N_RES, M_PER_STEP, WIDTH_B),
                           lambda bi, i: (bi, i // steps_per_tile, 0, i % steps_per_tile, 0))
    seq_spec = lambda width: pl.BlockSpec((ROWS, width), lambda bi, i: (bi * steps_per_seq + i, 0))
    slab = pltpu.VMEM((2, WIDTH_B // LANES, ROWS // 2, LANES), F32)
    n_in = w_in.shape[2]
    y_a, q, k, v = pl.pallas_call(
        _pre_kernel,
        grid=(b, steps_per_seq),
        in_specs=[seq_spec(D_MODEL), _resident((1, D_MODEL)), _resident((1, LANES)),
                  _resident((D_MODEL, n_in)), _resident((N_HEADS_A, CHUNK, CHUNK)),
                  _resident((CHUNK, WIDTH_A)), _resident((1, WIDTH_A)), _resident((1, WIDTH_A))],
        out_specs=[seq_spec(WIDTH_A), rm_spec, rm_spec, rm_spec],
        out_shape=[jax.ShapeDtypeStruct((n_tok, WIDTH_A), BF16), rm_shape, rm_shape, rm_shape],
        scratch_shapes=[pltpu.VMEM((ROWS, LANES), F32), pltpu.VMEM((ROWS, LANES), F32), slab, slab, slab],
        compiler_params=_params(48, 2),
    )(x2, row(mix_norm_g[0]), row(inv), w_in[0].astype(BF16), sgu_w[0], sgu_bias,
      row(sgu_norm_g[0]), row(out_norm_a[0]))

    step_lanes = PAIRS_PER_STEP * LANES
    tile_shape = (b, n_tiles, TILE, WIDTH_B)
    att_spec = pl.BlockSpec((None, None, TILE, step_lanes), lambda bi, ci, ji: (bi, ji, 0, ci))
    y_b = pl.pallas_call(
        _attn_kernel,
        grid=(b, WIDTH_B // step_lanes, n_tiles),
        in_specs=[att_spec, att_spec, att_spec],
        out_specs=att_spec,
        out_shape=jax.ShapeDtypeStruct(tile_shape, F32),
        scratch_shapes=[pltpu.VMEM((2 * TILE, step_lanes), F32), pltpu.VMEM((2 * TILE, step_lanes), F32),
                        pltpu.VMEM((TILE, step_lanes), F32), pltpu.VMEM((TILE, step_lanes), F32),
                        pltpu.VMEM((TILE, step_lanes), F32),
                        pltpu.VMEM((2 * len(DILATIONS), 2 * BLOCK, BLOCK), BF16),
                        pltpu.VMEM((2 * BLOCK, BLOCK), BF16),
                        pltpu.VMEM((2, PAIRS_PER_STEP, 2 * BLOCK, 2 * BLOCK), F32),
                        pltpu.VMEM((2, PAIRS_PER_STEP, 2 * BLOCK, 2 * BLOCK), BF16),
                        pltpu.VMEM((2, PAIRS_PER_STEP, BLOCK, LANES), F32)],
        compiler_params=_params(48, 3),
    )(q.reshape(tile_shape), k.reshape(tile_shape), v.reshape(tile_shape))

    w_out0 = w_out[0].astype(BF16)
    out = pl.pallas_call(
        _post_kernel,
        grid=(n_tok // ROWS,),
        in_specs=[
            tok_spec(D_MODEL), tok_spec(PLE_DIM), tok_spec(WIDTH_A),
            pl.BlockSpec((None, None, N_RES, M_PER_STEP, WIDTH_B),
                         lambda i: (i // steps_per_seq, (i % steps_per_seq) // steps_per_tile, 0,
                                    i % steps_per_tile, 0)),
            _resident((1, WIDTH_B)),
            _resident((WIDTH_A, D_MODEL)), _resident((WIDTH_B, D_MODEL)),
            _resident((1, D_MODEL)),
            _resident((D_MODEL, D_FF)), _resident((D_MODEL, D_FF)), _resident((D_FF, D_MODEL)),
            _resident((1, D_MODEL)),
            _resident((D_MODEL, D_MODEL)), _resident((PLE_DIM, D_MODEL)),
            _resident((1, D_MODEL)),
        ],
        out_specs=tok_spec(D_MODEL),
        out_shape=jax.ShapeDtypeStruct((n_tok, D_MODEL), x.dtype),
        scratch_shapes=[pltpu.VMEM((WIDTH_B // LANES, ROWS, LANES), F32), pltpu.VMEM((ROWS, D_FF), BF16)],
        compiler_params=_params(56, 1),
    )(x2, p.reshape(n_tok, PLE_DIM), y_a, y_b.reshape(b, n_tiles, N_RES, BLOCK, WIDTH_B),
      row(out_norm_b[0]), w_out0[:WIDTH_A], w_out0[WIDTH_A:], row(ffn_norm_g[0]),
      w_gate[0].astype(BF16), w_up[0].astype(BF16), w_down[0].astype(BF16),
      row(ple_norm_g[0]), w_ple_gate[0].astype(BF16), w_ple_proj[0].astype(BF16),
      row(final_norm_g))
    return out.reshape(b, s, D_MODEL)
```

```python
import jax
import jax.numpy as jnp
from jax import lax
from jax.experimental import pallas as pl
from jax.experimental.pallas import tpu as pltpu

D_MODEL = 1024
HEAD_DIM = 64
N_HEADS_A = 4
WIDTH_A = 256
WIDTH_B = 768
CHUNK = 128
BLOCK = 128
DILATIONS = (1, 4, 16)
N_RES = 16
TILE = N_RES * BLOCK
D_FF = 2816
PLE_DIM = 256
EPS = 1e-6
ROPE_THETA = 10000.0
LANES = 128
SUBLANES = 8

ROWS = 512
M_PER_STEP = ROWS // N_RES
FF_CHUNK = 256
ROW_GROUPS = 2
PAIRS_PER_STEP = 2
PIECE = 32
SCORE_LEAD = 2
SOFTMAX_LEAD = 1
N_SLOTS = max(SCORE_LEAD - SOFTMAX_LEAD, SOFTMAX_LEAD) + 1
LOG2_E = 1.4426950408889634

F32 = jnp.float32
BF16 = jnp.bfloat16


def _rms(x, g):
    return x * lax.rsqrt(jnp.mean(x * x, axis=-1, keepdims=True) + EPS) * g


def _dot(a, b):
    return jnp.dot(a, b, preferred_element_type=F32)


def _resident(shape):
    return pl.BlockSpec(shape, lambda *_: (0,) * len(shape), pipeline_mode=pl.Buffered(1))


def _to_residue_major(t, c, m0, slab_ref, out_ref):
    n_m = t.shape[0] // N_RES
    halves = t.reshape(n_m, 2, SUBLANES, LANES)
    for a in range(2):
        slab_ref[a, c, m0 * SUBLANES:(m0 + n_m) * SUBLANES, :] = halves[:, a].reshape(n_m * SUBLANES, LANES)
    for r in range(N_RES):
        out_ref[r, m0:m0 + n_m, c * LANES:(c + 1) * LANES] = (
            slab_ref[r // SUBLANES, c, pl.ds(m0 * SUBLANES + r % SUBLANES, n_m, stride=SUBLANES), :])


def _pre_kernel(x_ref, g_ref, inv_ref, w_ref, sw_ref, sb_ref, sg_ref, na_ref,
                ya_ref, q_ref, k_ref, v_ref, cu_ref, su_ref, qs_ref, ks_ref, vs_ref):
    i = pl.program_id(1)
    lane = lax.broadcasted_iota(jnp.int32, (1, LANES), 1)
    sign = jnp.where((lane % HEAD_DIM) < (HEAD_DIM // 2), -1.0, 1.0)

    @pl.when((pl.program_id(0) == 0) & (i == 0))
    def _():
        u = lax.broadcasted_iota(jnp.int32, (ROWS, LANES), 0).astype(F32)
        ang = u * inv_ref[...]
        cu_ref[...] = jnp.cos(ang)
        su_ref[...] = jnp.sin(ang) * sign

    ang0 = (i * ROWS).astype(F32) * inv_ref[...]
    ct = jnp.cos(ang0)
    st = jnp.sin(ang0) * sign
    cos = ct * cu_ref[...] - st * su_ref[...]
    sin = st * cu_ref[...] + ct * su_ref[...]
    first_half = (lax.broadcasted_iota(jnp.int32, (ROWS, LANES), 1) % HEAD_DIM) < (HEAD_DIM // 2)

    hn = _rms(x_ref[...], g_ref[...]).astype(BF16)

    def project(col0, slab_ref, out_ref, rope, scale, row_groups=1):
        for g in range(row_groups):
            rows = slice(g * (ROWS // row_groups), (g + 1) * (ROWS // row_groups))
            t = _dot(hn[rows], w_ref[:, col0:col0 + WIDTH_B])
            for c in range(WIDTH_B // LANES):
                tc = t[:, c * LANES:(c + 1) * LANES]
                if rope:
                    rot = jnp.where(first_half[rows],
                                    pltpu.roll(tc, LANES - HEAD_DIM // 2, 1),
                                    pltpu.roll(tc, HEAD_DIM // 2, 1))
                    tc = tc * cos[rows] + rot * sin[rows]
                if scale != 1.0:
                    tc = tc * scale
                _to_residue_major(tc, c, g * (M_PER_STEP // row_groups), slab_ref, out_ref)

    uv = _dot(hn, w_ref[:, :2 * WIDTH_A])
    u = jax.nn.gelu(uv[:, :WIDTH_A])
    v = jax.nn.gelu(uv[:, WIDTH_A:])
    mu = jnp.mean(v, axis=-1, keepdims=True)
    d = v - mu
    var = jnp.mean(d * d, axis=-1, keepdims=True)
    vf = d * lax.rsqrt(var + EPS) * sg_ref[...]

    project(2 * WIDTH_A, qs_ref, q_ref, True, HEAD_DIM ** -0.5 * LOG2_E)
    project(2 * WIDTH_A + WIDTH_B, ks_ref, k_ref, True, 1.0)

    row = lax.broadcasted_iota(jnp.int32, (CHUNK, CHUNK), 0)
    col = lax.broadcasted_iota(jnp.int32, (CHUNK, CHUNK), 1)
    w4 = jnp.concatenate(
        [jnp.where(col <= row, sw_ref[h], 0.0) for h in range(N_HEADS_A)], axis=0).astype(BF16)
    lane_a = lax.broadcasted_iota(jnp.int32, (CHUNK, WIDTH_A), 1)
    for c in range(ROWS // CHUNK):
        rows = slice(c * CHUNK, (c + 1) * CHUNK)
        mm = _dot(w4, vf[rows].astype(BF16))
        mixed = mm[3 * CHUNK:]
        for h in (2, 1, 0):
            mixed = jnp.where(lane_a < (h + 1) * HEAD_DIM, mm[h * CHUNK:(h + 1) * CHUNK], mixed)
        ya = u[rows] * (mixed + sb_ref[...])
        ya_ref[rows, :] = _rms(ya, na_ref[...]).astype(ya_ref.dtype)

    project(2 * WIDTH_A + 2 * WIDTH_B, vs_ref, v_ref, False, 1.0, row_groups=2)


def _branch_chunks(dilation, base, idx):
    if dilation == 16:
        return [(base + idx * BLOCK, BLOCK)]
    if dilation == 4:
        r4, jj = idx % 4, idx // 4
        return [(base + (4 * c + r4) * BLOCK + 32 * jj, 32) for c in range(4)]
    return [(base + r * BLOCK + SUBLANES * idx, SUBLANES) for r in range(N_RES)]


def _prev_block(dilation, idx, cur, prv):
    if dilation == 16:
        return prv, idx
    n_per_res = N_RES // dilation
    if dilation == 4:
        r4, jj = idx % 4, idx // 4
        base = jnp.where(jj > 0, cur, prv)
        return base, r4 + 4 * ((jj + n_per_res - 1) % n_per_res)
    base = jnp.where(idx > 0, cur, prv)
    return base, (idx + n_per_res - 1) % n_per_res


def _sub_index(dilation, a):
    if dilation == 16:
        return a
    if dilation == 4:
        return 4 * (a % 32) + a // 32
    return N_RES * (a % SUBLANES) + a // SUBLANES


def _rows(start, n):
    return pl.ds(start if isinstance(start, int) else pl.multiple_of(start, SUBLANES), n)


def _load_rows(ref, chunks, lanes):
    parts = [ref[_rows(s, n), lanes] for s, n in chunks]
    return parts[0] if len(parts) == 1 else jnp.concatenate(parts, axis=0)


def _store_rows(ref, chunks, lanes, val):
    off = 0
    for s, n in chunks:
        ref[_rows(s, n), lanes] = val[off:off + n]
        off += n


def _row_pieces(chunks, piece):
    flat = []
    for s, n in chunks:
        step = min(n, piece)
        flat += [(s + o, step) for o in range(0, n, step)]
    per_group = piece // flat[0][1]
    return [flat[g:g + per_group] for g in range(0, len(flat), per_group)]


def _attn_kernel(q_ref, k_ref, v_ref, o_ref, kk_ref, vv_ref, acc_ref, m_ref, l_ref, bias_ref,
                 s_ref, p_ref, mb_ref):
    j = pl.program_id(2)
    ring = j % 2
    cur = pl.multiple_of(ring * TILE, TILE)
    prv = pl.multiple_of((1 - ring) * TILE, TILE)

    @pl.when(j == 0)
    def _():
        kk_ref[pl.ds(prv, TILE), :] = jnp.zeros((TILE, kk_ref.shape[1]), F32)
        vv_ref[pl.ds(prv, TILE), :] = jnp.zeros((TILE, vv_ref.shape[1]), F32)

    kk_ref[pl.ds(cur, TILE), :] = k_ref[...]
    vv_ref[pl.ds(cur, TILE), :] = v_ref[...]

    @pl.when((pl.program_id(0) == 0) & (pl.program_id(1) == 0) & (j == 0))
    def _():
        qa = lax.broadcasted_iota(jnp.int32, (BLOCK, 2 * BLOCK), 0)
        kc = lax.broadcasted_iota(jnp.int32, (BLOCK, 2 * BLOCK), 1)
        for bi, dilation in enumerate(DILATIONS):
            dist = (BLOCK + _sub_index(dilation, qa)
                    - _sub_index(dilation, kc % BLOCK) - BLOCK * (kc // BLOCK))
            band = (dist >= 0) & (dist <= BLOCK)
            bias_ref[2 * bi] = jnp.where(band, 0.0, -jnp.inf)
            bias_ref[2 * bi + 1] = jnp.where(band & (kc >= BLOCK), 0.0, -jnp.inf)

    n_blk = TILE // BLOCK
    n_units = len(DILATIONS) * n_blk
    pair_lanes = [slice(p * LANES, (p + 1) * LANES) for p in range(PAIRS_PER_STEP)]
    n_pieces = BLOCK // PIECE

    def key_chunks(bi, idx):
        dilation = DILATIONS[bi]
        pbase, pidx = _prev_block(dilation, idx, cur, prv)
        return _branch_chunks(dilation, pbase, pidx), _branch_chunks(dilation, cur, idx)

    def scores(bi, idx, slot):
        q_chunks = _branch_chunks(DILATIONS[bi], 0, idx)
        kp_chunks, kc_chunks = key_chunks(bi, idx)
        low = lax.broadcasted_iota(jnp.int32, (BLOCK, LANES), 1) < HEAD_DIM
        for p, lanes in enumerate(pair_lanes):
            q = _load_rows(q_ref, q_chunks, lanes)
            lhs = jnp.concatenate([jnp.where(low, q, 0.0), jnp.where(low, 0.0, q)],
                                  axis=0).astype(BF16)
            kb = jnp.concatenate([_load_rows(kk_ref, kp_chunks, lanes),
                                  _load_rows(kk_ref, kc_chunks, lanes)], axis=0).astype(BF16)
            s_ref[slot, p] = lax.dot_general(lhs, kb, (((1,), (1,)), ((), ())),
                                             preferred_element_type=F32)

    def softmax(bi, idx, slot):
        dilation = DILATIONS[bi]
        if dilation == 16:
            first = j == 0
        elif dilation == 4:
            first = (j == 0) & (idx // 4 == 0)
        else:
            first = (j == 0) & (idx == 0)
        table = 2 * bi + first.astype(jnp.int32)
        low = lax.broadcasted_iota(jnp.int32, (PIECE, LANES), 1) < HEAD_DIM
        for p in range(PAIRS_PER_STEP):
            for c in range(n_pieces):
                rows = slice(c * PIECE, (c + 1) * PIECE)
                bias = bias_ref[table, rows, :]
                ms = []
                for h in range(2):
                    head_rows = slice(h * BLOCK + c * PIECE, h * BLOCK + (c + 1) * PIECE)
                    sh = s_ref[slot, p, head_rows, :] + bias
                    mh = jnp.max(sh, axis=-1, keepdims=True)
                    ms.append(mh)
                    p_ref[slot, p, head_rows, :] = jnp.exp2(sh - mh)
                mb_ref[slot, p, rows, :] = jnp.where(low, ms[0], ms[1])

    def combine(bi, idx, slot):
        row_groups = _row_pieces(_branch_chunks(DILATIONS[bi], 0, idx), PIECE)
        kp_chunks, kc_chunks = key_chunks(bi, idx)
        low = lax.broadcasted_iota(jnp.int32, (PIECE, 2 * LANES), 1) % LANES < HEAD_DIM
        for p, lanes in enumerate(pair_lanes):
            vb = jnp.concatenate([_load_rows(vv_ref, kp_chunks, lanes),
                                  _load_rows(vv_ref, kc_chunks, lanes)], axis=0)
            pv = _dot(p_ref[slot, p], jnp.concatenate([vb, jnp.ones_like(vb)], axis=1))
            for c, chunks in enumerate(row_groups):
                rows = slice(c * PIECE, (c + 1) * PIECE)
                both = jnp.where(low, pv[c * PIECE:(c + 1) * PIECE],
                                 pv[BLOCK + c * PIECE:BLOCK + (c + 1) * PIECE])
                out, l_b = both[:, :LANES], both[:, LANES:]
                m_b = mb_ref[slot, p, rows, :]
                if bi == 0:
                    _store_rows(acc_ref, chunks, lanes, out)
                    _store_rows(m_ref, chunks, lanes, m_b)
                    _store_rows(l_ref, chunks, lanes, l_b)
                    continue
                m_o = _load_rows(m_ref, chunks, lanes)
                m_n = jnp.maximum(m_o, m_b)
                a_o = jnp.exp2(m_o - m_n)
                a_b = jnp.exp2(m_b - m_n)
                acc = a_o * _load_rows(acc_ref, chunks, lanes) + a_b * out
                l_n = a_o * _load_rows(l_ref, chunks, lanes) + a_b * l_b
                if bi == len(DILATIONS) - 1:
                    _store_rows(o_ref, chunks, lanes, acc / l_n)
                else:
                    _store_rows(acc_ref, chunks, lanes, acc)
                    _store_rows(m_ref, chunks, lanes, m_n)
                    _store_rows(l_ref, chunks, lanes, l_n)

    unit = lambda g: (g // n_blk, g % n_blk)
    for g in range(-SCORE_LEAD, n_units):
        if g >= 0:
            combine(*unit(g), g % N_SLOTS)
        if 0 <= g + SCORE_LEAD < n_units:
            scores(*unit(g + SCORE_LEAD), (g + SCORE_LEAD) % N_SLOTS)
        if 0 <= g + SOFTMAX_LEAD < n_units:
            softmax(*unit(g + SOFTMAX_LEAD), (g + SOFTMAX_LEAD) % N_SLOTS)


def _post_kernel(x_ref, p_ref, ya_ref, yb_ref, nb_ref, woa_ref, wob_ref, gf_ref, wg_ref, wu_ref,
                 wd_ref, gp_ref, wpg_ref, wpp_ref, gz_ref, o_ref, slab_ref, act_ref):
    for r in range(N_RES):
        for c in range(WIDTH_B // LANES):
            slab_ref[c, pl.ds(r, M_PER_STEP, stride=N_RES), :] = yb_ref[r, :, c * LANES:(c + 1) * LANES]
    groups = [slice(g * (ROWS // ROW_GROUPS), (g + 1) * (ROWS // ROW_GROUPS)) for g in range(ROW_GROUPS)]
    pe = [_dot(p_ref[rows, :].astype(BF16), wpp_ref[...]) for rows in groups]
    ha = [x_ref[rows, :] + _dot(ya_ref[rows, :], woa_ref[...]) for rows in groups]
    ybn = [_rms(jnp.concatenate([slab_ref[c, rows, :] for c in range(WIDTH_B // LANES)], axis=1),
                nb_ref[...]).astype(BF16) for rows in groups]
    h = [hg + _dot(y, wob_ref[...]) for hg, y in zip(ha, ybn)]

    hn = [_rms(hg, gf_ref[...]).astype(BF16) for hg in h]
    for rows, hg in zip(groups, hn):
        for c in range(D_FF // FF_CHUNK):
            cols = slice(c * FF_CHUNK, (c + 1) * FF_CHUNK)
            gate = _dot(hg, wg_ref[:, cols])
            act_ref[rows, cols] = (jax.nn.silu(gate) * _dot(hg, wu_ref[:, cols])).astype(BF16)
    h = [hg + _dot(act_ref[rows, :], wd_ref[...]) for rows, hg in zip(groups, h)]

    gate = [jax.nn.sigmoid(_dot(_rms(hg, gp_ref[...]).astype(BF16), wpg_ref[...])) for hg in h]
    for rows, hg, gg, pg in zip(groups, h, gate, pe):
        o_ref[rows, :] = _rms(hg + gg * pg, gz_ref[...])


def _params(vmem_mib, n_axes):
    return pltpu.CompilerParams(dimension_semantics=("arbitrary",) * n_axes,
                                vmem_limit_bytes=vmem_mib * 1024 * 1024)


def kernel(x, p, mix_norm_g, w_in, sgu_w, sgu_b, sgu_norm_g, out_norm_a, out_norm_b, w_out,
           ffn_norm_g, w_gate, w_up, w_down, ple_norm_g, w_ple_gate, w_ple_proj, final_norm_g):
    b, s, d = x.shape
    assert d == D_MODEL and s % TILE == 0 and w_in.shape[0] == 1
    n_tiles = s // TILE
    n_tok = b * s
    steps_per_seq = s // ROWS
    steps_per_tile = TILE // ROWS
    row = lambda a: a.reshape(1, -1)

    x2 = x.reshape(n_tok, D_MODEL)
    tok_spec = lambda width: pl.BlockSpec((ROWS, width), lambda i: (i, 0))

    half = HEAD_DIM // 2
    inv = ROPE_THETA ** (-(jnp.arange(LANES) % half).astype(F32) / half)
    sgu_bias = jnp.repeat(sgu_b[0].T, HEAD_DIM, axis=1)
    rm_shape = jax.ShapeDtypeStruct((b, n_tiles, N_RES, BLOCK, WIDTH_B), F32)
    rm_spec = pl.BlockSpec((None, None, N_RES, M_PER_STEP, WIDTH_B),
                           lambda bi, i: (bi, i // steps_per_tile, 0, i % steps_per_tile, 0))
    seq_spec = lambda width: pl.BlockSpec((ROWS, width), lambda bi, i: (bi * steps_per_seq + i, 0))
    slab = pltpu.VMEM((2, WIDTH_B // LANES, ROWS // 2, LANES), F32)
    n_in = w_in.shape[2]
    y_a, q, k, v = pl.pallas_call(
        _pre_kernel,
        grid=(b, steps_per_seq),
        in_specs=[seq_spec(D_MODEL), _resident((1, D_MODEL)), _resident((1, LANES)),
                  _resident((D_MODEL, n_in)), _resident((N_HEADS_A, CHUNK, CHUNK)),
                  _resident((CHUNK, WIDTH_A)), _resident((1, WIDTH_A)), _resident((1, WIDTH_A))],
        out_specs=[seq_spec(WIDTH_A), rm_spec, rm_spec, rm_spec],
        out_shape=[jax.ShapeDtypeStruct((n_tok, WIDTH_A), BF16), rm_shape, rm_shape, rm_shape],
        scratch_shapes=[pltpu.VMEM((ROWS, LANES), F32), pltpu.VMEM((ROWS, LANES), F32), slab, slab, slab],
        compiler_params=_params(48, 2),
    )(x2, row(mix_norm_g[0]), row(inv), w_in[0].astype(BF16), sgu_w[0], sgu_bias,
      row(sgu_norm_g[0]), row(out_norm_a[0]))

    step_lanes = PAIRS_PER_STEP * LANES
    tile_shape = (b, n_tiles, TILE, WIDTH_B)
    att_spec = pl.BlockSpec((None, None, TILE, step_lanes), lambda bi, ci, ji: (bi, ji, 0, ci))
    y_b = pl.pallas_call(
        _attn_kernel,
        grid=(b, WIDTH_B // step_lanes, n_tiles),
        in_specs=[att_spec, att_spec, att_spec],
        out_specs=att_spec,
        out_shape=jax.ShapeDtypeStruct(tile_shape, F32),
        scratch_shapes=[pltpu.VMEM((2 * TILE, step_lanes), F32), pltpu.VMEM((2 * TILE, step_lanes), F32),
                        pltpu.VMEM((TILE, step_lanes), F32), pltpu.VMEM((TILE, step_lanes), F32),
                        pltpu.VMEM((TILE, step_lanes), F32),
                        pltpu.VMEM((2 * len(DILATIONS), BLOCK, 2 * BLOCK), F32),
                        pltpu.VMEM((N_SLOTS, PAIRS_PER_STEP, 2 * BLOCK, 2 * BLOCK), F32),
                        pltpu.VMEM((N_SLOTS, PAIRS_PER_STEP, 2 * BLOCK, 2 * BLOCK), F32),
                        pltpu.VMEM((N_SLOTS, PAIRS_PER_STEP, BLOCK, LANES), F32)],
        compiler_params=_params(48, 3),
    )(q.reshape(tile_shape), k.reshape(tile_shape), v.reshape(tile_shape))

    w_out0 = w_out[0].astype(BF16)
    out = pl.pallas_call(
        _post_kernel,
        grid=(n_tok // ROWS,),
        in_specs=[
            tok_spec(D_MODEL), tok_spec(PLE_DIM), tok_spec(WIDTH_A),
            pl.BlockSpec((None, None, N_RES, M_PER_STEP, WIDTH_B),
                         lambda i: (i // steps_per_seq, (i % steps_per_seq) // steps_per_tile, 0,
                                    i % steps_per_tile, 0)),
            _resident((1, WIDTH_B)),
            _resident((WIDTH_A, D_MODEL)), _resident((WIDTH_B, D_MODEL)),
            _resident((1, D_MODEL)),
            _resident((D_MODEL, D_FF)), _resident((D_MODEL, D_FF)), _resident((D_FF, D_MODEL)),
            _resident((1, D_MODEL)),
            _resident((D_MODEL, D_MODEL)), _resident((PLE_DIM, D_MODEL)),
            _resident((1, D_MODEL)),
        ],
        out_specs=tok_spec(D_MODEL),
        out_shape=jax.ShapeDtypeStruct((n_tok, D_MODEL), x.dtype),
        scratch_shapes=[pltpu.VMEM((WIDTH_B // LANES, ROWS, LANES), F32), pltpu.VMEM((ROWS, D_FF), BF16)],
        compiler_params=_params(56, 1),
    )(x2, p.reshape(n_tok, PLE_DIM), y_a, y_b.reshape(b, n_tiles, N_RES, BLOCK, WIDTH_B),
      row(out_norm_b[0]), w_out0[:WIDTH_A], w_out0[WIDTH_A:], row(ffn_norm_g[0]),
      w_gate[0].astype(BF16), w_up[0].astype(BF16), w_down[0].astype(BF16),
      row(ple_norm_g[0]), w_ple_gate[0].astype(BF16), w_ple_proj[0].astype(BF16),
      row(final_norm_g))
    return out.reshape(b, s, D_MODEL)
```

```python
import jax
import jax.numpy as jnp
from jax import lax
from jax.experimental import pallas as pl
from jax.experimental.pallas import tpu as pltpu

D_MODEL = 1024
HEAD_DIM = 64
N_HEADS_A = 4
WIDTH_A = 256
WIDTH_B = 768
CHUNK = 128
BLOCK = 128
DILATIONS = (1, 4, 16)
N_RES = 16
TILE = N_RES * BLOCK
D_FF = 2816
PLE_DIM = 256
EPS = 1e-6
ROPE_THETA = 10000.0
LANES = 128
SUBLANES = 8

ROWS = 512
M_PER_STEP = ROWS // N_RES
FF_CHUNK = 256
ROW_GROUPS = 2
PAIRS_PER_STEP = 2
PIECE = 32
SCORE_LEAD = 2
SOFTMAX_LEAD = 1
N_SLOTS = max(SCORE_LEAD - SOFTMAX_LEAD, SOFTMAX_LEAD) + 1
LOG2_E = 1.4426950408889634

F32 = jnp.float32
BF16 = jnp.bfloat16


def _rms(x, g):
    return x * lax.rsqrt(jnp.mean(x * x, axis=-1, keepdims=True) + EPS) * g


def _dot(a, b):
    return jnp.dot(a, b, preferred_element_type=F32)


def _resident(shape):
    return pl.BlockSpec(shape, lambda *_: (0,) * len(shape), pipeline_mode=pl.Buffered(1))


def _to_residue_major(t, c, slab_ref, out_ref):
    halves = t.reshape(M_PER_STEP, 2, SUBLANES, LANES)
    for a in range(2):
        slab_ref[a, c] = halves[:, a].reshape(M_PER_STEP * SUBLANES, LANES)
    for r in range(N_RES):
        out_ref[r, :, c * LANES:(c + 1) * LANES] = (
            slab_ref[r // SUBLANES, c, pl.ds(r % SUBLANES, M_PER_STEP, stride=SUBLANES), :])


def _pre_kernel(x_ref, g_ref, inv_ref, w_ref, sw_ref, sb_ref, sg_ref, na_ref,
                ya_ref, q_ref, k_ref, v_ref, cu_ref, su_ref, qs_ref, ks_ref, vs_ref):
    i = pl.program_id(1)
    lane = lax.broadcasted_iota(jnp.int32, (1, LANES), 1)
    sign = jnp.where((lane % HEAD_DIM) < (HEAD_DIM // 2), -1.0, 1.0)

    @pl.when((pl.program_id(0) == 0) & (i == 0))
    def _():
        u = lax.broadcasted_iota(jnp.int32, (ROWS, LANES), 0).astype(F32)
        ang = u * inv_ref[...]
        cu_ref[...] = jnp.cos(ang)
        su_ref[...] = jnp.sin(ang) * sign

    ang0 = (i * ROWS).astype(F32) * inv_ref[...]
    ct = jnp.cos(ang0)
    st = jnp.sin(ang0) * sign
    cos = ct * cu_ref[...] - st * su_ref[...]
    sin = st * cu_ref[...] + ct * su_ref[...]
    first_half = (lax.broadcasted_iota(jnp.int32, (ROWS, LANES), 1) % HEAD_DIM) < (HEAD_DIM // 2)

    hn = _rms(x_ref[...], g_ref[...]).astype(BF16)

    def project(col0, slab_ref, out_ref, rope, scale):
        t = _dot(hn, w_ref[:, col0:col0 + WIDTH_B])
        for c in range(WIDTH_B // LANES):
            tc = t[:, c * LANES:(c + 1) * LANES]
            if rope:
                rot = jnp.where(first_half,
                                pltpu.roll(tc, LANES - HEAD_DIM // 2, 1),
                                pltpu.roll(tc, HEAD_DIM // 2, 1))
                tc = tc * cos + rot * sin
            if scale != 1.0:
                tc = tc * scale
            _to_residue_major(tc, c, slab_ref, out_ref)

    uv = _dot(hn, w_ref[:, :2 * WIDTH_A])
    u = jax.nn.gelu(uv[:, :WIDTH_A])
    v = jax.nn.gelu(uv[:, WIDTH_A:])
    mu = jnp.mean(v, axis=-1, keepdims=True)
    d = v - mu
    var = jnp.mean(d * d, axis=-1, keepdims=True)
    vf = d * lax.rsqrt(var + EPS) * sg_ref[...]

    project(2 * WIDTH_A, qs_ref, q_ref, True, HEAD_DIM ** -0.5 * LOG2_E)
    project(2 * WIDTH_A + WIDTH_B, ks_ref, k_ref, True, 1.0)

    row = lax.broadcasted_iota(jnp.int32, (CHUNK, CHUNK), 0)
    col = lax.broadcasted_iota(jnp.int32, (CHUNK, CHUNK), 1)
    w4 = jnp.concatenate(
        [jnp.where(col <= row, sw_ref[h], 0.0) for h in range(N_HEADS_A)], axis=0).astype(BF16)
    lane_a = lax.broadcasted_iota(jnp.int32, (CHUNK, WIDTH_A), 1)
    for c in range(ROWS // CHUNK):
        rows = slice(c * CHUNK, (c + 1) * CHUNK)
        mm = _dot(w4, vf[rows].astype(BF16))
        mixed = mm[3 * CHUNK:]
        for h in (2, 1, 0):
            mixed = jnp.where(lane_a < (h + 1) * HEAD_DIM, mm[h * CHUNK:(h + 1) * CHUNK], mixed)
        ya = u[rows] * (mixed + sb_ref[...])
        ya_ref[rows, :] = _rms(ya, na_ref[...]).astype(ya_ref.dtype)

    project(2 * WIDTH_A + 2 * WIDTH_B, vs_ref, v_ref, False, 1.0)


def _branch_chunks(dilation, idx):
    if dilation == 16:
        return [(idx * BLOCK, BLOCK)]
    if dilation == 4:
        r4, jj = idx % 4, idx // 4
        return [((4 * c + r4) * BLOCK + 32 * jj, 32) for c in range(4)]
    return [(r * BLOCK + SUBLANES * idx, SUBLANES) for r in range(N_RES)]


def _prev_block(dilation, idx):
    if dilation == 16:
        return True, idx
    n_per_res = N_RES // dilation
    if dilation == 4:
        r4, jj = idx % 4, idx // 4
        return jj == 0, r4 + 4 * ((jj + n_per_res - 1) % n_per_res)
    return idx == 0, (idx + n_per_res - 1) % n_per_res


def _sub_index(dilation, a):
    if dilation == 16:
        return a
    if dilation == 4:
        return 4 * (a % 32) + a // 32
    return N_RES * (a % SUBLANES) + a // SUBLANES


def _load_rows(ref, chunks, lanes):
    parts = [ref[s:s + n, lanes] for s, n in chunks]
    return parts[0] if len(parts) == 1 else jnp.concatenate(parts, axis=0)


def _store_rows(ref, chunks, lanes, val):
    off = 0
    for s, n in chunks:
        ref[s:s + n, lanes] = val[off:off + n]
        off += n


def _row_pieces(chunks, piece):
    flat = []
    for s, n in chunks:
        step = min(n, piece)
        flat += [(s + o, step) for o in range(0, n, step)]
    per_group = piece // flat[0][1]
    return [flat[g:g + per_group] for g in range(0, len(flat), per_group)]


def _attn_kernel(q_ref, k_ref, v_ref, o_ref, kp_ref, vp_ref, acc_ref, m_ref, l_ref, bias_ref,
                 s_ref, p_ref, mb_ref):
    j = pl.program_id(2)

    @pl.when(j == 0)
    def _():
        kp_ref[...] = jnp.zeros(kp_ref.shape, F32)
        vp_ref[...] = jnp.zeros(vp_ref.shape, F32)

    @pl.when((pl.program_id(0) == 0) & (pl.program_id(1) == 0) & (j == 0))
    def _():
        qa = lax.broadcasted_iota(jnp.int32, (BLOCK, 2 * BLOCK), 0)
        kc = lax.broadcasted_iota(jnp.int32, (BLOCK, 2 * BLOCK), 1)
        for bi, dilation in enumerate(DILATIONS):
            dist = (BLOCK + _sub_index(dilation, qa)
                    - _sub_index(dilation, kc % BLOCK) - BLOCK * (kc // BLOCK))
            band = (dist >= 0) & (dist <= BLOCK)
            bias_ref[2 * bi] = jnp.where(band, 0.0, -jnp.inf)
            bias_ref[2 * bi + 1] = jnp.where(band & (kc >= BLOCK), 0.0, -jnp.inf)

    n_blk = TILE // BLOCK
    n_units = len(DILATIONS) * n_blk
    pair_lanes = [slice(p * LANES, (p + 1) * LANES) for p in range(PAIRS_PER_STEP)]
    n_pieces = BLOCK // PIECE

    def key_block(bi, idx, cur_ref, prev_ref, lanes):
        dilation = DILATIONS[bi]
        in_prev_tile, pidx = _prev_block(dilation, idx)
        return jnp.concatenate(
            [_load_rows(prev_ref if in_prev_tile else cur_ref, _branch_chunks(dilation, pidx), lanes),
             _load_rows(cur_ref, _branch_chunks(dilation, idx), lanes)], axis=0)

    def scores(bi, idx, slot):
        q_chunks = _branch_chunks(DILATIONS[bi], idx)
        low = lax.broadcasted_iota(jnp.int32, (BLOCK, LANES), 1) < HEAD_DIM
        for p, lanes in enumerate(pair_lanes):
            q = _load_rows(q_ref, q_chunks, lanes)
            lhs = jnp.concatenate([jnp.where(low, q, 0.0), jnp.where(low, 0.0, q)],
                                  axis=0).astype(BF16)
            kb = key_block(bi, idx, k_ref, kp_ref, lanes).astype(BF16)
            for h in range(2):
                s_ref[slot, p, h * BLOCK:(h + 1) * BLOCK, :] = lax.dot_general(
                    lhs[h * BLOCK:(h + 1) * BLOCK], kb, (((1,), (1,)), ((), ())),
                    preferred_element_type=F32)

    def softmax(bi, idx, slot):
        dilation = DILATIONS[bi]
        if dilation == 16:
            first = j == 0
        elif dilation == 4:
            first = (j == 0) & (idx // 4 == 0)
        else:
            first = (j == 0) & (idx == 0)
        table = 2 * bi + first.astype(jnp.int32)
        low = lax.broadcasted_iota(jnp.int32, (PIECE, LANES), 1) < HEAD_DIM
        for p in range(PAIRS_PER_STEP):
            for c in range(n_pieces):
                rows = slice(c * PIECE, (c + 1) * PIECE)
                bias = bias_ref[table, rows, :]
                ms = []
                for h in range(2):
                    head_rows = slice(h * BLOCK + c * PIECE, h * BLOCK + (c + 1) * PIECE)
                    sh = s_ref[slot, p, head_rows, :] + bias
                    mh = jnp.max(sh, axis=-1, keepdims=True)
                    ms.append(mh)
                    p_ref[slot, p, head_rows, :] = jnp.exp2(sh - mh)
                mb_ref[slot, p, rows, :] = jnp.where(low, ms[0], ms[1])

    def combine(bi, idx, slot):
        row_groups = _row_pieces(_branch_chunks(DILATIONS[bi], idx), PIECE)
        low = lax.broadcasted_iota(jnp.int32, (PIECE, 2 * LANES), 1) % LANES < HEAD_DIM
        for p, lanes in enumerate(pair_lanes):
            vb = key_block(bi, idx, v_ref, vp_ref, lanes)
            vbx = jnp.concatenate([vb, jnp.ones_like(vb)], axis=1)
            pv = jnp.concatenate([_dot(p_ref[slot, p, h * BLOCK:(h + 1) * BLOCK, :], vbx) for h in range(2)],
                                 axis=0)
            for c, chunks in enumerate(row_groups):
                rows = slice(c * PIECE, (c + 1) * PIECE)
                both = jnp.where(low, pv[c * PIECE:(c + 1) * PIECE],
                                 pv[BLOCK + c * PIECE:BLOCK + (c + 1) * PIECE])
                out, l_b = both[:, :LANES], both[:, LANES:]
                m_b = mb_ref[slot, p, rows, :]
                if bi == 0:
                    _store_rows(acc_ref, chunks, lanes, out)
                    _store_rows(m_ref, chunks, lanes, m_b)
                    _store_rows(l_ref, chunks, lanes, l_b)
                    continue
                m_o = _load_rows(m_ref, chunks, lanes)
                m_n = jnp.maximum(m_o, m_b)
                a_o = jnp.exp2(m_o - m_n)
                a_b = jnp.exp2(m_b - m_n)
                acc = a_o * _load_rows(acc_ref, chunks, lanes) + a_b * out
                l_n = a_o * _load_rows(l_ref, chunks, lanes) + a_b * l_b
                if bi == len(DILATIONS) - 1:
                    _store_rows(o_ref, chunks, lanes, acc / l_n)
                else:
                    _store_rows(acc_ref, chunks, lanes, acc)
                    _store_rows(m_ref, chunks, lanes, m_n)
                    _store_rows(l_ref, chunks, lanes, l_n)

    unit = lambda g: (g // n_blk, g % n_blk)
    for g in range(-SCORE_LEAD, n_units):
        if g >= 0:
            combine(*unit(g), g % N_SLOTS)
        if 0 <= g + SCORE_LEAD < n_units:
            scores(*unit(g + SCORE_LEAD), (g + SCORE_LEAD) % N_SLOTS)
        if 0 <= g + SOFTMAX_LEAD < n_units:
            softmax(*unit(g + SOFTMAX_LEAD), (g + SOFTMAX_LEAD) % N_SLOTS)

    kp_ref[...] = k_ref[...]
    vp_ref[...] = v_ref[...]


def _post_kernel(x_ref, p_ref, ya_ref, yb_ref, nb_ref, woa_ref, wob_ref, gf_ref, wg_ref, wu_ref,
                 wd_ref, gp_ref, wpg_ref, wpp_ref, gz_ref, o_ref, slab_ref, act_ref):
    for r in range(N_RES):
        for c in range(WIDTH_B // LANES):
            slab_ref[c, pl.ds(r, M_PER_STEP, stride=N_RES), :] = yb_ref[r, :, c * LANES:(c + 1) * LANES]
    groups = [slice(g * (ROWS // ROW_GROUPS), (g + 1) * (ROWS // ROW_GROUPS)) for g in range(ROW_GROUPS)]
    pe = [_dot(p_ref[rows, :].astype(BF16), wpp_ref[...]) for rows in groups]
    ha = [x_ref[rows, :] + _dot(ya_ref[rows, :], woa_ref[...]) for rows in groups]
    ybn = [_rms(jnp.concatenate([slab_ref[c, rows, :] for c in range(WIDTH_B // LANES)], axis=1),
                nb_ref[...]).astype(BF16) for rows in groups]
    h = [hg + _dot(y, wob_ref[...]) for hg, y in zip(ha, ybn)]

    hn = [_rms(hg, gf_ref[...]).astype(BF16) for hg in h]
    for rows, hg in zip(groups, hn):
        for c in range(D_FF // FF_CHUNK):
            cols = slice(c * FF_CHUNK, (c + 1) * FF_CHUNK)
            gate = _dot(hg, wg_ref[:, cols])
            act_ref[rows, cols] = (jax.nn.silu(gate) * _dot(hg, wu_ref[:, cols])).astype(BF16)
    h = [hg + _dot(act_ref[rows, :], wd_ref[...]) for rows, hg in zip(groups, h)]

    gate = [jax.nn.sigmoid(_dot(_rms(hg, gp_ref[...]).astype(BF16), wpg_ref[...])) for hg in h]
    for rows, hg, gg, pg in zip(groups, h, gate, pe):
        o_ref[rows, :] = _rms(hg + gg * pg, gz_ref[...])


def _params(vmem_mib, n_axes):
    return pltpu.CompilerParams(dimension_semantics=("arbitrary",) * n_axes,
                                vmem_limit_bytes=vmem_mib * 1024 * 1024)


def kernel(x, p, mix_norm_g, w_in, sgu_w, sgu_b, sgu_norm_g, out_norm_a, out_norm_b, w_out,
           ffn_norm_g, w_gate, w_up, w_down, ple_norm_g, w_ple_gate, w_ple_proj, final_norm_g):
    b, s, d = x.shape
    assert d == D_MODEL and s % TILE == 0 and w_in.shape[0] == 1
    n_tiles = s // TILE
    n_tok = b * s
    steps_per_seq = s // ROWS
    steps_per_tile = TILE // ROWS
    row = lambda a: a.reshape(1, -1)

    x2 = x.reshape(n_tok, D_MODEL)
    tok_spec = lambda width: pl.BlockSpec((ROWS, width), lambda i: (i, 0))

    half = HEAD_DIM // 2
    inv = ROPE_THETA ** (-(jnp.arange(LANES) % half).astype(F32) / half)
    sgu_bias = jnp.repeat(sgu_b[0].T, HEAD_DIM, axis=1)
    rm_shape = jax.ShapeDtypeStruct((b, n_tiles, N_RES, BLOCK, WIDTH_B), F32)
    rm_spec = pl.BlockSpec((None, None, N_RES, M_PER_STEP, WIDTH_B),
                           lambda bi, i: (bi, i // steps_per_tile, 0, i % steps_per_tile, 0))
    seq_spec = lambda width: pl.BlockSpec((ROWS, width), lambda bi, i: (bi * steps_per_seq + i, 0))
    slab = pltpu.VMEM((2, WIDTH_B // LANES, ROWS // 2, LANES), F32)
    n_in = w_in.shape[2]
    y_a, q, k, v = pl.pallas_call(
        _pre_kernel,
        grid=(b, steps_per_seq),
        in_specs=[seq_spec(D_MODEL), _resident((1, D_MODEL)), _resident((1, LANES)),
                  _resident((D_MODEL, n_in)), _resident((N_HEADS_A, CHUNK, CHUNK)),
                  _resident((CHUNK, WIDTH_A)), _resident((1, WIDTH_A)), _resident((1, WIDTH_A))],
        out_specs=[seq_spec(WIDTH_A), rm_spec, rm_spec, rm_spec],
        out_shape=[jax.ShapeDtypeStruct((n_tok, WIDTH_A), BF16), rm_shape, rm_shape, rm_shape],
        scratch_shapes=[pltpu.VMEM((ROWS, LANES), F32), pltpu.VMEM((ROWS, LANES), F32), slab, slab, slab],
        compiler_params=_params(48, 2),
    )(x2, row(mix_norm_g[0]), row(inv), w_in[0].astype(BF16), sgu_w[0], sgu_bias,
      row(sgu_norm_g[0]), row(out_norm_a[0]))

    step_lanes = PAIRS_PER_STEP * LANES
    tile_shape = (b, n_tiles, TILE, WIDTH_B)
    att_spec = pl.BlockSpec((None, None, TILE, step_lanes), lambda bi, ci, ji: (bi, ji, 0, ci))
    y_b = pl.pallas_call(
        _attn_kernel,
        grid=(b, WIDTH_B // step_lanes, n_tiles),
        in_specs=[att_spec, att_spec, att_spec],
        out_specs=att_spec,
        out_shape=jax.ShapeDtypeStruct(tile_shape, F32),
        scratch_shapes=[pltpu.VMEM((TILE, step_lanes), F32), pltpu.VMEM((TILE, step_lanes), F32),
                        pltpu.VMEM((TILE, step_lanes), F32), pltpu.VMEM((TILE, step_lanes), F32),
                        pltpu.VMEM((TILE, step_lanes), F32),
                        pltpu.VMEM((2 * len(DILATIONS), BLOCK, 2 * BLOCK), F32),
                        pltpu.VMEM((N_SLOTS, PAIRS_PER_STEP, 2 * BLOCK, 2 * BLOCK), F32),
                        pltpu.VMEM((N_SLOTS, PAIRS_PER_STEP, 2 * BLOCK, 2 * BLOCK), F32),
                        pltpu.VMEM((N_SLOTS, PAIRS_PER_STEP, BLOCK, LANES), F32)],
        compiler_params=_params(48, 3),
    )(q.reshape(tile_shape), k.reshape(tile_shape), v.reshape(tile_shape))

    w_out0 = w_out[0].astype(BF16)
    out = pl.pallas_call(
        _post_kernel,
        grid=(n_tok // ROWS,),
        in_specs=[
            tok_spec(D_MODEL), tok_spec(PLE_DIM), tok_spec(WIDTH_A),
            pl.BlockSpec((None, None, N_RES, M_PER_STEP, WIDTH_B),
                         lambda i: (i // steps_per_seq, (i % steps_per_seq) // steps_per_tile, 0,
                                    i % steps_per_tile, 0)),
            _resident((1, WIDTH_B)),
            _resident((WIDTH_A, D_MODEL)), _resident((WIDTH_B, D_MODEL)),
            _resident((1, D_MODEL)),
            _resident((D_MODEL, D_FF)), _resident((D_MODEL, D_FF)), _resident((D_FF, D_MODEL)),
            _resident((1, D_MODEL)),
            _resident((D_MODEL, D_MODEL)), _resident((PLE_DIM, D_MODEL)),
            _resident((1, D_MODEL)),
        ],
        out_specs=tok_spec(D_MODEL),
        out_shape=jax.ShapeDtypeStruct((n_tok, D_MODEL), x.dtype),
        scratch_shapes=[pltpu.VMEM((WIDTH_B // LANES, ROWS, LANES), F32), pltpu.VMEM((ROWS, D_FF), BF16)],
        compiler_params=_params(56, 1),
    )(x2, p.reshape(n_tok, PLE_DIM), y_a, y_b.reshape(b, n_tiles, N_RES, BLOCK, WIDTH_B),
      row(out_norm_b[0]), w_out0[:WIDTH_A], w_out0[WIDTH_A:], row(ffn_norm_g[0]),
      w_gate[0].astype(BF16), w_up[0].astype(BF16), w_down[0].astype(BF16),
      row(ple_norm_g[0]), w_ple_gate[0].astype(BF16), w_ple_proj[0].astype(BF16),
      row(final_norm_g))
    return out.reshape(b, s, D_MODEL)
```

```python
import jax
import jax.numpy as jnp
from jax import lax
from jax.experimental import pallas as pl
from jax.experimental.pallas import tpu as pltpu

D_MODEL = 1024
HEAD_DIM = 64
N_HEADS_A = 4
WIDTH_A = 256
WIDTH_B = 768
CHUNK = 128
BLOCK = 128
DILATIONS = (1, 4, 16)
N_RES = 16
TILE = N_RES * BLOCK
D_FF = 2816
PLE_DIM = 256
EPS = 1e-6
ROPE_THETA = 10000.0
LANES = 128
SUBLANES = 8

ROWS = 512
M_PER_STEP = ROWS // N_RES
FF_CHUNK = 256
ROW_GROUPS = 2
PAIRS_PER_STEP = 2
PIECE = 32
SCORE_LEAD = 2
SOFTMAX_LEAD = 1
N_SLOTS = max(SCORE_LEAD - SOFTMAX_LEAD, SOFTMAX_LEAD) + 1
LOG2_E = 1.4426950408889634

F32 = jnp.float32
BF16 = jnp.bfloat16


def _rms(x, g):
    return x * lax.rsqrt(jnp.mean(x * x, axis=-1, keepdims=True) + EPS) * g


def _dot(a, b):
    return jnp.dot(a, b, preferred_element_type=F32)


def _resident(shape):
    return pl.BlockSpec(shape, lambda *_: (0,) * len(shape), pipeline_mode=pl.Buffered(1))


def _to_residue_major(t, c, slab_ref, out_ref):
    halves = t.reshape(M_PER_STEP, 2, SUBLANES, LANES)
    for a in range(2):
        slab_ref[a, c] = halves[:, a].reshape(M_PER_STEP * SUBLANES, LANES)
    for r in range(N_RES):
        out_ref[r, :, c * LANES:(c + 1) * LANES] = (
            slab_ref[r // SUBLANES, c, pl.ds(r % SUBLANES, M_PER_STEP, stride=SUBLANES), :])


def _pre_kernel(x_ref, g_ref, inv_ref, w_ref, sw_ref, sb_ref, sg_ref, na_ref,
                ya_ref, q_ref, k_ref, v_ref, cu_ref, su_ref, qs_ref, ks_ref, vs_ref):
    i = pl.program_id(1)
    lane = lax.broadcasted_iota(jnp.int32, (1, LANES), 1)
    sign = jnp.where((lane % HEAD_DIM) < (HEAD_DIM // 2), -1.0, 1.0)

    @pl.when((pl.program_id(0) == 0) & (i == 0))
    def _():
        u = lax.broadcasted_iota(jnp.int32, (ROWS, LANES), 0).astype(F32)
        ang = u * inv_ref[...]
        cu_ref[...] = jnp.cos(ang)
        su_ref[...] = jnp.sin(ang) * sign

    ang0 = (i * ROWS).astype(F32) * inv_ref[...]
    ct = jnp.cos(ang0)
    st = jnp.sin(ang0) * sign
    cos = ct * cu_ref[...] - st * su_ref[...]
    sin = st * cu_ref[...] + ct * su_ref[...]
    first_half = (lax.broadcasted_iota(jnp.int32, (ROWS, LANES), 1) % HEAD_DIM) < (HEAD_DIM // 2)

    hn = _rms(x_ref[...], g_ref[...]).astype(BF16)

    def project(col0, slab_ref, out_ref, rope, scale):
        t = _dot(hn, w_ref[:, col0:col0 + WIDTH_B])
        for c in range(WIDTH_B // LANES):
            tc = t[:, c * LANES:(c + 1) * LANES]
            if rope:
                rot = jnp.where(first_half,
                                pltpu.roll(tc, LANES - HEAD_DIM // 2, 1),
                                pltpu.roll(tc, HEAD_DIM // 2, 1))
                tc = tc * cos + rot * sin
            if scale != 1.0:
                tc = tc * scale
            _to_residue_major(tc, c, slab_ref, out_ref)

    uv = _dot(hn, w_ref[:, :2 * WIDTH_A])
    u = jax.nn.gelu(uv[:, :WIDTH_A])
    v = jax.nn.gelu(uv[:, WIDTH_A:])
    mu = jnp.mean(v, axis=-1, keepdims=True)
    d = v - mu
    var = jnp.mean(d * d, axis=-1, keepdims=True)
    vf = d * lax.rsqrt(var + EPS) * sg_ref[...]

    project(2 * WIDTH_A, qs_ref, q_ref, True, HEAD_DIM ** -0.5 * LOG2_E)
    project(2 * WIDTH_A + WIDTH_B, ks_ref, k_ref, True, 1.0)

    row = lax.broadcasted_iota(jnp.int32, (CHUNK, CHUNK), 0)
    col = lax.broadcasted_iota(jnp.int32, (CHUNK, CHUNK), 1)
    w4 = jnp.concatenate(
        [jnp.where(col <= row, sw_ref[h], 0.0) for h in range(N_HEADS_A)], axis=0).astype(BF16)
    lane_a = lax.broadcasted_iota(jnp.int32, (CHUNK, WIDTH_A), 1)
    for c in range(ROWS // CHUNK):
        rows = slice(c * CHUNK, (c + 1) * CHUNK)
        mm = _dot(w4, vf[rows].astype(BF16))
        mixed = mm[3 * CHUNK:]
        for h in (2, 1, 0):
            mixed = jnp.where(lane_a < (h + 1) * HEAD_DIM, mm[h * CHUNK:(h + 1) * CHUNK], mixed)
        ya = u[rows] * (mixed + sb_ref[...])
        ya_ref[rows, :] = _rms(ya, na_ref[...]).astype(ya_ref.dtype)

    project(2 * WIDTH_A + 2 * WIDTH_B, vs_ref, v_ref, False, 1.0)


def _branch_chunks(dilation, idx):
    if dilation == 16:
        return [(idx * BLOCK, BLOCK)]
    if dilation == 4:
        r4, jj = idx % 4, idx // 4
        return [((4 * c + r4) * BLOCK + 32 * jj, 32) for c in range(4)]
    return [(r * BLOCK + SUBLANES * idx, SUBLANES) for r in range(N_RES)]


def _prev_block(dilation, idx):
    if dilation == 16:
        return True, idx
    n_per_res = N_RES // dilation
    if dilation == 4:
        r4, jj = idx % 4, idx // 4
        return jj == 0, r4 + 4 * ((jj + n_per_res - 1) % n_per_res)
    return idx == 0, (idx + n_per_res - 1) % n_per_res


def _sub_index(dilation, a):
    if dilation == 16:
        return a
    if dilation == 4:
        return 4 * (a % 32) + a // 32
    return N_RES * (a % SUBLANES) + a // SUBLANES


def _load_rows(ref, chunks, lanes):
    parts = [ref[s:s + n, lanes] for s, n in chunks]
    return parts[0] if len(parts) == 1 else jnp.concatenate(parts, axis=0)


def _store_rows(ref, chunks, lanes, val):
    off = 0
    for s, n in chunks:
        ref[s:s + n, lanes] = val[off:off + n]
        off += n


def _row_pieces(chunks, piece):
    flat = []
    for s, n in chunks:
        step = min(n, piece)
        flat += [(s + o, step) for o in range(0, n, step)]
    per_group = piece // flat[0][1]
    return [flat[g:g + per_group] for g in range(0, len(flat), per_group)]


def _attn_kernel(q_ref, k_ref, v_ref, o_ref, kp_ref, vp_ref, acc_ref, m_ref, l_ref, bias_ref,
                 s_ref, p_ref, mb_ref):
    j = pl.program_id(2)

    @pl.when(j == 0)
    def _():
        kp_ref[...] = jnp.zeros(kp_ref.shape, F32)
        vp_ref[...] = jnp.zeros(vp_ref.shape, F32)

    @pl.when((pl.program_id(0) == 0) & (pl.program_id(1) == 0) & (j == 0))
    def _():
        qa = lax.broadcasted_iota(jnp.int32, (BLOCK, 2 * BLOCK), 0)
        kc = lax.broadcasted_iota(jnp.int32, (BLOCK, 2 * BLOCK), 1)
        for bi, dilation in enumerate(DILATIONS):
            dist = (BLOCK + _sub_index(dilation, qa)
                    - _sub_index(dilation, kc % BLOCK) - BLOCK * (kc // BLOCK))
            band = (dist >= 0) & (dist <= BLOCK)
            bias_ref[2 * bi] = jnp.where(band, 0.0, -jnp.inf)
            bias_ref[2 * bi + 1] = jnp.where(band & (kc >= BLOCK), 0.0, -jnp.inf)

    n_blk = TILE // BLOCK
    n_units = len(DILATIONS) * n_blk
    pair_lanes = [slice(p * LANES, (p + 1) * LANES) for p in range(PAIRS_PER_STEP)]
    n_pieces = BLOCK // PIECE

    def key_block(bi, idx, cur_ref, prev_ref, lanes):
        dilation = DILATIONS[bi]
        in_prev_tile, pidx = _prev_block(dilation, idx)
        return jnp.concatenate(
            [_load_rows(prev_ref if in_prev_tile else cur_ref, _branch_chunks(dilation, pidx), lanes),
             _load_rows(cur_ref, _branch_chunks(dilation, idx), lanes)], axis=0)

    def scores(bi, idx, slot):
        q_chunks = _branch_chunks(DILATIONS[bi], idx)
        low = lax.broadcasted_iota(jnp.int32, (BLOCK, LANES), 1) < HEAD_DIM
        for p, lanes in enumerate(pair_lanes):
            q = _load_rows(q_ref, q_chunks, lanes)
            lhs = jnp.concatenate([jnp.where(low, q, 0.0), jnp.where(low, 0.0, q)],
                                  axis=0).astype(BF16)
            kb = key_block(bi, idx, k_ref, kp_ref, lanes).astype(BF16)
            for h in range(2):
                s_ref[slot, p, h * BLOCK:(h + 1) * BLOCK, :] = lax.dot_general(
                    lhs[h * BLOCK:(h + 1) * BLOCK], kb, (((1,), (1,)), ((), ())),
                    preferred_element_type=F32)

    def softmax(bi, idx, slot):
        dilation = DILATIONS[bi]
        if dilation == 16:
            first = j == 0
        elif dilation == 4:
            first = (j == 0) & (idx // 4 == 0)
        else:
            first = (j == 0) & (idx == 0)
        table = 2 * bi + first.astype(jnp.int32)
        low = lax.broadcasted_iota(jnp.int32, (PIECE, LANES), 1) < HEAD_DIM
        for p in range(PAIRS_PER_STEP):
            for c in range(n_pieces):
                rows = slice(c * PIECE, (c + 1) * PIECE)
                bias = bias_ref[table, rows, :]
                ms = []
                for h in range(2):
                    head_rows = slice(h * BLOCK + c * PIECE, h * BLOCK + (c + 1) * PIECE)
                    sh = s_ref[slot, p, head_rows, :] + bias
                    mh = jnp.max(sh, axis=-1, keepdims=True)
                    ms.append(mh)
                    p_ref[slot, p, head_rows, :] = jnp.exp2(sh - mh)
                mb_ref[slot, p, rows, :] = jnp.where(low, ms[0], ms[1])

    def combine(bi, idx, slot):
        row_groups = _row_pieces(_branch_chunks(DILATIONS[bi], idx), PIECE)
        low = lax.broadcasted_iota(jnp.int32, (PIECE, 2 * LANES), 1) % LANES < HEAD_DIM
        for p, lanes in enumerate(pair_lanes):
            vb = key_block(bi, idx, v_ref, vp_ref, lanes)
            vbx = jnp.concatenate([vb, jnp.ones_like(vb)], axis=1)
            pv = jnp.concatenate([_dot(p_ref[slot, p, h * BLOCK:(h + 1) * BLOCK, :], vbx) for h in range(2)],
                                 axis=0)
            for c, chunks in enumerate(row_groups):
                rows = slice(c * PIECE, (c + 1) * PIECE)
                both = jnp.where(low, pv[c * PIECE:(c + 1) * PIECE],
                                 pv[BLOCK + c * PIECE:BLOCK + (c + 1) * PIECE])
                out, l_b = both[:, :LANES], both[:, LANES:]
                m_b = mb_ref[slot, p, rows, :]
                if bi == 0:
                    _store_rows(acc_ref, chunks, lanes, out)
                    _store_rows(m_ref, chunks, lanes, m_b)
                    _store_rows(l_ref, chunks, lanes, l_b)
                    continue
                m_o = _load_rows(m_ref, chunks, lanes)
                m_n = jnp.maximum(m_o, m_b)
                a_o = jnp.exp2(m_o - m_n)
                a_b = jnp.exp2(m_b - m_n)
                acc = a_o * _load_rows(acc_ref, chunks, lanes) + a_b * out
                l_n = a_o * _load_rows(l_ref, chunks, lanes) + a_b * l_b
                if bi == len(DILATIONS) - 1:
                    _store_rows(o_ref, chunks, lanes, acc / l_n)
                else:
                    _store_rows(acc_ref, chunks, lanes, acc)
                    _store_rows(m_ref, chunks, lanes, m_n)
                    _store_rows(l_ref, chunks, lanes, l_n)

    unit = lambda g: (g // n_blk, g % n_blk)
    for g in range(-SCORE_LEAD, n_units):
        if g >= 0:
            combine(*unit(g), g % N_SLOTS)
        if 0 <= g + SCORE_LEAD < n_units:
            scores(*unit(g + SCORE_LEAD), (g + SCORE_LEAD) % N_SLOTS)
        if 0 <= g + SOFTMAX_LEAD < n_units:
            softmax(*unit(g + SOFTMAX_LEAD), (g + SOFTMAX_LEAD) % N_SLOTS)

    kp_ref[...] = k_ref[...]
    vp_ref[...] = v_ref[...]


def _post_kernel(x_ref, p_ref, ya_ref, yb_ref, nb_ref, wo_ref, gf_ref, wg_ref, wu_ref,
                 wd_ref, gp_ref, wpg_ref, wpp_ref, gz_ref, o_ref, slab_ref, act_ref):
    for r in range(N_RES):
        for c in range(WIDTH_B // LANES):
            slab_ref[c, pl.ds(r, M_PER_STEP, stride=N_RES), :] = yb_ref[r, :, c * LANES:(c + 1) * LANES]
    groups = [slice(g * (ROWS // ROW_GROUPS), (g + 1) * (ROWS // ROW_GROUPS)) for g in range(ROW_GROUPS)]
    pe = [_dot(p_ref[rows, :].astype(BF16), wpp_ref[...]) for rows in groups]
    ha = [x_ref[rows, :] + _dot(ya_ref[rows, :], wo_ref[:WIDTH_A, :]) for rows in groups]
    ybn = [_rms(jnp.concatenate([slab_ref[c, rows, :] for c in range(WIDTH_B // LANES)], axis=1),
                nb_ref[...]).astype(BF16) for rows in groups]
    h = [hg + _dot(y, wo_ref[WIDTH_A:, :]) for hg, y in zip(ha, ybn)]

    hn = [_rms(hg, gf_ref[...]).astype(BF16) for hg in h]
    for rows, hg in zip(groups, hn):
        for c in range(D_FF // FF_CHUNK):
            cols = slice(c * FF_CHUNK, (c + 1) * FF_CHUNK)
            gate = _dot(hg, wg_ref[:, cols])
            act_ref[rows, cols] = (jax.nn.silu(gate) * _dot(hg, wu_ref[:, cols])).astype(BF16)
    h = [hg + _dot(act_ref[rows, :], wd_ref[...]) for rows, hg in zip(groups, h)]

    gate = [jax.nn.sigmoid(_dot(_rms(hg, gp_ref[...]).astype(BF16), wpg_ref[...])) for hg in h]
    for rows, hg, gg, pg in zip(groups, h, gate, pe):
        o_ref[rows, :] = _rms(hg + gg * pg, gz_ref[...])


def _params(vmem_mib, n_axes):
    return pltpu.CompilerParams(dimension_semantics=("arbitrary",) * n_axes,
                                vmem_limit_bytes=vmem_mib * 1024 * 1024)


def kernel(x, p, mix_norm_g, w_in, sgu_w, sgu_b, sgu_norm_g, out_norm_a, out_norm_b, w_out,
           ffn_norm_g, w_gate, w_up, w_down, ple_norm_g, w_ple_gate, w_ple_proj, final_norm_g):
    b, s, d = x.shape
    assert d == D_MODEL and s % TILE == 0 and w_in.shape[0] == 1
    n_tiles = s // TILE
    n_tok = b * s
    steps_per_seq = s // ROWS
    steps_per_tile = TILE // ROWS
    row = lambda a: a.reshape(1, -1)

    x2 = x.reshape(n_tok, D_MODEL)
    tok_spec = lambda width: pl.BlockSpec((ROWS, width), lambda i: (i, 0))

    half = HEAD_DIM // 2
    inv = ROPE_THETA ** (-(jnp.arange(LANES) % half).astype(F32) / half)
    sgu_bias = jnp.repeat(sgu_b[0].T, HEAD_DIM, axis=1)
    rm_shape = jax.ShapeDtypeStruct((b, n_tiles, N_RES, BLOCK, WIDTH_B), F32)
    rm_spec = pl.BlockSpec((None, None, N_RES, M_PER_STEP, WIDTH_B),
                           lambda bi, i: (bi, i // steps_per_tile, 0, i % steps_per_tile, 0))
    seq_spec = lambda width: pl.BlockSpec((ROWS, width), lambda bi, i: (bi * steps_per_seq + i, 0))
    slab = pltpu.VMEM((2, WIDTH_B // LANES, ROWS // 2, LANES), F32)
    n_in = w_in.shape[2]
    y_a, q, k, v = pl.pallas_call(
        _pre_kernel,
        grid=(b, steps_per_seq),
        in_specs=[seq_spec(D_MODEL), _resident((1, D_MODEL)), _resident((1, LANES)),
                  _resident((D_MODEL, n_in)), _resident((N_HEADS_A, CHUNK, CHUNK)),
                  _resident((CHUNK, WIDTH_A)), _resident((1, WIDTH_A)), _resident((1, WIDTH_A))],
        out_specs=[seq_spec(WIDTH_A), rm_spec, rm_spec, rm_spec],
        out_shape=[jax.ShapeDtypeStruct((n_tok, WIDTH_A), BF16), rm_shape, rm_shape, rm_shape],
        scratch_shapes=[pltpu.VMEM((ROWS, LANES), F32), pltpu.VMEM((ROWS, LANES), F32), slab, slab, slab],
        compiler_params=_params(48, 2),
    )(x2, row(mix_norm_g[0]), row(inv), w_in.reshape(D_MODEL, n_in), sgu_w[0], sgu_bias,
      row(sgu_norm_g[0]), row(out_norm_a[0]))

    step_lanes = PAIRS_PER_STEP * LANES
    tile_shape = (b, n_tiles, TILE, WIDTH_B)
    att_spec = pl.BlockSpec((None, None, TILE, step_lanes), lambda bi, ci, ji: (bi, ji, 0, ci))
    y_b = pl.pallas_call(
        _attn_kernel,
        grid=(b, WIDTH_B // step_lanes, n_tiles),
        in_specs=[att_spec, att_spec, att_spec],
        out_specs=att_spec,
        out_shape=jax.ShapeDtypeStruct(tile_shape, F32),
        scratch_shapes=[pltpu.VMEM((TILE, step_lanes), F32), pltpu.VMEM((TILE, step_lanes), F32),
                        pltpu.VMEM((TILE, step_lanes), F32), pltpu.VMEM((TILE, step_lanes), F32),
                        pltpu.VMEM((TILE, step_lanes), F32),
                        pltpu.VMEM((2 * len(DILATIONS), BLOCK, 2 * BLOCK), F32),
                        pltpu.VMEM((N_SLOTS, PAIRS_PER_STEP, 2 * BLOCK, 2 * BLOCK), F32),
                        pltpu.VMEM((N_SLOTS, PAIRS_PER_STEP, 2 * BLOCK, 2 * BLOCK), F32),
                        pltpu.VMEM((N_SLOTS, PAIRS_PER_STEP, BLOCK, LANES), F32)],
        compiler_params=_params(48, 3),
    )(q.reshape(tile_shape), k.reshape(tile_shape), v.reshape(tile_shape))

    out = pl.pallas_call(
        _post_kernel,
        grid=(n_tok // ROWS,),
        in_specs=[
            tok_spec(D_MODEL), tok_spec(PLE_DIM), tok_spec(WIDTH_A),
            pl.BlockSpec((None, None, N_RES, M_PER_STEP, WIDTH_B),
                         lambda i: (i // steps_per_seq, (i % steps_per_seq) // steps_per_tile, 0,
                                    i % steps_per_tile, 0)),
            _resident((1, WIDTH_B)),
            _resident((D_MODEL, D_MODEL)),
            _resident((1, D_MODEL)),
            _resident((D_MODEL, D_FF)), _resident((D_MODEL, D_FF)), _resident((D_FF, D_MODEL)),
            _resident((1, D_MODEL)),
            _resident((D_MODEL, D_MODEL)), _resident((PLE_DIM, D_MODEL)),
            _resident((1, D_MODEL)),
        ],
        out_specs=tok_spec(D_MODEL),
        out_shape=jax.ShapeDtypeStruct((n_tok, D_MODEL), x.dtype),
        scratch_shapes=[pltpu.VMEM((WIDTH_B // LANES, ROWS, LANES), F32), pltpu.VMEM((ROWS, D_FF), BF16)],
        compiler_params=_params(56, 1),
    )(x2, p.reshape(n_tok, PLE_DIM), y_a, y_b.reshape(b, n_tiles, N_RES, BLOCK, WIDTH_B),
      row(out_norm_b[0]), w_out.reshape(D_MODEL, D_MODEL), row(ffn_norm_g[0]),
      w_gate[0].astype(BF16), w_up[0].astype(BF16), w_down[0].astype(BF16),
      row(ple_norm_g[0]), w_ple_gate.reshape(D_MODEL, D_MODEL), w_ple_proj.reshape(PLE_DIM, D_MODEL),
      row(final_norm_g))
    return out.reshape(b, s, D_MODEL)
```

```python
import jax
import jax.numpy as jnp
from jax import lax
from jax.experimental import pallas as pl
from jax.experimental.pallas import tpu as pltpu

D_MODEL = 1024
HEAD_DIM = 64
N_HEADS_A = 4
WIDTH_A = 256
WIDTH_B = 768
CHUNK = 128
BLOCK = 128
DILATIONS = (1, 4, 16)
N_RES = 16
TILE = N_RES * BLOCK
D_FF = 2816
PLE_DIM = 256
EPS = 1e-6
ROPE_THETA = 10000.0
LANES = 128
SUBLANES = 8

ROWS = 512
M_PER_STEP = ROWS // N_RES
FF_CHUNK = 256
ROW_GROUPS = 2
PAIRS_PER_STEP = 2
PIECE = 32
SCORE_LEAD = 2
SOFTMAX_LEAD = 1
N_SLOTS = max(SCORE_LEAD - SOFTMAX_LEAD, SOFTMAX_LEAD) + 1
LOG2_E = 1.4426950408889634

F32 = jnp.float32
BF16 = jnp.bfloat16


def _rms(x, g):
    return x * lax.rsqrt(jnp.mean(x * x, axis=-1, keepdims=True) + EPS) * g


def _dot(a, b):
    return jnp.dot(a, b, preferred_element_type=F32)


def _resident(shape):
    return pl.BlockSpec(shape, lambda *_: (0,) * len(shape), pipeline_mode=pl.Buffered(1))


def _to_residue_major(t, c, slab_ref, out_ref):
    halves = t.reshape(M_PER_STEP, 2, SUBLANES, LANES)
    for a in range(2):
        slab_ref[a, c] = halves[:, a].reshape(M_PER_STEP * SUBLANES, LANES)
    for r in range(N_RES):
        out_ref[r, :, c * LANES:(c + 1) * LANES] = (
            slab_ref[r // SUBLANES, c, pl.ds(r % SUBLANES, M_PER_STEP, stride=SUBLANES), :])


def _pre_kernel(x_ref, g_ref, inv_ref, w_ref, sw_ref, sb_ref, sg_ref, na_ref, wg_ref, wu_ref, wd_ref,
                ya_ref, q_ref, k_ref, v_ref, wg_out, wu_out, wd_out, cu_ref, su_ref, qs_ref, ks_ref, vs_ref):
    i = pl.program_id(1)
    wg_out[...] = wg_ref[...].astype(BF16)
    wu_out[...] = wu_ref[...].astype(BF16)
    wd_out[...] = wd_ref[...].astype(BF16)
    lane = lax.broadcasted_iota(jnp.int32, (1, LANES), 1)
    sign = jnp.where((lane % HEAD_DIM) < (HEAD_DIM // 2), -1.0, 1.0)

    @pl.when((pl.program_id(0) == 0) & (i == 0))
    def _():
        u = lax.broadcasted_iota(jnp.int32, (ROWS, LANES), 0).astype(F32)
        ang = u * inv_ref[...]
        cu_ref[...] = jnp.cos(ang)
        su_ref[...] = jnp.sin(ang) * sign

    ang0 = (i * ROWS).astype(F32) * inv_ref[...]
    ct = jnp.cos(ang0)
    st = jnp.sin(ang0) * sign
    cos = ct * cu_ref[...] - st * su_ref[...]
    sin = st * cu_ref[...] + ct * su_ref[...]
    first_half = (lax.broadcasted_iota(jnp.int32, (ROWS, LANES), 1) % HEAD_DIM) < (HEAD_DIM // 2)

    hn = _rms(x_ref[...], g_ref[...]).astype(BF16)

    def project(col0, slab_ref, out_ref, rope, scale):
        t = _dot(hn, w_ref[:, col0:col0 + WIDTH_B])
        for c in range(WIDTH_B // LANES):
            tc = t[:, c * LANES:(c + 1) * LANES]
            if rope:
                rot = jnp.where(first_half,
                                pltpu.roll(tc, LANES - HEAD_DIM // 2, 1),
                                pltpu.roll(tc, HEAD_DIM // 2, 1))
                tc = tc * cos + rot * sin
            if scale != 1.0:
                tc = tc * scale
            _to_residue_major(tc, c, slab_ref, out_ref)

    uv = _dot(hn, w_ref[:, :2 * WIDTH_A])
    u = jax.nn.gelu(uv[:, :WIDTH_A])
    v = jax.nn.gelu(uv[:, WIDTH_A:])
    mu = jnp.mean(v, axis=-1, keepdims=True)
    d = v - mu
    var = jnp.mean(d * d, axis=-1, keepdims=True)
    vf = d * lax.rsqrt(var + EPS) * sg_ref[...]

    project(2 * WIDTH_A, qs_ref, q_ref, True, HEAD_DIM ** -0.5 * LOG2_E)
    project(2 * WIDTH_A + WIDTH_B, ks_ref, k_ref, True, 1.0)

    row = lax.broadcasted_iota(jnp.int32, (CHUNK, CHUNK), 0)
    col = lax.broadcasted_iota(jnp.int32, (CHUNK, CHUNK), 1)
    w4 = jnp.concatenate(
        [jnp.where(col <= row, sw_ref[h], 0.0) for h in range(N_HEADS_A)], axis=0).astype(BF16)
    lane_a = lax.broadcasted_iota(jnp.int32, (CHUNK, WIDTH_A), 1)
    for c in range(ROWS // CHUNK):
        rows = slice(c * CHUNK, (c + 1) * CHUNK)
        mm = _dot(w4, vf[rows].astype(BF16))
        mixed = mm[3 * CHUNK:]
        for h in (2, 1, 0):
            mixed = jnp.where(lane_a < (h + 1) * HEAD_DIM, mm[h * CHUNK:(h + 1) * CHUNK], mixed)
        ya = u[rows] * (mixed + sb_ref[...])
        ya_ref[rows, :] = _rms(ya, na_ref[...]).astype(ya_ref.dtype)

    project(2 * WIDTH_A + 2 * WIDTH_B, vs_ref, v_ref, False, 1.0)


def _branch_chunks(dilation, idx):
    if dilation == 16:
        return [(idx * BLOCK, BLOCK)]
    if dilation == 4:
        r4, jj = idx % 4, idx // 4
        return [((4 * c + r4) * BLOCK + 32 * jj, 32) for c in range(4)]
    return [(r * BLOCK + SUBLANES * idx, SUBLANES) for r in range(N_RES)]


def _prev_block(dilation, idx):
    if dilation == 16:
        return True, idx
    n_per_res = N_RES // dilation
    if dilation == 4:
        r4, jj = idx % 4, idx // 4
        return jj == 0, r4 + 4 * ((jj + n_per_res - 1) % n_per_res)
    return idx == 0, (idx + n_per_res - 1) % n_per_res


def _sub_index(dilation, a):
    if dilation == 16:
        return a
    if dilation == 4:
        return 4 * (a % 32) + a // 32
    return N_RES * (a % SUBLANES) + a // SUBLANES


def _load_rows(ref, chunks, lanes):
    parts = [ref[s:s + n, lanes] for s, n in chunks]
    return parts[0] if len(parts) == 1 else jnp.concatenate(parts, axis=0)


def _store_rows(ref, chunks, lanes, val):
    off = 0
    for s, n in chunks:
        ref[s:s + n, lanes] = val[off:off + n]
        off += n


def _row_pieces(chunks, piece):
    flat = []
    for s, n in chunks:
        step = min(n, piece)
        flat += [(s + o, step) for o in range(0, n, step)]
    per_group = piece // flat[0][1]
    return [flat[g:g + per_group] for g in range(0, len(flat), per_group)]


def _attn_kernel(q_ref, k_ref, v_ref, o_ref, kp_ref, vp_ref, acc_ref, m_ref, l_ref, bias_ref,
                 s_ref, p_ref, mb_ref):
    j = pl.program_id(2)

    @pl.when(j == 0)
    def _():
        kp_ref[...] = jnp.zeros(kp_ref.shape, F32)
        vp_ref[...] = jnp.zeros(vp_ref.shape, F32)

    @pl.when((pl.program_id(0) == 0) & (pl.program_id(1) == 0) & (j == 0))
    def _():
        qa = lax.broadcasted_iota(jnp.int32, (BLOCK, 2 * BLOCK), 0)
        kc = lax.broadcasted_iota(jnp.int32, (BLOCK, 2 * BLOCK), 1)
        for bi, dilation in enumerate(DILATIONS):
            dist = (BLOCK + _sub_index(dilation, qa)
                    - _sub_index(dilation, kc % BLOCK) - BLOCK * (kc // BLOCK))
            band = (dist >= 0) & (dist <= BLOCK)
            bias_ref[2 * bi] = jnp.where(band, 0.0, -jnp.inf)
            bias_ref[2 * bi + 1] = jnp.where(band & (kc >= BLOCK), 0.0, -jnp.inf)

    n_blk = TILE // BLOCK
    n_units = len(DILATIONS) * n_blk
    pair_lanes = [slice(p * LANES, (p + 1) * LANES) for p in range(PAIRS_PER_STEP)]
    n_pieces = BLOCK // PIECE

    def key_block(bi, idx, cur_ref, prev_ref, lanes):
        dilation = DILATIONS[bi]
        in_prev_tile, pidx = _prev_block(dilation, idx)
        return jnp.concatenate(
            [_load_rows(prev_ref if in_prev_tile else cur_ref, _branch_chunks(dilation, pidx), lanes),
             _load_rows(cur_ref, _branch_chunks(dilation, idx), lanes)], axis=0)

    def scores(bi, idx, slot):
        q_chunks = _branch_chunks(DILATIONS[bi], idx)
        low = lax.broadcasted_iota(jnp.int32, (BLOCK, LANES), 1) < HEAD_DIM
        for p, lanes in enumerate(pair_lanes):
            q = _load_rows(q_ref, q_chunks, lanes)
            lhs = jnp.concatenate([jnp.where(low, q, 0.0), jnp.where(low, 0.0, q)],
                                  axis=0).astype(BF16)
            kb = key_block(bi, idx, k_ref, kp_ref, lanes).astype(BF16)
            for h in range(2):
                s_ref[slot, p, h * BLOCK:(h + 1) * BLOCK, :] = lax.dot_general(
                    lhs[h * BLOCK:(h + 1) * BLOCK], kb, (((1,), (1,)), ((), ())),
                    preferred_element_type=F32)

    def softmax(bi, idx, slot):
        dilation = DILATIONS[bi]
        if dilation == 16:
            first = j == 0
        elif dilation == 4:
            first = (j == 0) & (idx // 4 == 0)
        else:
            first = (j == 0) & (idx == 0)
        table = 2 * bi + first.astype(jnp.int32)
        low = lax.broadcasted_iota(jnp.int32, (PIECE, LANES), 1) < HEAD_DIM
        for p in range(PAIRS_PER_STEP):
            for c in range(n_pieces):
                rows = slice(c * PIECE, (c + 1) * PIECE)
                bias = bias_ref[table, rows, :]
                ms = []
                for h in range(2):
                    head_rows = slice(h * BLOCK + c * PIECE, h * BLOCK + (c + 1) * PIECE)
                    sh = s_ref[slot, p, head_rows, :] + bias
                    mh = jnp.max(sh, axis=-1, keepdims=True)
                    ms.append(mh)
                    p_ref[slot, p, head_rows, :] = jnp.exp2(sh - mh)
                mb_ref[slot, p, rows, :] = jnp.where(low, ms[0], ms[1])

    def combine(bi, idx, slot):
        row_groups = _row_pieces(_branch_chunks(DILATIONS[bi], idx), PIECE)
        low = lax.broadcasted_iota(jnp.int32, (PIECE, 2 * LANES), 1) % LANES < HEAD_DIM
        for p, lanes in enumerate(pair_lanes):
            vb = key_block(bi, idx, v_ref, vp_ref, lanes)
            vbx = jnp.concatenate([vb, jnp.ones_like(vb)], axis=1)
            pv = jnp.concatenate([_dot(p_ref[slot, p, h * BLOCK:(h + 1) * BLOCK, :], vbx) for h in range(2)],
                                 axis=0)
            for c, chunks in enumerate(row_groups):
                rows = slice(c * PIECE, (c + 1) * PIECE)
                both = jnp.where(low, pv[c * PIECE:(c + 1) * PIECE],
                                 pv[BLOCK + c * PIECE:BLOCK + (c + 1) * PIECE])
                out, l_b = both[:, :LANES], both[:, LANES:]
                m_b = mb_ref[slot, p, rows, :]
                if bi == 0:
                    _store_rows(acc_ref, chunks, lanes, out)
                    _store_rows(m_ref, chunks, lanes, m_b)
                    _store_rows(l_ref, chunks, lanes, l_b)
                    continue
                m_o = _load_rows(m_ref, chunks, lanes)
                m_n = jnp.maximum(m_o, m_b)
                a_o = jnp.exp2(m_o - m_n)
                a_b = jnp.exp2(m_b - m_n)
                acc = a_o * _load_rows(acc_ref, chunks, lanes) + a_b * out
                l_n = a_o * _load_rows(l_ref, chunks, lanes) + a_b * l_b
                if bi == len(DILATIONS) - 1:
                    _store_rows(o_ref, chunks, lanes, acc / l_n)
                else:
                    _store_rows(acc_ref, chunks, lanes, acc)
                    _store_rows(m_ref, chunks, lanes, m_n)
                    _store_rows(l_ref, chunks, lanes, l_n)

    unit = lambda g: (g // n_blk, g % n_blk)
    for g in range(-SCORE_LEAD, n_units):
        if g >= 0:
            combine(*unit(g), g % N_SLOTS)
        if 0 <= g + SCORE_LEAD < n_units:
            scores(*unit(g + SCORE_LEAD), (g + SCORE_LEAD) % N_SLOTS)
        if 0 <= g + SOFTMAX_LEAD < n_units:
            softmax(*unit(g + SOFTMAX_LEAD), (g + SOFTMAX_LEAD) % N_SLOTS)

    kp_ref[...] = k_ref[...]
    vp_ref[...] = v_ref[...]


def _post_kernel(x_ref, p_ref, ya_ref, yb_ref, nb_ref, wo_ref, gf_ref, wg_ref, wu_ref,
                 wd_ref, gp_ref, wpg_ref, wpp_ref, gz_ref, o_ref, slab_ref, act_ref):
    for r in range(N_RES):
        for c in range(WIDTH_B // LANES):
            slab_ref[c, pl.ds(r, M_PER_STEP, stride=N_RES), :] = yb_ref[r, :, c * LANES:(c + 1) * LANES]
    groups = [slice(g * (ROWS // ROW_GROUPS), (g + 1) * (ROWS // ROW_GROUPS)) for g in range(ROW_GROUPS)]
    pe = [_dot(p_ref[rows, :].astype(BF16), wpp_ref[...]) for rows in groups]
    ha = [x_ref[rows, :] + _dot(ya_ref[rows, :], wo_ref[:WIDTH_A, :]) for rows in groups]
    ybn = [_rms(jnp.concatenate([slab_ref[c, rows, :] for c in range(WIDTH_B // LANES)], axis=1),
                nb_ref[...]).astype(BF16) for rows in groups]
    h = [hg + _dot(y, wo_ref[WIDTH_A:, :]) for hg, y in zip(ha, ybn)]

    hn = [_rms(hg, gf_ref[...]).astype(BF16) for hg in h]
    for rows, hg in zip(groups, hn):
        for c in range(D_FF // FF_CHUNK):
            cols = slice(c * FF_CHUNK, (c + 1) * FF_CHUNK)
            gate = _dot(hg, wg_ref[:, cols])
            act_ref[rows, cols] = (jax.nn.silu(gate) * _dot(hg, wu_ref[:, cols])).astype(BF16)
    h = [hg + _dot(act_ref[rows, :], wd_ref[...]) for rows, hg in zip(groups, h)]

    gate = [jax.nn.sigmoid(_dot(_rms(hg, gp_ref[...]).astype(BF16), wpg_ref[...])) for hg in h]
    for rows, hg, gg, pg in zip(groups, h, gate, pe):
        o_ref[rows, :] = _rms(hg + gg * pg, gz_ref[...])


def _params(vmem_mib, n_axes):
    return pltpu.CompilerParams(dimension_semantics=("arbitrary",) * n_axes,
                                vmem_limit_bytes=vmem_mib * 1024 * 1024)


def kernel(x, p, mix_norm_g, w_in, sgu_w, sgu_b, sgu_norm_g, out_norm_a, out_norm_b, w_out,
           ffn_norm_g, w_gate, w_up, w_down, ple_norm_g, w_ple_gate, w_ple_proj, final_norm_g):
    b, s, d = x.shape
    assert d == D_MODEL and s % TILE == 0 and w_in.shape[0] == 1
    n_tiles = s // TILE
    n_tok = b * s
    steps_per_seq = s // ROWS
    steps_per_tile = TILE // ROWS
    row = lambda a: a.reshape(1, -1)

    x2 = x.reshape(n_tok, D_MODEL)
    tok_spec = lambda width: pl.BlockSpec((ROWS, width), lambda i: (i, 0))

    half = HEAD_DIM // 2
    inv = ROPE_THETA ** (-(jnp.arange(LANES) % half).astype(F32) / half)
    sgu_bias = jnp.repeat(sgu_b[0].T, HEAD_DIM, axis=1)
    rm_shape = jax.ShapeDtypeStruct((b, n_tiles, N_RES, BLOCK, WIDTH_B), F32)
    rm_spec = pl.BlockSpec((None, None, N_RES, M_PER_STEP, WIDTH_B),
                           lambda bi, i: (bi, i // steps_per_tile, 0, i % steps_per_tile, 0))
    seq_spec = lambda width: pl.BlockSpec((ROWS, width), lambda bi, i: (bi * steps_per_seq + i, 0))
    slab = pltpu.VMEM((2, WIDTH_B // LANES, ROWS // 2, LANES), F32)
    n_in = w_in.shape[2]

    def slab_spec(n_rows, n_cols):
        n_slabs = max(n for n in range(1, b * steps_per_seq + 1)
                      if n_rows % n == 0 and (n_rows // n) % (2 * SUBLANES) == 0)
        return pl.BlockSpec((n_rows // n_slabs, n_cols),
                            lambda bi, i: (jnp.minimum(bi * steps_per_seq + i, n_slabs - 1), 0))

    ffn_specs = [slab_spec(D_MODEL, D_FF), slab_spec(D_MODEL, D_FF), slab_spec(D_FF, D_MODEL)]
    y_a, q, k, v, w_gate_b, w_up_b, w_down_b = pl.pallas_call(
        _pre_kernel,
        grid=(b, steps_per_seq),
        in_specs=[seq_spec(D_MODEL), _resident((1, D_MODEL)), _resident((1, LANES)),
                  _resident((D_MODEL, n_in)), _resident((N_HEADS_A, CHUNK, CHUNK)),
                  _resident((CHUNK, WIDTH_A)), _resident((1, WIDTH_A)), _resident((1, WIDTH_A))] + ffn_specs,
        out_specs=[seq_spec(WIDTH_A), rm_spec, rm_spec, rm_spec] + ffn_specs,
        out_shape=[jax.ShapeDtypeStruct((n_tok, WIDTH_A), BF16), rm_shape, rm_shape, rm_shape,
                   jax.ShapeDtypeStruct((D_MODEL, D_FF), BF16), jax.ShapeDtypeStruct((D_MODEL, D_FF), BF16),
                   jax.ShapeDtypeStruct((D_FF, D_MODEL), BF16)],
        scratch_shapes=[pltpu.VMEM((ROWS, LANES), F32), pltpu.VMEM((ROWS, LANES), F32), slab, slab, slab],
        compiler_params=_params(48, 2),
    )(x2, row(mix_norm_g[0]), row(inv), w_in.reshape(D_MODEL, n_in), sgu_w[0], sgu_bias,
      row(sgu_norm_g[0]), row(out_norm_a[0]),
      w_gate.reshape(D_MODEL, D_FF), w_up.reshape(D_MODEL, D_FF), w_down.reshape(D_FF, D_MODEL))

    step_lanes = PAIRS_PER_STEP * LANES
    tile_shape = (b, n_tiles, TILE, WIDTH_B)
    att_spec = pl.BlockSpec((None, None, TILE, step_lanes), lambda bi, ci, ji: (bi, ji, 0, ci))
    y_b = pl.pallas_call(
        _attn_kernel,
        grid=(b, WIDTH_B // step_lanes, n_tiles),
        in_specs=[att_spec, att_spec, att_spec],
        out_specs=att_spec,
        out_shape=jax.ShapeDtypeStruct(tile_shape, F32),
        scratch_shapes=[pltpu.VMEM((TILE, step_lanes), F32), pltpu.VMEM((TILE, step_lanes), F32),
                        pltpu.VMEM((TILE, step_lanes), F32), pltpu.VMEM((TILE, step_lanes), F32),
                        pltpu.VMEM((TILE, step_lanes), F32),
                        pltpu.VMEM((2 * len(DILATIONS), BLOCK, 2 * BLOCK), F32),
                        pltpu.VMEM((N_SLOTS, PAIRS_PER_STEP, 2 * BLOCK, 2 * BLOCK), F32),
                        pltpu.VMEM((N_SLOTS, PAIRS_PER_STEP, 2 * BLOCK, 2 * BLOCK), F32),
                        pltpu.VMEM((N_SLOTS, PAIRS_PER_STEP, BLOCK, LANES), F32)],
        compiler_params=_params(48, 3),
    )(q.reshape(tile_shape), k.reshape(tile_shape), v.reshape(tile_shape))

    out = pl.pallas_call(
        _post_kernel,
        grid=(n_tok // ROWS,),
        in_specs=[
            tok_spec(D_MODEL), tok_spec(PLE_DIM), tok_spec(WIDTH_A),
            pl.BlockSpec((None, None, N_RES, M_PER_STEP, WIDTH_B),
                         lambda i: (i // steps_per_seq, (i % steps_per_seq) // steps_per_tile, 0,
                                    i % steps_per_tile, 0)),
            _resident((1, WIDTH_B)),
            _resident((D_MODEL, D_MODEL)),
            _resident((1, D_MODEL)),
            _resident((D_MODEL, D_FF)), _resident((D_MODEL, D_FF)), _resident((D_FF, D_MODEL)),
            _resident((1, D_MODEL)),
            _resident((D_MODEL, D_MODEL)), _resident((PLE_DIM, D_MODEL)),
            _resident((1, D_MODEL)),
        ],
        out_specs=tok_spec(D_MODEL),
        out_shape=jax.ShapeDtypeStruct((n_tok, D_MODEL), x.dtype),
        scratch_shapes=[pltpu.VMEM((WIDTH_B // LANES, ROWS, LANES), F32), pltpu.VMEM((ROWS, D_FF), BF16)],
        compiler_params=_params(56, 1),
    )(x2, p.reshape(n_tok, PLE_DIM), y_a, y_b.reshape(b, n_tiles, N_RES, BLOCK, WIDTH_B),
      row(out_norm_b[0]), w_out.reshape(D_MODEL, D_MODEL), row(ffn_norm_g[0]),
      w_gate_b, w_up_b, w_down_b,
      row(ple_norm_g[0]), w_ple_gate.reshape(D_MODEL, D_MODEL), w_ple_proj.reshape(PLE_DIM, D_MODEL),
      row(final_norm_g))
    return out.reshape(b, s, D_MODEL)
```

```python
import jax
import jax.numpy as jnp
from jax import lax
from jax.experimental import pallas as pl
from jax.experimental.pallas import tpu as pltpu

D_MODEL = 1024
HEAD_DIM = 64
N_HEADS_A = 4
WIDTH_A = 256
WIDTH_B = 768
CHUNK = 128
BLOCK = 128
DILATIONS = (1, 4, 16)
N_RES = 16
TILE = N_RES * BLOCK
D_FF = 2816
PLE_DIM = 256
EPS = 1e-6
ROPE_THETA = 10000.0
LANES = 128
SUBLANES = 8

ROWS = 512
M_PER_STEP = ROWS // N_RES
FF_CHUNK = 256
ROW_GROUPS = 2
PAIRS_PER_STEP = 2
PIECE = 32
SCORE_LEAD = 2
SOFTMAX_LEAD = 1
N_SLOTS = max(SCORE_LEAD - SOFTMAX_LEAD, SOFTMAX_LEAD) + 1
LOG2_E = 1.4426950408889634

F32 = jnp.float32
BF16 = jnp.bfloat16


def _rms(x, g):
    return x * lax.rsqrt(jnp.mean(x * x, axis=-1, keepdims=True) + EPS) * g


def _dot(a, b):
    return jnp.dot(a, b, preferred_element_type=F32)


def _resident(shape):
    return pl.BlockSpec(shape, lambda *_: (0,) * len(shape), pipeline_mode=pl.Buffered(1))


def _to_residue_major(t, c, slab_ref, out_ref):
    halves = t.reshape(M_PER_STEP, 2, SUBLANES, LANES)
    for a in range(2):
        slab_ref[a, c] = halves[:, a].reshape(M_PER_STEP * SUBLANES, LANES)
    for r in range(N_RES):
        out_ref[r, :, c * LANES:(c + 1) * LANES] = (
            slab_ref[r // SUBLANES, c, pl.ds(r % SUBLANES, M_PER_STEP, stride=SUBLANES), :])


def _pre_kernel(x_ref, g_ref, inv_ref, w_ref, sw_ref, sb_ref, sg_ref, na_ref, wg_ref, wu_ref, wd_ref,
                ya_ref, q_ref, k_ref, v_ref, wg_out, wu_out, wd_out, cu_ref, su_ref, qs_ref, ks_ref, vs_ref):
    i = pl.program_id(1)
    wg_out[...] = wg_ref[...].astype(BF16)
    wu_out[...] = wu_ref[...].astype(BF16)
    wd_out[...] = wd_ref[...].astype(BF16)
    lane = lax.broadcasted_iota(jnp.int32, (1, LANES), 1)
    sign = jnp.where((lane % HEAD_DIM) < (HEAD_DIM // 2), -1.0, 1.0)

    @pl.when((pl.program_id(0) == 0) & (i == 0))
    def _():
        u = lax.broadcasted_iota(jnp.int32, (ROWS, LANES), 0).astype(F32)
        ang = u * inv_ref[...]
        cu_ref[...] = jnp.cos(ang)
        su_ref[...] = jnp.sin(ang) * sign

    ang0 = (i * ROWS).astype(F32) * inv_ref[...]
    ct = jnp.cos(ang0)
    st = jnp.sin(ang0) * sign
    cos = ct * cu_ref[...] - st * su_ref[...]
    sin = st * cu_ref[...] + ct * su_ref[...]
    first_half = (lax.broadcasted_iota(jnp.int32, (ROWS, LANES), 1) % HEAD_DIM) < (HEAD_DIM // 2)

    hn = _rms(x_ref[...], g_ref[...]).astype(BF16)

    def project(col0, slab_ref, out_ref, rope, scale):
        t = _dot(hn, w_ref[:, col0:col0 + WIDTH_B])
        for c in range(WIDTH_B // LANES):
            tc = t[:, c * LANES:(c + 1) * LANES]
            if rope:
                rot = jnp.where(first_half,
                                pltpu.roll(tc, LANES - HEAD_DIM // 2, 1),
                                pltpu.roll(tc, HEAD_DIM // 2, 1))
                tc = tc * cos + rot * sin
            if scale != 1.0:
                tc = tc * scale
            _to_residue_major(tc, c, slab_ref, out_ref)

    uv = _dot(hn, w_ref[:, :2 * WIDTH_A])
    u = jax.nn.gelu(uv[:, :WIDTH_A])
    v = jax.nn.gelu(uv[:, WIDTH_A:])
    mu = jnp.mean(v, axis=-1, keepdims=True)
    d = v - mu
    var = jnp.mean(d * d, axis=-1, keepdims=True)
    vf = d * lax.rsqrt(var + EPS) * sg_ref[...]

    project(2 * WIDTH_A, qs_ref, q_ref, True, HEAD_DIM ** -0.5 * LOG2_E)
    project(2 * WIDTH_A + WIDTH_B, ks_ref, k_ref, True, 1.0)

    row = lax.broadcasted_iota(jnp.int32, (CHUNK, CHUNK), 0)
    col = lax.broadcasted_iota(jnp.int32, (CHUNK, CHUNK), 1)
    w4 = jnp.concatenate(
        [jnp.where(col <= row, sw_ref[h], 0.0) for h in range(N_HEADS_A)], axis=0).astype(BF16)
    lane_a = lax.broadcasted_iota(jnp.int32, (CHUNK, WIDTH_A), 1)
    for c in range(ROWS // CHUNK):
        rows = slice(c * CHUNK, (c + 1) * CHUNK)
        mm = _dot(w4, vf[rows].astype(BF16))
        mixed = mm[3 * CHUNK:]
        for h in (2, 1, 0):
            mixed = jnp.where(lane_a < (h + 1) * HEAD_DIM, mm[h * CHUNK:(h + 1) * CHUNK], mixed)
        ya = u[rows] * (mixed + sb_ref[...])
        ya_ref[rows, :] = _rms(ya, na_ref[...]).astype(ya_ref.dtype)

    project(2 * WIDTH_A + 2 * WIDTH_B, vs_ref, v_ref, False, 1.0)


def _branch_chunks(dilation, idx):
    if dilation == 16:
        return [(idx * BLOCK, BLOCK)]
    if dilation == 4:
        r4, jj = idx % 4, idx // 4
        return [((4 * c + r4) * BLOCK + 32 * jj, 32) for c in range(4)]
    return [(r * BLOCK + SUBLANES * idx, SUBLANES) for r in range(N_RES)]


def _prev_block(dilation, idx):
    if dilation == 16:
        return True, idx
    n_per_res = N_RES // dilation
    if dilation == 4:
        r4, jj = idx % 4, idx // 4
        return jj == 0, r4 + 4 * ((jj + n_per_res - 1) % n_per_res)
    return idx == 0, (idx + n_per_res - 1) % n_per_res


def _sub_index(dilation, a):
    if dilation == 16:
        return a
    if dilation == 4:
        return 4 * (a % 32) + a // 32
    return N_RES * (a % SUBLANES) + a // SUBLANES


def _load_rows(ref, chunks, lanes):
    parts = [ref[s:s + n, lanes] for s, n in chunks]
    return parts[0] if len(parts) == 1 else jnp.concatenate(parts, axis=0)


def _store_rows(ref, chunks, lanes, val):
    off = 0
    for s, n in chunks:
        ref[s:s + n, lanes] = val[off:off + n]
        off += n


def _row_pieces(chunks, piece):
    flat = []
    for s, n in chunks:
        step = min(n, piece)
        flat += [(s + o, step) for o in range(0, n, step)]
    per_group = piece // flat[0][1]
    return [flat[g:g + per_group] for g in range(0, len(flat), per_group)]


def _attn_kernel(q_ref, k_ref, v_ref, o_ref, kp_ref, vp_ref, acc_ref, m_ref, l_ref, bias_ref,
                 s_ref, p_ref, mb_ref):
    j = pl.program_id(2)

    @pl.when(j == 0)
    def _():
        kp_ref[...] = jnp.zeros(kp_ref.shape, F32)
        vp_ref[...] = jnp.zeros(vp_ref.shape, F32)

    @pl.when((pl.program_id(0) == 0) & (pl.program_id(1) == 0) & (j == 0))
    def _():
        qa = lax.broadcasted_iota(jnp.int32, (BLOCK, 2 * BLOCK), 0)
        kc = lax.broadcasted_iota(jnp.int32, (BLOCK, 2 * BLOCK), 1)
        for bi, dilation in enumerate(DILATIONS):
            dist = (BLOCK + _sub_index(dilation, qa)
                    - _sub_index(dilation, kc % BLOCK) - BLOCK * (kc // BLOCK))
            band = (dist >= 0) & (dist <= BLOCK)
            bias_ref[2 * bi] = jnp.where(band, 0.0, -jnp.inf)
            bias_ref[2 * bi + 1] = jnp.where(band & (kc >= BLOCK), 0.0, -jnp.inf)

    n_blk = TILE // BLOCK
    n_units = len(DILATIONS) * n_blk
    pair_lanes = [slice(p * LANES, (p + 1) * LANES) for p in range(PAIRS_PER_STEP)]
    n_pieces = BLOCK // PIECE

    def key_block(bi, idx, cur_ref, prev_ref, lanes):
        dilation = DILATIONS[bi]
        in_prev_tile, pidx = _prev_block(dilation, idx)
        return jnp.concatenate(
            [_load_rows(prev_ref if in_prev_tile else cur_ref, _branch_chunks(dilation, pidx), lanes),
             _load_rows(cur_ref, _branch_chunks(dilation, idx), lanes)], axis=0)

    def scores(bi, idx, slot):
        q_chunks = _branch_chunks(DILATIONS[bi], idx)
        low = lax.broadcasted_iota(jnp.int32, (BLOCK, LANES), 1) < HEAD_DIM
        for p, lanes in enumerate(pair_lanes):
            q = _load_rows(q_ref, q_chunks, lanes)
            lhs = jnp.concatenate([jnp.where(low, q, 0.0), jnp.where(low, 0.0, q)],
                                  axis=0).astype(BF16)
            kb = key_block(bi, idx, k_ref, kp_ref, lanes).astype(BF16)
            for h in range(2):
                s_ref[slot, p, h * BLOCK:(h + 1) * BLOCK, :] = lax.dot_general(
                    lhs[h * BLOCK:(h + 1) * BLOCK], kb, (((1,), (1,)), ((), ())),
                    preferred_element_type=F32)

    def softmax(bi, idx, slot):
        dilation = DILATIONS[bi]
        if dilation == 16:
            first = j == 0
        elif dilation == 4:
            first = (j == 0) & (idx // 4 == 0)
        else:
            first = (j == 0) & (idx == 0)
        table = 2 * bi + first.astype(jnp.int32)
        low = lax.broadcasted_iota(jnp.int32, (PIECE, LANES), 1) < HEAD_DIM
        for p in range(PAIRS_PER_STEP):
            for c in range(n_pieces):
                rows = slice(c * PIECE, (c + 1) * PIECE)
                bias = bias_ref[table, rows, :]
                ms = []
                for h in range(2):
                    head_rows = slice(h * BLOCK + c * PIECE, h * BLOCK + (c + 1) * PIECE)
                    sh = s_ref[slot, p, head_rows, :] + bias
                    mh = jnp.max(sh, axis=-1, keepdims=True)
                    ms.append(mh)
                    p_ref[slot, p, head_rows, :] = jnp.exp2(sh - mh)
                mb_ref[slot, p, rows, :] = jnp.where(low, ms[0], ms[1])

    def combine(bi, idx, slot):
        row_groups = _row_pieces(_branch_chunks(DILATIONS[bi], idx), PIECE)
        low = lax.broadcasted_iota(jnp.int32, (PIECE, 2 * LANES), 1) % LANES < HEAD_DIM
        for p, lanes in enumerate(pair_lanes):
            vb = key_block(bi, idx, v_ref, vp_ref, lanes)
            vbx = jnp.concatenate([vb, jnp.ones_like(vb)], axis=1)
            pv = jnp.concatenate([_dot(p_ref[slot, p, h * BLOCK:(h + 1) * BLOCK, :], vbx) for h in range(2)],
                                 axis=0)
            for c, chunks in enumerate(row_groups):
                rows = slice(c * PIECE, (c + 1) * PIECE)
                both = jnp.where(low, pv[c * PIECE:(c + 1) * PIECE],
                                 pv[BLOCK + c * PIECE:BLOCK + (c + 1) * PIECE])
                out, l_b = both[:, :LANES], both[:, LANES:]
                m_b = mb_ref[slot, p, rows, :]
                if bi == 0:
                    _store_rows(acc_ref, chunks, lanes, out)
                    _store_rows(m_ref, chunks, lanes, m_b)
                    _store_rows(l_ref, chunks, lanes, l_b)
                    continue
                m_o = _load_rows(m_ref, chunks, lanes)
                m_n = jnp.maximum(m_o, m_b)
                a_o = jnp.exp2(m_o - m_n)
                a_b = jnp.exp2(m_b - m_n)
                acc = a_o * _load_rows(acc_ref, chunks, lanes) + a_b * out
                l_n = a_o * _load_rows(l_ref, chunks, lanes) + a_b * l_b
                if bi == len(DILATIONS) - 1:
                    _store_rows(o_ref, chunks, lanes, acc / l_n)
                else:
                    _store_rows(acc_ref, chunks, lanes, acc)
                    _store_rows(m_ref, chunks, lanes, m_n)
                    _store_rows(l_ref, chunks, lanes, l_n)

    unit = lambda g: (g // n_blk, g % n_blk)
    for g in range(-SCORE_LEAD, n_units):
        if g >= 0:
            combine(*unit(g), g % N_SLOTS)
        if 0 <= g + SCORE_LEAD < n_units:
            scores(*unit(g + SCORE_LEAD), (g + SCORE_LEAD) % N_SLOTS)
        if 0 <= g + SOFTMAX_LEAD < n_units:
            softmax(*unit(g + SOFTMAX_LEAD), (g + SOFTMAX_LEAD) % N_SLOTS)

    kp_ref[...] = k_ref[...]
    vp_ref[...] = v_ref[...]


def _post_kernel(x_ref, p_ref, ya_ref, yb_ref, nb_ref, wo_ref, gf_ref, wg_ref, wu_ref,
                 wd_ref, gp_ref, wpg_ref, wpp_ref, gz_ref, o_ref, slab_ref, act_ref):
    for r in range(N_RES):
        for c in range(WIDTH_B // LANES):
            slab_ref[r // SUBLANES, c, pl.ds(r % SUBLANES, M_PER_STEP, stride=SUBLANES), :] = (
                yb_ref[r, :, c * LANES:(c + 1) * LANES])

    def natural_rows(c, rows):
        half = slice(rows.start // 2, rows.stop // 2)
        tiles = [slab_ref[a, c, half, :].reshape(-1, 1, SUBLANES, LANES) for a in range(2)]
        return jnp.concatenate(tiles, axis=1).reshape(rows.stop - rows.start, LANES)
    groups = [slice(g * (ROWS // ROW_GROUPS), (g + 1) * (ROWS // ROW_GROUPS)) for g in range(ROW_GROUPS)]
    pe = [_dot(p_ref[rows, :].astype(BF16), wpp_ref[...]) for rows in groups]
    ha = [x_ref[rows, :] + _dot(ya_ref[rows, :], wo_ref[:WIDTH_A, :]) for rows in groups]
    ybn = [_rms(jnp.concatenate([natural_rows(c, rows) for c in range(WIDTH_B // LANES)], axis=1),
                nb_ref[...]).astype(BF16) for rows in groups]
    h = [hg + _dot(y, wo_ref[WIDTH_A:, :]) for hg, y in zip(ha, ybn)]

    hn = [_rms(hg, gf_ref[...]).astype(BF16) for hg in h]
    for rows, hg in zip(groups, hn):
        for c in range(D_FF // FF_CHUNK):
            cols = slice(c * FF_CHUNK, (c + 1) * FF_CHUNK)
            gate = _dot(hg, wg_ref[:, cols])
            act_ref[rows, cols] = (jax.nn.silu(gate) * _dot(hg, wu_ref[:, cols])).astype(BF16)
    h = [hg + _dot(act_ref[rows, :], wd_ref[...]) for rows, hg in zip(groups, h)]

    gate = [jax.nn.sigmoid(_dot(_rms(hg, gp_ref[...]).astype(BF16), wpg_ref[...])) for hg in h]
    for rows, hg, gg, pg in zip(groups, h, gate, pe):
        o_ref[rows, :] = _rms(hg + gg * pg, gz_ref[...])


def _params(vmem_mib, n_axes):
    return pltpu.CompilerParams(dimension_semantics=("arbitrary",) * n_axes,
                                vmem_limit_bytes=vmem_mib * 1024 * 1024)


def kernel(x, p, mix_norm_g, w_in, sgu_w, sgu_b, sgu_norm_g, out_norm_a, out_norm_b, w_out,
           ffn_norm_g, w_gate, w_up, w_down, ple_norm_g, w_ple_gate, w_ple_proj, final_norm_g):
    b, s, d = x.shape
    assert d == D_MODEL and s % TILE == 0 and w_in.shape[0] == 1
    n_tiles = s // TILE
    n_tok = b * s
    steps_per_seq = s // ROWS
    steps_per_tile = TILE // ROWS
    row = lambda a: a.reshape(1, -1)

    x2 = x.reshape(n_tok, D_MODEL)
    tok_spec = lambda width: pl.BlockSpec((ROWS, width), lambda i: (i, 0))

    half = HEAD_DIM // 2
    inv = ROPE_THETA ** (-(jnp.arange(LANES) % half).astype(F32) / half)
    sgu_bias = jnp.repeat(sgu_b[0].T, HEAD_DIM, axis=1)
    rm_shape = jax.ShapeDtypeStruct((b, n_tiles, N_RES, BLOCK, WIDTH_B), F32)
    rm_spec = pl.BlockSpec((None, None, N_RES, M_PER_STEP, WIDTH_B),
                           lambda bi, i: (bi, i // steps_per_tile, 0, i % steps_per_tile, 0))
    seq_spec = lambda width: pl.BlockSpec((ROWS, width), lambda bi, i: (bi * steps_per_seq + i, 0))
    slab = pltpu.VMEM((2, WIDTH_B // LANES, ROWS // 2, LANES), F32)
    n_in = w_in.shape[2]

    def slab_spec(n_rows, n_cols):
        n_slabs = max(n for n in range(1, b * steps_per_seq + 1)
                      if n_rows % n == 0 and (n_rows // n) % (2 * SUBLANES) == 0)
        return pl.BlockSpec((n_rows // n_slabs, n_cols),
                            lambda bi, i: (jnp.minimum(bi * steps_per_seq + i, n_slabs - 1), 0))

    ffn_specs = [slab_spec(D_MODEL, D_FF), slab_spec(D_MODEL, D_FF), slab_spec(D_FF, D_MODEL)]
    y_a, q, k, v, w_gate_b, w_up_b, w_down_b = pl.pallas_call(
        _pre_kernel,
        grid=(b, steps_per_seq),
        in_specs=[seq_spec(D_MODEL), _resident((1, D_MODEL)), _resident((1, LANES)),
                  _resident((D_MODEL, n_in)), _resident((N_HEADS_A, CHUNK, CHUNK)),
                  _resident((CHUNK, WIDTH_A)), _resident((1, WIDTH_A)), _resident((1, WIDTH_A))] + ffn_specs,
        out_specs=[seq_spec(WIDTH_A), rm_spec, rm_spec, rm_spec] + ffn_specs,
        out_shape=[jax.ShapeDtypeStruct((n_tok, WIDTH_A), BF16), rm_shape, rm_shape, rm_shape,
                   jax.ShapeDtypeStruct((D_MODEL, D_FF), BF16), jax.ShapeDtypeStruct((D_MODEL, D_FF), BF16),
                   jax.ShapeDtypeStruct((D_FF, D_MODEL), BF16)],
        scratch_shapes=[pltpu.VMEM((ROWS, LANES), F32), pltpu.VMEM((ROWS, LANES), F32), slab, slab, slab],
        compiler_params=_params(48, 2),
    )(x2, row(mix_norm_g[0]), row(inv), w_in.reshape(D_MODEL, n_in), sgu_w[0], sgu_bias,
      row(sgu_norm_g[0]), row(out_norm_a[0]),
      w_gate.reshape(D_MODEL, D_FF), w_up.reshape(D_MODEL, D_FF), w_down.reshape(D_FF, D_MODEL))

    step_lanes = PAIRS_PER_STEP * LANES
    tile_shape = (b, n_tiles, TILE, WIDTH_B)
    att_spec = pl.BlockSpec((None, None, TILE, step_lanes), lambda bi, ci, ji: (bi, ji, 0, ci))
    y_b = pl.pallas_call(
        _attn_kernel,
        grid=(b, WIDTH_B // step_lanes, n_tiles),
        in_specs=[att_spec, att_spec, att_spec],
        out_specs=att_spec,
        out_shape=jax.ShapeDtypeStruct(tile_shape, F32),
        scratch_shapes=[pltpu.VMEM((TILE, step_lanes), F32), pltpu.VMEM((TILE, step_lanes), F32),
                        pltpu.VMEM((TILE, step_lanes), F32), pltpu.VMEM((TILE, step_lanes), F32),
                        pltpu.VMEM((TILE, step_lanes), F32),
                        pltpu.VMEM((2 * len(DILATIONS), BLOCK, 2 * BLOCK), F32),
                        pltpu.VMEM((N_SLOTS, PAIRS_PER_STEP, 2 * BLOCK, 2 * BLOCK), F32),
                        pltpu.VMEM((N_SLOTS, PAIRS_PER_STEP, 2 * BLOCK, 2 * BLOCK), F32),
                        pltpu.VMEM((N_SLOTS, PAIRS_PER_STEP, BLOCK, LANES), F32)],
        compiler_params=_params(48, 3),
    )(q.reshape(tile_shape), k.reshape(tile_shape), v.reshape(tile_shape))

    out = pl.pallas_call(
        _post_kernel,
        grid=(n_tok // ROWS,),
        in_specs=[
            tok_spec(D_MODEL), tok_spec(PLE_DIM), tok_spec(WIDTH_A),
            pl.BlockSpec((None, None, N_RES, M_PER_STEP, WIDTH_B),
                         lambda i: (i // steps_per_seq, (i % steps_per_seq) // steps_per_tile, 0,
                                    i % steps_per_tile, 0)),
            _resident((1, WIDTH_B)),
            _resident((D_MODEL, D_MODEL)),
            _resident((1, D_MODEL)),
            _resident((D_MODEL, D_FF)), _resident((D_MODEL, D_FF)), _resident((D_FF, D_MODEL)),
            _resident((1, D_MODEL)),
            _resident((D_MODEL, D_MODEL)), _resident((PLE_DIM, D_MODEL)),
            _resident((1, D_MODEL)),
        ],
        out_specs=tok_spec(D_MODEL),
        out_shape=jax.ShapeDtypeStruct((n_tok, D_MODEL), x.dtype),
        scratch_shapes=[slab, pltpu.VMEM((ROWS, D_FF), BF16)],
        compiler_params=_params(56, 1),
    )(x2, p.reshape(n_tok, PLE_DIM), y_a, y_b.reshape(b, n_tiles, N_RES, BLOCK, WIDTH_B),
      row(out_norm_b[0]), w_out.reshape(D_MODEL, D_MODEL), row(ffn_norm_g[0]),
      w_gate_b, w_up_b, w_down_b,
      row(ple_norm_g[0]), w_ple_gate.reshape(D_MODEL, D_MODEL), w_ple_proj.reshape(PLE_DIM, D_MODEL),
      row(final_norm_g))
    return out.reshape(b, s, D_MODEL)
```

```python
import jax
import jax.numpy as jnp
from jax import lax
from jax.experimental import pallas as pl
from jax.experimental.pallas import tpu as pltpu

D_MODEL = 1024
HEAD_DIM = 64
N_HEADS_A = 4
WIDTH_A = 256
WIDTH_B = 768
CHUNK = 128
BLOCK = 128
DILATIONS = (1, 4, 16)
N_RES = 16
TILE = N_RES * BLOCK
D_FF = 2816
PLE_DIM = 256
EPS = 1e-6
ROPE_THETA = 10000.0
LANES = 128
SUBLANES = 8

ROWS = 512
M_PER_STEP = ROWS // N_RES
FF_CHUNK = 256
ROW_GROUPS = 2
PAIRS_PER_STEP = 2
PIECE = 32
SCORE_LEAD = 2
SOFTMAX_LEAD = 1
N_SLOTS = max(SCORE_LEAD - SOFTMAX_LEAD, SOFTMAX_LEAD) + 1
LOG2_E = 1.4426950408889634

F32 = jnp.float32
BF16 = jnp.bfloat16


def _rms(x, g):
    return x * lax.rsqrt(jnp.mean(x * x, axis=-1, keepdims=True) + EPS) * g


def _dot(a, b):
    return jnp.dot(a, b, preferred_element_type=F32)


def _resident(shape):
    return pl.BlockSpec(shape, lambda *_: (0,) * len(shape), pipeline_mode=pl.Buffered(1))


def _to_residue_major(t, c, slab_ref, out_ref):
    halves = t.reshape(M_PER_STEP, 2, SUBLANES, LANES)
    for a in range(2):
        slab_ref[a, c] = halves[:, a].reshape(M_PER_STEP * SUBLANES, LANES)
    for r in range(N_RES):
        out_ref[r, :, c * LANES:(c + 1) * LANES] = (
            slab_ref[r // SUBLANES, c, pl.ds(r % SUBLANES, M_PER_STEP, stride=SUBLANES), :])


def _pre_kernel(x_ref, g_ref, inv_ref, w_ref, sw_ref, sb_ref, sg_ref, na_ref, wg_ref, wu_ref, wd_ref,
                ya_ref, q_ref, k_ref, v_ref, wg_out, wu_out, wd_out, cu_ref, su_ref, qs_ref, ks_ref, vs_ref):
    i = pl.program_id(1)
    wg_out[...] = wg_ref[...].astype(BF16)
    wu_out[...] = wu_ref[...].astype(BF16)
    wd_out[...] = wd_ref[...].astype(BF16)
    lane = lax.broadcasted_iota(jnp.int32, (1, LANES), 1)
    sign = jnp.where((lane % HEAD_DIM) < (HEAD_DIM // 2), -1.0, 1.0)

    @pl.when((pl.program_id(0) == 0) & (i == 0))
    def _():
        u = lax.broadcasted_iota(jnp.int32, (ROWS, LANES), 0).astype(F32)
        ang = u * inv_ref[...]
        cu_ref[...] = jnp.cos(ang)
        su_ref[...] = jnp.sin(ang) * sign

    ang0 = (i * ROWS).astype(F32) * inv_ref[...]
    ct = jnp.cos(ang0)
    st = jnp.sin(ang0) * sign
    cos = ct * cu_ref[...] - st * su_ref[...]
    sin = st * cu_ref[...] + ct * su_ref[...]
    first_half = (lax.broadcasted_iota(jnp.int32, (ROWS, LANES), 1) % HEAD_DIM) < (HEAD_DIM // 2)

    hn = _rms(x_ref[...], g_ref[...]).astype(BF16)

    def project(col0, slab_ref, out_ref, rope, scale):
        t = _dot(hn, w_ref[:, col0:col0 + WIDTH_B])
        for c in range(WIDTH_B // LANES):
            tc = t[:, c * LANES:(c + 1) * LANES]
            if rope:
                rot = jnp.where(first_half,
                                pltpu.roll(tc, LANES - HEAD_DIM // 2, 1),
                                pltpu.roll(tc, HEAD_DIM // 2, 1))
                tc = tc * cos + rot * sin
            if scale != 1.0:
                tc = tc * scale
            _to_residue_major(tc, c, slab_ref, out_ref)

    uv = _dot(hn, w_ref[:, :2 * WIDTH_A])
    u = jax.nn.gelu(uv[:, :WIDTH_A])
    v = jax.nn.gelu(uv[:, WIDTH_A:])
    mu = jnp.mean(v, axis=-1, keepdims=True)
    d = v - mu
    var = jnp.mean(d * d, axis=-1, keepdims=True)
    vf = d * lax.rsqrt(var + EPS) * sg_ref[...]

    project(2 * WIDTH_A, qs_ref, q_ref, True, HEAD_DIM ** -0.5 * LOG2_E)
    project(2 * WIDTH_A + WIDTH_B, ks_ref, k_ref, True, 1.0)

    row = lax.broadcasted_iota(jnp.int32, (CHUNK, CHUNK), 0)
    col = lax.broadcasted_iota(jnp.int32, (CHUNK, CHUNK), 1)
    w4 = jnp.concatenate(
        [jnp.where(col <= row, sw_ref[h], 0.0) for h in range(N_HEADS_A)], axis=0).astype(BF16)
    lane_a = lax.broadcasted_iota(jnp.int32, (CHUNK, WIDTH_A), 1)
    for c in range(ROWS // CHUNK):
        rows = slice(c * CHUNK, (c + 1) * CHUNK)
        mm = _dot(w4, vf[rows].astype(BF16))
        mixed = mm[3 * CHUNK:]
        for h in (2, 1, 0):
            mixed = jnp.where(lane_a < (h + 1) * HEAD_DIM, mm[h * CHUNK:(h + 1) * CHUNK], mixed)
        ya = u[rows] * (mixed + sb_ref[...])
        ya_ref[rows, :] = _rms(ya, na_ref[...]).astype(ya_ref.dtype)

    project(2 * WIDTH_A + 2 * WIDTH_B, vs_ref, v_ref, False, 1.0)


def _branch_chunks(dilation, idx):
    if dilation == 16:
        return [(idx * BLOCK, BLOCK)]
    if dilation == 4:
        r4, jj = idx % 4, idx // 4
        return [((4 * c + r4) * BLOCK + 32 * jj, 32) for c in range(4)]
    return [(r * BLOCK + SUBLANES * idx, SUBLANES) for r in range(N_RES)]


def _prev_block(dilation, idx):
    if dilation == 16:
        return True, idx
    n_per_res = N_RES // dilation
    if dilation == 4:
        r4, jj = idx % 4, idx // 4
        return jj == 0, r4 + 4 * ((jj + n_per_res - 1) % n_per_res)
    return idx == 0, (idx + n_per_res - 1) % n_per_res


def _sub_index(dilation, a):
    if dilation == 16:
        return a
    if dilation == 4:
        return 4 * (a % 32) + a // 32
    return N_RES * (a % SUBLANES) + a // SUBLANES


def _load_rows(ref, chunks, lanes):
    parts = [ref[s:s + n, lanes] for s, n in chunks]
    return parts[0] if len(parts) == 1 else jnp.concatenate(parts, axis=0)


def _store_rows(ref, chunks, lanes, val):
    off = 0
    for s, n in chunks:
        ref[s:s + n, lanes] = val[off:off + n]
        off += n


def _row_pieces(chunks, piece):
    flat = []
    for s, n in chunks:
        step = min(n, piece)
        flat += [(s + o, step) for o in range(0, n, step)]
    per_group = piece // flat[0][1]
    return [flat[g:g + per_group] for g in range(0, len(flat), per_group)]


def _attn_kernel(q_ref, k_ref, v_ref, kp_ref, vp_ref, o_ref, acc_ref, m_ref, l_ref, bias_ref,
                 s_ref, p_ref, mb_ref):
    j = pl.program_id(2)

    @pl.when((pl.program_id(0) == 0) & (pl.program_id(1) == 0) & (j == 0))
    def _():
        qa = lax.broadcasted_iota(jnp.int32, (BLOCK, 2 * BLOCK), 0)
        kc = lax.broadcasted_iota(jnp.int32, (BLOCK, 2 * BLOCK), 1)
        for bi, dilation in enumerate(DILATIONS):
            dist = (BLOCK + _sub_index(dilation, qa)
                    - _sub_index(dilation, kc % BLOCK) - BLOCK * (kc // BLOCK))
            band = (dist >= 0) & (dist <= BLOCK)
            bias_ref[2 * bi] = jnp.where(band, 0.0, -jnp.inf)
            bias_ref[2 * bi + 1] = jnp.where(band & (kc >= BLOCK), 0.0, -jnp.inf)

    n_blk = TILE // BLOCK
    n_units = len(DILATIONS) * n_blk
    pair_lanes = [slice(p * LANES, (p + 1) * LANES) for p in range(PAIRS_PER_STEP)]
    n_pieces = BLOCK // PIECE

    def key_block(bi, idx, cur_ref, prev_ref, lanes):
        dilation = DILATIONS[bi]
        in_prev_tile, pidx = _prev_block(dilation, idx)
        return jnp.concatenate(
            [_load_rows(prev_ref if in_prev_tile else cur_ref, _branch_chunks(dilation, pidx), lanes),
             _load_rows(cur_ref, _branch_chunks(dilation, idx), lanes)], axis=0)

    def scores(bi, idx, slot):
        q_chunks = _branch_chunks(DILATIONS[bi], idx)
        low = lax.broadcasted_iota(jnp.int32, (BLOCK, LANES), 1) < HEAD_DIM
        for p, lanes in enumerate(pair_lanes):
            q = _load_rows(q_ref, q_chunks, lanes)
            lhs = jnp.concatenate([jnp.where(low, q, 0.0), jnp.where(low, 0.0, q)],
                                  axis=0).astype(BF16)
            kb = key_block(bi, idx, k_ref, kp_ref, lanes).astype(BF16)
            for h in range(2):
                s_ref[slot, p, h * BLOCK:(h + 1) * BLOCK, :] = lax.dot_general(
                    lhs[h * BLOCK:(h + 1) * BLOCK], kb, (((1,), (1,)), ((), ())),
                    preferred_element_type=F32)

    def softmax(bi, idx, slot):
        dilation = DILATIONS[bi]
        if dilation == 16:
            first = j == 0
        elif dilation == 4:
            first = (j == 0) & (idx // 4 == 0)
        else:
            first = (j == 0) & (idx == 0)
        table = 2 * bi + first.astype(jnp.int32)
        low = lax.broadcasted_iota(jnp.int32, (PIECE, LANES), 1) < HEAD_DIM
        for p in range(PAIRS_PER_STEP):
            for c in range(n_pieces):
                rows = slice(c * PIECE, (c + 1) * PIECE)
                bias = bias_ref[table, rows, :]
                ms = []
                for h in range(2):
                    head_rows = slice(h * BLOCK + c * PIECE, h * BLOCK + (c + 1) * PIECE)
                    sh = s_ref[slot, p, head_rows, :] + bias
                    mh = jnp.max(sh, axis=-1, keepdims=True)
                    ms.append(mh)
                    p_ref[slot, p, head_rows, :] = jnp.exp2(sh - mh)
                mb_ref[slot, p, rows, :] = jnp.where(low, ms[0], ms[1])

    def combine(bi, idx, slot):
        row_groups = _row_pieces(_branch_chunks(DILATIONS[bi], idx), PIECE)
        low = lax.broadcasted_iota(jnp.int32, (PIECE, 2 * LANES), 1) % LANES < HEAD_DIM
        for p, lanes in enumerate(pair_lanes):
            vb = key_block(bi, idx, v_ref, vp_ref, lanes)
            vbx = jnp.concatenate([vb, jnp.ones_like(vb)], axis=1)
            pv = jnp.concatenate([_dot(p_ref[slot, p, h * BLOCK:(h + 1) * BLOCK, :], vbx) for h in range(2)],
                                 axis=0)
            for c, chunks in enumerate(row_groups):
                rows = slice(c * PIECE, (c + 1) * PIECE)
                both = jnp.where(low, pv[c * PIECE:(c + 1) * PIECE],
                                 pv[BLOCK + c * PIECE:BLOCK + (c + 1) * PIECE])
                out, l_b = both[:, :LANES], both[:, LANES:]
                m_b = mb_ref[slot, p, rows, :]
                if bi == 0:
                    _store_rows(acc_ref, chunks, lanes, out)
                    _store_rows(m_ref, chunks, lanes, m_b)
                    _store_rows(l_ref, chunks, lanes, l_b)
                    continue
                m_o = _load_rows(m_ref, chunks, lanes)
                m_n = jnp.maximum(m_o, m_b)
                a_o = jnp.exp2(m_o - m_n)
                a_b = jnp.exp2(m_b - m_n)
                acc = a_o * _load_rows(acc_ref, chunks, lanes) + a_b * out
                l_n = a_o * _load_rows(l_ref, chunks, lanes) + a_b * l_b
                if bi == len(DILATIONS) - 1:
                    _store_rows(o_ref, chunks, lanes, acc / l_n)
                else:
                    _store_rows(acc_ref, chunks, lanes, acc)
                    _store_rows(m_ref, chunks, lanes, m_n)
                    _store_rows(l_ref, chunks, lanes, l_n)

    unit = lambda g: (g // n_blk, g % n_blk)
    for g in range(-SCORE_LEAD, n_units):
        if g >= 0:
            combine(*unit(g), g % N_SLOTS)
        if 0 <= g + SCORE_LEAD < n_units:
            scores(*unit(g + SCORE_LEAD), (g + SCORE_LEAD) % N_SLOTS)
        if 0 <= g + SOFTMAX_LEAD < n_units:
            softmax(*unit(g + SOFTMAX_LEAD), (g + SOFTMAX_LEAD) % N_SLOTS)


def _post_kernel(x_ref, p_ref, ya_ref, yb_ref, nb_ref, wo_ref, gf_ref, wg_ref, wu_ref,
                 wd_ref, gp_ref, wpg_ref, wpp_ref, gz_ref, o_ref, slab_ref, act_ref):
    for r in range(N_RES):
        for c in range(WIDTH_B // LANES):
            slab_ref[r // SUBLANES, c, pl.ds(r % SUBLANES, M_PER_STEP, stride=SUBLANES), :] = (
                yb_ref[r, :, c * LANES:(c + 1) * LANES])

    def natural_rows(c, rows):
        half = slice(rows.start // 2, rows.stop // 2)
        tiles = [slab_ref[a, c, half, :].reshape(-1, 1, SUBLANES, LANES) for a in range(2)]
        return jnp.concatenate(tiles, axis=1).reshape(rows.stop - rows.start, LANES)
    groups = [slice(g * (ROWS // ROW_GROUPS), (g + 1) * (ROWS // ROW_GROUPS)) for g in range(ROW_GROUPS)]
    pe = [_dot(p_ref[rows, :].astype(BF16), wpp_ref[...]) for rows in groups]
    ha = [x_ref[rows, :] + _dot(ya_ref[rows, :], wo_ref[:WIDTH_A, :]) for rows in groups]
    ybn = [_rms(jnp.concatenate([natural_rows(c, rows) for c in range(WIDTH_B // LANES)], axis=1),
                nb_ref[...]).astype(BF16) for rows in groups]
    h = [hg + _dot(y, wo_ref[WIDTH_A:, :]) for hg, y in zip(ha, ybn)]

    hn = [_rms(hg, gf_ref[...]).astype(BF16) for hg in h]
    for rows, hg in zip(groups, hn):
        for c in range(D_FF // FF_CHUNK):
            cols = slice(c * FF_CHUNK, (c + 1) * FF_CHUNK)
            gate = _dot(hg, wg_ref[:, cols])
            act_ref[rows, cols] = (jax.nn.silu(gate) * _dot(hg, wu_ref[:, cols])).astype(BF16)
    h = [hg + _dot(act_ref[rows, :], wd_ref[...]) for rows, hg in zip(groups, h)]

    gate = [jax.nn.sigmoid(_dot(_rms(hg, gp_ref[...]).astype(BF16), wpg_ref[...])) for hg in h]
    for rows, hg, gg, pg in zip(groups, h, gate, pe):
        o_ref[rows, :] = _rms(hg + gg * pg, gz_ref[...])


def _params(vmem_mib, n_axes):
    return pltpu.CompilerParams(dimension_semantics=("arbitrary",) * n_axes,
                                vmem_limit_bytes=vmem_mib * 1024 * 1024)


def kernel(x, p, mix_norm_g, w_in, sgu_w, sgu_b, sgu_norm_g, out_norm_a, out_norm_b, w_out,
           ffn_norm_g, w_gate, w_up, w_down, ple_norm_g, w_ple_gate, w_ple_proj, final_norm_g):
    b, s, d = x.shape
    assert d == D_MODEL and s % TILE == 0 and w_in.shape[0] == 1
    n_tiles = s // TILE
    n_tok = b * s
    steps_per_seq = s // ROWS
    steps_per_tile = TILE // ROWS
    row = lambda a: a.reshape(1, -1)

    x2 = x.reshape(n_tok, D_MODEL)
    tok_spec = lambda width: pl.BlockSpec((ROWS, width), lambda i: (i, 0))

    half = HEAD_DIM // 2
    inv = ROPE_THETA ** (-(jnp.arange(LANES) % half).astype(F32) / half)
    sgu_bias = jnp.repeat(sgu_b[0].T, HEAD_DIM, axis=1)
    rm_shape = jax.ShapeDtypeStruct((b, n_tiles, N_RES, BLOCK, WIDTH_B), F32)
    rm_spec = pl.BlockSpec((None, None, N_RES, M_PER_STEP, WIDTH_B),
                           lambda bi, i: (bi, i // steps_per_tile, 0, i % steps_per_tile, 0))
    seq_spec = lambda width: pl.BlockSpec((ROWS, width), lambda bi, i: (bi * steps_per_seq + i, 0))
    slab = pltpu.VMEM((2, WIDTH_B // LANES, ROWS // 2, LANES), F32)
    n_in = w_in.shape[2]

    def slab_spec(n_rows, n_cols):
        n_slabs = max(n for n in range(1, b * steps_per_seq + 1)
                      if n_rows % n == 0 and (n_rows // n) % (2 * SUBLANES) == 0)
        return pl.BlockSpec((n_rows // n_slabs, n_cols),
                            lambda bi, i: (jnp.minimum(bi * steps_per_seq + i, n_slabs - 1), 0))

    ffn_specs = [slab_spec(D_MODEL, D_FF), slab_spec(D_MODEL, D_FF), slab_spec(D_FF, D_MODEL)]
    y_a, q, k, v, w_gate_b, w_up_b, w_down_b = pl.pallas_call(
        _pre_kernel,
        grid=(b, steps_per_seq),
        in_specs=[seq_spec(D_MODEL), _resident((1, D_MODEL)), _resident((1, LANES)),
                  _resident((D_MODEL, n_in)), _resident((N_HEADS_A, CHUNK, CHUNK)),
                  _resident((CHUNK, WIDTH_A)), _resident((1, WIDTH_A)), _resident((1, WIDTH_A))] + ffn_specs,
        out_specs=[seq_spec(WIDTH_A), rm_spec, rm_spec, rm_spec] + ffn_specs,
        out_shape=[jax.ShapeDtypeStruct((n_tok, WIDTH_A), BF16), rm_shape, rm_shape, rm_shape,
                   jax.ShapeDtypeStruct((D_MODEL, D_FF), BF16), jax.ShapeDtypeStruct((D_MODEL, D_FF), BF16),
                   jax.ShapeDtypeStruct((D_FF, D_MODEL), BF16)],
        scratch_shapes=[pltpu.VMEM((ROWS, LANES), F32), pltpu.VMEM((ROWS, LANES), F32), slab, slab, slab],
        compiler_params=_params(48, 2),
    )(x2, row(mix_norm_g[0]), row(inv), w_in.reshape(D_MODEL, n_in), sgu_w[0], sgu_bias,
      row(sgu_norm_g[0]), row(out_norm_a[0]),
      w_gate.reshape(D_MODEL, D_FF), w_up.reshape(D_MODEL, D_FF), w_down.reshape(D_FF, D_MODEL))

    step_lanes = PAIRS_PER_STEP * LANES
    tile_shape = (b, n_tiles, TILE, WIDTH_B)
    att_spec = pl.BlockSpec((None, None, TILE, step_lanes), lambda bi, ci, ji: (bi, ji, 0, ci))
    prev_spec = pl.BlockSpec((None, None, TILE, step_lanes),
                             lambda bi, ci, ji: (bi, jnp.maximum(ji - 1, 0), 0, ci))
    y_b = pl.pallas_call(
        _attn_kernel,
        grid=(b, WIDTH_B // step_lanes, n_tiles),
        in_specs=[att_spec, att_spec, att_spec, prev_spec, prev_spec],
        out_specs=att_spec,
        out_shape=jax.ShapeDtypeStruct(tile_shape, F32),
        scratch_shapes=[pltpu.VMEM((TILE, step_lanes), F32), pltpu.VMEM((TILE, step_lanes), F32),
                        pltpu.VMEM((TILE, step_lanes), F32),
                        pltpu.VMEM((2 * len(DILATIONS), BLOCK, 2 * BLOCK), F32),
                        pltpu.VMEM((N_SLOTS, PAIRS_PER_STEP, 2 * BLOCK, 2 * BLOCK), F32),
                        pltpu.VMEM((N_SLOTS, PAIRS_PER_STEP, 2 * BLOCK, 2 * BLOCK), F32),
                        pltpu.VMEM((N_SLOTS, PAIRS_PER_STEP, BLOCK, LANES), F32)],
        compiler_params=_params(48, 3),
    )(q.reshape(tile_shape), k.reshape(tile_shape), v.reshape(tile_shape),
      k.reshape(tile_shape), v.reshape(tile_shape))

    out = pl.pallas_call(
        _post_kernel,
        grid=(n_tok // ROWS,),
        in_specs=[
            tok_spec(D_MODEL), tok_spec(PLE_DIM), tok_spec(WIDTH_A),
            pl.BlockSpec((None, None, N_RES, M_PER_STEP, WIDTH_B),
                         lambda i: (i // steps_per_seq, (i % steps_per_seq) // steps_per_tile, 0,
                                    i % steps_per_tile, 0)),
            _resident((1, WIDTH_B)),
            _resident((D_MODEL, D_MODEL)),
            _resident((1, D_MODEL)),
            _resident((D_MODEL, D_FF)), _resident((D_MODEL, D_FF)), _resident((D_FF, D_MODEL)),
            _resident((1, D_MODEL)),
            _resident((D_MODEL, D_MODEL)), _resident((PLE_DIM, D_MODEL)),
            _resident((1, D_MODEL)),
        ],
        out_specs=tok_spec(D_MODEL),
        out_shape=jax.ShapeDtypeStruct((n_tok, D_MODEL), x.dtype),
        scratch_shapes=[slab, pltpu.VMEM((ROWS, D_FF), BF16)],
        compiler_params=_params(56, 1),
    )(x2, p.reshape(n_tok, PLE_DIM), y_a, y_b.reshape(b, n_tiles, N_RES, BLOCK, WIDTH_B),
      row(out_norm_b[0]), w_out.reshape(D_MODEL, D_MODEL), row(ffn_norm_g[0]),
      w_gate_b, w_up_b, w_down_b,
      row(ple_norm_g[0]), w_ple_gate.reshape(D_MODEL, D_MODEL), w_ple_proj.reshape(PLE_DIM, D_MODEL),
      row(final_norm_g))
    return out.reshape(b, s, D_MODEL)
```

```python
import jax
import jax.numpy as jnp
from jax import lax
from jax.experimental import pallas as pl
from jax.experimental.pallas import tpu as pltpu

D_MODEL = 1024
HEAD_DIM = 64
N_HEADS_A = 4
WIDTH_A = 256
WIDTH_B = 768
CHUNK = 128
BLOCK = 128
DILATIONS = (1, 4, 16)
N_RES = 16
TILE = N_RES * BLOCK
D_FF = 2816
PLE_DIM = 256
EPS = 1e-6
ROPE_THETA = 10000.0
LANES = 128
SUBLANES = 8

ROWS = 512
M_PER_STEP = ROWS // N_RES
PRE_ROWS = 1024
FF_CHUNK = 256
ROW_GROUPS = 2
PAIRS_PER_STEP = 2
PIECE = 32
SCORE_LEAD = 2
SOFTMAX_LEAD = 1
N_SLOTS = max(SCORE_LEAD - SOFTMAX_LEAD, SOFTMAX_LEAD) + 1
LOG2_E = 1.4426950408889634

F32 = jnp.float32
BF16 = jnp.bfloat16


def _rms(x, g):
    return x * lax.rsqrt(jnp.mean(x * x, axis=-1, keepdims=True) + EPS) * g


def _dot(a, b):
    return jnp.dot(a, b, preferred_element_type=F32)


def _resident(shape):
    return pl.BlockSpec(shape, lambda *_: (0,) * len(shape), pipeline_mode=pl.Buffered(1))


def _to_residue_major(t, c, slab_ref, out_ref):
    n_m = t.shape[0] // N_RES
    halves = t.reshape(n_m, 2, SUBLANES, LANES)
    for a in range(2):
        slab_ref[a, c] = halves[:, a].reshape(n_m * SUBLANES, LANES)
    for r in range(N_RES):
        out_ref[r, :, c * LANES:(c + 1) * LANES] = (
            slab_ref[r // SUBLANES, c, pl.ds(r % SUBLANES, n_m, stride=SUBLANES), :])


def _pre_kernel(x_ref, g_ref, inv_ref, w_ref, sw_ref, sb_ref, sg_ref, na_ref, wg_ref, wu_ref, wd_ref,
                ya_ref, q_ref, k_ref, v_ref, wg_out, wu_out, wd_out, cu_ref, su_ref, slab_ref):
    i = pl.program_id(1)
    wg_out[...] = wg_ref[...].astype(BF16)
    wu_out[...] = wu_ref[...].astype(BF16)
    wd_out[...] = wd_ref[...].astype(BF16)
    lane = lax.broadcasted_iota(jnp.int32, (1, LANES), 1)
    sign = jnp.where((lane % HEAD_DIM) < (HEAD_DIM // 2), -1.0, 1.0)

    @pl.when((pl.program_id(0) == 0) & (i == 0))
    def _():
        u = lax.broadcasted_iota(jnp.int32, (PRE_ROWS, LANES), 0).astype(F32)
        ang = u * inv_ref[...]
        cu_ref[...] = jnp.cos(ang)
        su_ref[...] = jnp.sin(ang) * sign

    ang0 = (i * PRE_ROWS).astype(F32) * inv_ref[...]
    ct = jnp.cos(ang0)
    st = jnp.sin(ang0) * sign
    cos = ct * cu_ref[...] - st * su_ref[...]
    sin = st * cu_ref[...] + ct * su_ref[...]
    first_half = (lax.broadcasted_iota(jnp.int32, (PRE_ROWS, LANES), 1) % HEAD_DIM) < (HEAD_DIM // 2)

    hn = _rms(x_ref[...], g_ref[...]).astype(BF16)

    def project(col0, slab_ref, out_ref, rope, scale):
        t = _dot(hn, w_ref[:, col0:col0 + WIDTH_B])
        for c in range(WIDTH_B // LANES):
            tc = t[:, c * LANES:(c + 1) * LANES]
            if rope:
                rot = jnp.where(first_half,
                                pltpu.roll(tc, LANES - HEAD_DIM // 2, 1),
                                pltpu.roll(tc, HEAD_DIM // 2, 1))
                tc = tc * cos + rot * sin
            if scale != 1.0:
                tc = tc * scale
            _to_residue_major(tc, c, slab_ref, out_ref)

    uv = _dot(hn, w_ref[:, :2 * WIDTH_A])
    u = jax.nn.gelu(uv[:, :WIDTH_A])
    v = jax.nn.gelu(uv[:, WIDTH_A:])
    mu = jnp.mean(v, axis=-1, keepdims=True)
    d = v - mu
    var = jnp.mean(d * d, axis=-1, keepdims=True)
    vf = d * lax.rsqrt(var + EPS) * sg_ref[...]

    project(2 * WIDTH_A, slab_ref, q_ref, True, HEAD_DIM ** -0.5 * LOG2_E)
    project(2 * WIDTH_A + WIDTH_B, slab_ref, k_ref, True, 1.0)

    row = lax.broadcasted_iota(jnp.int32, (CHUNK, CHUNK), 0)
    col = lax.broadcasted_iota(jnp.int32, (CHUNK, CHUNK), 1)
    w4 = jnp.concatenate(
        [jnp.where(col <= row, sw_ref[h], 0.0) for h in range(N_HEADS_A)], axis=0).astype(BF16)
    lane_a = lax.broadcasted_iota(jnp.int32, (CHUNK, WIDTH_A), 1)
    for c in range(PRE_ROWS // CHUNK):
        rows = slice(c * CHUNK, (c + 1) * CHUNK)
        mm = _dot(w4, vf[rows].astype(BF16))
        mixed = mm[3 * CHUNK:]
        for h in (2, 1, 0):
            mixed = jnp.where(lane_a < (h + 1) * HEAD_DIM, mm[h * CHUNK:(h + 1) * CHUNK], mixed)
        ya = u[rows] * (mixed + sb_ref[...])
        ya_ref[rows, :] = _rms(ya, na_ref[...]).astype(ya_ref.dtype)

    project(2 * WIDTH_A + 2 * WIDTH_B, slab_ref, v_ref, False, 1.0)


def _branch_chunks(dilation, idx):
    if dilation == 16:
        return [(idx * BLOCK, BLOCK)]
    if dilation == 4:
        r4, jj = idx % 4, idx // 4
        return [((4 * c + r4) * BLOCK + 32 * jj, 32) for c in range(4)]
    return [(r * BLOCK + SUBLANES * idx, SUBLANES) for r in range(N_RES)]


def _prev_block(dilation, idx):
    if dilation == 16:
        return True, idx
    n_per_res = N_RES // dilation
    if dilation == 4:
        r4, jj = idx % 4, idx // 4
        return jj == 0, r4 + 4 * ((jj + n_per_res - 1) % n_per_res)
    return idx == 0, (idx + n_per_res - 1) % n_per_res


def _sub_index(dilation, a):
    if dilation == 16:
        return a
    if dilation == 4:
        return 4 * (a % 32) + a // 32
    return N_RES * (a % SUBLANES) + a // SUBLANES


def _load_rows(ref, chunks, lanes):
    parts = [ref[s:s + n, lanes] for s, n in chunks]
    return parts[0] if len(parts) == 1 else jnp.concatenate(parts, axis=0)


def _store_rows(ref, chunks, lanes, val):
    off = 0
    for s, n in chunks:
        ref[s:s + n, lanes] = val[off:off + n]
        off += n


def _row_pieces(chunks, piece):
    flat = []
    for s, n in chunks:
        step = min(n, piece)
        flat += [(s + o, step) for o in range(0, n, step)]
    per_group = piece // flat[0][1]
    return [flat[g:g + per_group] for g in range(0, len(flat), per_group)]


def _attn_kernel(q_ref, k_ref, v_ref, kp_ref, vp_ref, o_ref, acc_ref, m_ref, l_ref, bias_ref,
                 s_ref, p_ref, mb_ref):
    j = pl.program_id(2)

    @pl.when((pl.program_id(0) == 0) & (pl.program_id(1) == 0) & (j == 0))
    def _():
        qa = lax.broadcasted_iota(jnp.int32, (BLOCK, 2 * BLOCK), 0)
        kc = lax.broadcasted_iota(jnp.int32, (BLOCK, 2 * BLOCK), 1)
        for bi, dilation in enumerate(DILATIONS):
            dist = (BLOCK + _sub_index(dilation, qa)
                    - _sub_index(dilation, kc % BLOCK) - BLOCK * (kc // BLOCK))
            band = (dist >= 0) & (dist <= BLOCK)
            bias_ref[2 * bi] = jnp.where(band, 0.0, -jnp.inf)
            bias_ref[2 * bi + 1] = jnp.where(band & (kc >= BLOCK), 0.0, -jnp.inf)

    n_blk = TILE // BLOCK
    n_units = len(DILATIONS) * n_blk
    pair_lanes = [slice(p * LANES, (p + 1) * LANES) for p in range(PAIRS_PER_STEP)]
    n_pieces = BLOCK // PIECE

    def key_block(bi, idx, cur_ref, prev_ref, lanes):
        dilation = DILATIONS[bi]
        in_prev_tile, pidx = _prev_block(dilation, idx)
        return jnp.concatenate(
            [_load_rows(prev_ref if in_prev_tile else cur_ref, _branch_chunks(dilation, pidx), lanes),
             _load_rows(cur_ref, _branch_chunks(dilation, idx), lanes)], axis=0)

    def scores(bi, idx, slot):
        q_chunks = _branch_chunks(DILATIONS[bi], idx)
        low = lax.broadcasted_iota(jnp.int32, (BLOCK, LANES), 1) < HEAD_DIM
        for p, lanes in enumerate(pair_lanes):
            q = _load_rows(q_ref, q_chunks, lanes)
            lhs = jnp.concatenate([jnp.where(low, q, 0.0), jnp.where(low, 0.0, q)],
                                  axis=0).astype(BF16)
            kb = key_block(bi, idx, k_ref, kp_ref, lanes).astype(BF16)
            for h in range(2):
                s_ref[slot, p, h * BLOCK:(h + 1) * BLOCK, :] = lax.dot_general(
                    lhs[h * BLOCK:(h + 1) * BLOCK], kb, (((1,), (1,)), ((), ())),
                    preferred_element_type=F32)

    def softmax(bi, idx, slot):
        dilation = DILATIONS[bi]
        if dilation == 16:
            first = j == 0
        elif dilation == 4:
            first = (j == 0) & (idx // 4 == 0)
        else:
            first = (j == 0) & (idx == 0)
        table = 2 * bi + first.astype(jnp.int32)
        low = lax.broadcasted_iota(jnp.int32, (PIECE, LANES), 1) < HEAD_DIM
        for p in range(PAIRS_PER_STEP):
            for c in range(n_pieces):
                rows = slice(c * PIECE, (c + 1) * PIECE)
                bias = bias_ref[table, rows, :]
                ms = []
                for h in range(2):
                    head_rows = slice(h * BLOCK + c * PIECE, h * BLOCK + (c + 1) * PIECE)
                    sh = s_ref[slot, p, head_rows, :] + bias
                    mh = jnp.max(sh, axis=-1, keepdims=True)
                    ms.append(mh)
                    p_ref[slot, p, head_rows, :] = jnp.exp2(sh - mh)
                mb_ref[slot, p, rows, :] = jnp.where(low, ms[0], ms[1])

    def combine(bi, idx, slot):
        row_groups = _row_pieces(_branch_chunks(DILATIONS[bi], idx), PIECE)
        low = lax.broadcasted_iota(jnp.int32, (PIECE, 2 * LANES), 1) % LANES < HEAD_DIM
        for p, lanes in enumerate(pair_lanes):
            vb = key_block(bi, idx, v_ref, vp_ref, lanes)
            vbx = jnp.concatenate([vb, jnp.ones_like(vb)], axis=1)
            pv = jnp.concatenate([_dot(p_ref[slot, p, h * BLOCK:(h + 1) * BLOCK, :], vbx) for h in range(2)],
                                 axis=0)
            for c, chunks in enumerate(row_groups):
                rows = slice(c * PIECE, (c + 1) * PIECE)
                both = jnp.where(low, pv[c * PIECE:(c + 1) * PIECE],
                                 pv[BLOCK + c * PIECE:BLOCK + (c + 1) * PIECE])
                out, l_b = both[:, :LANES], both[:, LANES:]
                m_b = mb_ref[slot, p, rows, :]
                if bi == 0:
                    _store_rows(acc_ref, chunks, lanes, out)
                    _store_rows(m_ref, chunks, lanes, m_b)
                    _store_rows(l_ref, chunks, lanes, l_b)
                    continue
                m_o = _load_rows(m_ref, chunks, lanes)
                m_n = jnp.maximum(m_o, m_b)
                a_o = jnp.exp2(m_o - m_n)
                a_b = jnp.exp2(m_b - m_n)
                acc = a_o * _load_rows(acc_ref, chunks, lanes) + a_b * out
                l_n = a_o * _load_rows(l_ref, chunks, lanes) + a_b * l_b
                if bi == len(DILATIONS) - 1:
                    _store_rows(o_ref, chunks, lanes, acc / l_n)
                else:
                    _store_rows(acc_ref, chunks, lanes, acc)
                    _store_rows(m_ref, chunks, lanes, m_n)
                    _store_rows(l_ref, chunks, lanes, l_n)

    unit = lambda g: (g // n_blk, g % n_blk)
    for g in range(-SCORE_LEAD, n_units):
        if g >= 0:
            combine(*unit(g), g % N_SLOTS)
        if 0 <= g + SCORE_LEAD < n_units:
            scores(*unit(g + SCORE_LEAD), (g + SCORE_LEAD) % N_SLOTS)
        if 0 <= g + SOFTMAX_LEAD < n_units:
            softmax(*unit(g + SOFTMAX_LEAD), (g + SOFTMAX_LEAD) % N_SLOTS)


def _post_kernel(x_ref, p_ref, ya_ref, yb_ref, nb_ref, wo_ref, gf_ref, wg_ref, wu_ref,
                 wd_ref, gp_ref, wpg_ref, wpp_ref, gz_ref, o_ref, slab_ref, act_ref):
    for r in range(N_RES):
        for c in range(WIDTH_B // LANES):
            slab_ref[r // SUBLANES, c, pl.ds(r % SUBLANES, M_PER_STEP, stride=SUBLANES), :] = (
                yb_ref[r, :, c * LANES:(c + 1) * LANES])

    def natural_rows(c, rows):
        half = slice(rows.start // 2, rows.stop // 2)
        tiles = [slab_ref[a, c, half, :].reshape(-1, 1, SUBLANES, LANES) for a in range(2)]
        return jnp.concatenate(tiles, axis=1).reshape(rows.stop - rows.start, LANES)
    groups = [slice(g * (ROWS // ROW_GROUPS), (g + 1) * (ROWS // ROW_GROUPS)) for g in range(ROW_GROUPS)]
    pe = [_dot(p_ref[rows, :].astype(BF16), wpp_ref[...]) for rows in groups]
    ha = [x_ref[rows, :] + _dot(ya_ref[rows, :], wo_ref[:WIDTH_A, :]) for rows in groups]
    ybn = [_rms(jnp.concatenate([natural_rows(c, rows) for c in range(WIDTH_B // LANES)], axis=1),
                nb_ref[...]).astype(BF16) for rows in groups]
    h = [hg + _dot(y, wo_ref[WIDTH_A:, :]) for hg, y in zip(ha, ybn)]

    hn = [_rms(hg, gf_ref[...]).astype(BF16) for hg in h]
    for rows, hg in zip(groups, hn):
        for c in range(D_FF // FF_CHUNK):
            cols = slice(c * FF_CHUNK, (c + 1) * FF_CHUNK)
            gate = _dot(hg, wg_ref[:, cols])
            act_ref[rows, cols] = (jax.nn.silu(gate) * _dot(hg, wu_ref[:, cols])).astype(BF16)
    h = [hg + _dot(act_ref[rows, :], wd_ref[...]) for rows, hg in zip(groups, h)]

    gate = [jax.nn.sigmoid(_dot(_rms(hg, gp_ref[...]).astype(BF16), wpg_ref[...])) for hg in h]
    for rows, hg, gg, pg in zip(groups, h, gate, pe):
        o_ref[rows, :] = _rms(hg + gg * pg, gz_ref[...])


def _params(vmem_mib, n_axes):
    return pltpu.CompilerParams(dimension_semantics=("arbitrary",) * n_axes,
                                vmem_limit_bytes=vmem_mib * 1024 * 1024)


def kernel(x, p, mix_norm_g, w_in, sgu_w, sgu_b, sgu_norm_g, out_norm_a, out_norm_b, w_out,
           ffn_norm_g, w_gate, w_up, w_down, ple_norm_g, w_ple_gate, w_ple_proj, final_norm_g):
    b, s, d = x.shape
    assert d == D_MODEL and s % TILE == 0 and w_in.shape[0] == 1
    n_tiles = s // TILE
    n_tok = b * s
    steps_per_seq = s // ROWS
    steps_per_tile = TILE // ROWS
    row = lambda a: a.reshape(1, -1)

    x2 = x.reshape(n_tok, D_MODEL)
    tok_spec = lambda width: pl.BlockSpec((ROWS, width), lambda i: (i, 0))

    half = HEAD_DIM // 2
    inv = ROPE_THETA ** (-(jnp.arange(LANES) % half).astype(F32) / half)
    sgu_bias = jnp.repeat(sgu_b[0].T, HEAD_DIM, axis=1)
    rm_shape = jax.ShapeDtypeStruct((b, n_tiles, N_RES, BLOCK, WIDTH_B), F32)
    pre_steps = s // PRE_ROWS
    pre_per_tile = TILE // PRE_ROWS
    rm_spec = pl.BlockSpec((None, None, N_RES, PRE_ROWS // N_RES, WIDTH_B),
                           lambda bi, i: (bi, i // pre_per_tile, 0, i % pre_per_tile, 0))
    seq_spec = lambda width: pl.BlockSpec((PRE_ROWS, width), lambda bi, i: (bi * pre_steps + i, 0))
    slab = lambda rows: pltpu.VMEM((2, WIDTH_B // LANES, rows // 2, LANES), F32)
    n_in = w_in.shape[2]

    def slab_spec(n_rows, n_cols):
        n_slabs = max(n for n in range(1, b * pre_steps + 1)
                      if n_rows % n == 0 and (n_rows // n) % (2 * SUBLANES) == 0)
        return pl.BlockSpec((n_rows // n_slabs, n_cols),
                            lambda bi, i: (jnp.minimum(bi * pre_steps + i, n_slabs - 1), 0))

    ffn_specs = [slab_spec(D_MODEL, D_FF), slab_spec(D_MODEL, D_FF), slab_spec(D_FF, D_MODEL)]
    y_a, q, k, v, w_gate_b, w_up_b, w_down_b = pl.pallas_call(
        _pre_kernel,
        grid=(b, pre_steps),
        in_specs=[seq_spec(D_MODEL), _resident((1, D_MODEL)), _resident((1, LANES)),
                  _resident((D_MODEL, n_in)), _resident((N_HEADS_A, CHUNK, CHUNK)),
                  _resident((CHUNK, WIDTH_A)), _resident((1, WIDTH_A)), _resident((1, WIDTH_A))] + ffn_specs,
        out_specs=[seq_spec(WIDTH_A), rm_spec, rm_spec, rm_spec] + ffn_specs,
        out_shape=[jax.ShapeDtypeStruct((n_tok, WIDTH_A), BF16), rm_shape, rm_shape, rm_shape,
                   jax.ShapeDtypeStruct((D_MODEL, D_FF), BF16), jax.ShapeDtypeStruct((D_MODEL, D_FF), BF16),
                   jax.ShapeDtypeStruct((D_FF, D_MODEL), BF16)],
        scratch_shapes=[pltpu.VMEM((PRE_ROWS, LANES), F32), pltpu.VMEM((PRE_ROWS, LANES), F32),
                        slab(PRE_ROWS)],
        compiler_params=_params(60, 2),
    )(x2, row(mix_norm_g[0]), row(inv), w_in.reshape(D_MODEL, n_in), sgu_w[0], sgu_bias,
      row(sgu_norm_g[0]), row(out_norm_a[0]),
      w_gate.reshape(D_MODEL, D_FF), w_up.reshape(D_MODEL, D_FF), w_down.reshape(D_FF, D_MODEL))

    step_lanes = PAIRS_PER_STEP * LANES
    tile_shape = (b, n_tiles, TILE, WIDTH_B)
    att_spec = pl.BlockSpec((None, None, TILE, step_lanes), lambda bi, ci, ji: (bi, ji, 0, ci))
    prev_spec = pl.BlockSpec((None, None, TILE, step_lanes),
                             lambda bi, ci, ji: (bi, jnp.maximum(ji - 1, 0), 0, ci))
    y_b = pl.pallas_call(
        _attn_kernel,
        grid=(b, WIDTH_B // step_lanes, n_tiles),
        in_specs=[att_spec, att_spec, att_spec, prev_spec, prev_spec],
        out_specs=att_spec,
        out_shape=jax.ShapeDtypeStruct(tile_shape, F32),
        scratch_shapes=[pltpu.VMEM((TILE, step_lanes), F32), pltpu.VMEM((TILE, step_lanes), F32),
                        pltpu.VMEM((TILE, step_lanes), F32),
                        pltpu.VMEM((2 * len(DILATIONS), BLOCK, 2 * BLOCK), F32),
                        pltpu.VMEM((N_SLOTS, PAIRS_PER_STEP, 2 * BLOCK, 2 * BLOCK), F32),
                        pltpu.VMEM((N_SLOTS, PAIRS_PER_STEP, 2 * BLOCK, 2 * BLOCK), F32),
                        pltpu.VMEM((N_SLOTS, PAIRS_PER_STEP, BLOCK, LANES), F32)],
        compiler_params=_params(48, 3),
    )(q.reshape(tile_shape), k.reshape(tile_shape), v.reshape(tile_shape),
      k.reshape(tile_shape), v.reshape(tile_shape))

    out = pl.pallas_call(
        _post_kernel,
        grid=(n_tok // ROWS,),
        in_specs=[
            tok_spec(D_MODEL), tok_spec(PLE_DIM), tok_spec(WIDTH_A),
            pl.BlockSpec((None, None, N_RES, M_PER_STEP, WIDTH_B),
                         lambda i: (i // steps_per_seq, (i % steps_per_seq) // steps_per_tile, 0,
                                    i % steps_per_tile, 0)),
            _resident((1, WIDTH_B)),
            _resident((D_MODEL, D_MODEL)),
            _resident((1, D_MODEL)),
            _resident((D_MODEL, D_FF)), _resident((D_MODEL, D_FF)), _resident((D_FF, D_MODEL)),
            _resident((1, D_MODEL)),
            _resident((D_MODEL, D_MODEL)), _resident((PLE_DIM, D_MODEL)),
            _resident((1, D_MODEL)),
        ],
        out_specs=tok_spec(D_MODEL),
        out_shape=jax.ShapeDtypeStruct((n_tok, D_MODEL), x.dtype),
        scratch_shapes=[slab(ROWS), pltpu.VMEM((ROWS, D_FF), BF16)],
        compiler_params=_params(56, 1),
    )(x2, p.reshape(n_tok, PLE_DIM), y_a, y_b.reshape(b, n_tiles, N_RES, BLOCK, WIDTH_B),
      row(out_norm_b[0]), w_out.reshape(D_MODEL, D_MODEL), row(ffn_norm_g[0]),
      w_gate_b, w_up_b, w_down_b,
      row(ple_norm_g[0]), w_ple_gate.reshape(D_MODEL, D_MODEL), w_ple_proj.reshape(PLE_DIM, D_MODEL),
      row(final_norm_g))
    return out.reshape(b, s, D_MODEL)
```

```python
import jax
import jax.numpy as jnp
from jax import lax
from jax.experimental import pallas as pl
from jax.experimental.pallas import tpu as pltpu

D_MODEL = 1024
HEAD_DIM = 64
N_HEADS_A = 4
WIDTH_A = 256
WIDTH_B = 768
CHUNK = 128
BLOCK = 128
DILATIONS = (1, 4, 16)
N_RES = 16
TILE = N_RES * BLOCK
D_FF = 2816
PLE_DIM = 256
EPS = 1e-6
ROPE_THETA = 10000.0
LANES = 128
SUBLANES = 8

ROWS = 512
M_PER_STEP = ROWS // N_RES
PRE_ROWS = 1024
FF_CHUNK = 256
ROW_GROUPS = 2
PAIRS_PER_STEP = 2
PIECE = 32
SCORE_LEAD = 2
SOFTMAX_LEAD = 1
N_SLOTS = max(SCORE_LEAD - SOFTMAX_LEAD, SOFTMAX_LEAD) + 1
LOG2_E = 1.4426950408889634

F32 = jnp.float32
BF16 = jnp.bfloat16


def _rms(x, g):
    return x * lax.rsqrt(jnp.mean(x * x, axis=-1, keepdims=True) + EPS) * g


def _dot(a, b):
    return jnp.dot(a, b, preferred_element_type=F32)


def _resident(shape):
    return pl.BlockSpec(shape, lambda *_: (0,) * len(shape), pipeline_mode=pl.Buffered(1))


def _to_residue_major(t, c, slab_ref, out_ref):
    n_m = t.shape[0] // N_RES
    halves = t.reshape(n_m, 2, SUBLANES, LANES)
    for a in range(2):
        slab_ref[a, c] = halves[:, a].reshape(n_m * SUBLANES, LANES)
    for r in range(N_RES):
        out_ref[r, :, c * LANES:(c + 1) * LANES] = (
            slab_ref[r // SUBLANES, c, pl.ds(r % SUBLANES, n_m, stride=SUBLANES), :])


def _pre_kernel(x_ref, g_ref, inv_ref, w_ref, sw_ref, sb_ref, sg_ref, na_ref, wg_ref, wu_ref, wd_ref,
                ya_ref, q_ref, k_ref, v_ref, wg_out, wu_out, wd_out, cu_ref, su_ref, slab_ref):
    i = pl.program_id(1)
    wg_out[...] = wg_ref[...].astype(BF16)
    wu_out[...] = wu_ref[...].astype(BF16)
    wd_out[...] = wd_ref[...].astype(BF16)
    lane = lax.broadcasted_iota(jnp.int32, (1, LANES), 1)
    sign = jnp.where((lane % HEAD_DIM) < (HEAD_DIM // 2), -1.0, 1.0)

    @pl.when((pl.program_id(0) == 0) & (i == 0))
    def _():
        u = lax.broadcasted_iota(jnp.int32, (PRE_ROWS, LANES), 0).astype(F32)
        ang = u * inv_ref[...]
        cu_ref[...] = jnp.cos(ang)
        su_ref[...] = jnp.sin(ang) * sign

    ang0 = (i * PRE_ROWS).astype(F32) * inv_ref[...]
    ct = jnp.cos(ang0)
    st = jnp.sin(ang0) * sign
    cos = ct * cu_ref[...] - st * su_ref[...]
    sin = st * cu_ref[...] + ct * su_ref[...]
    first_half = (lax.broadcasted_iota(jnp.int32, (PRE_ROWS, LANES), 1) % HEAD_DIM) < (HEAD_DIM // 2)

    hn = _rms(x_ref[...], g_ref[...]).astype(BF16)

    def project(col0, slab_ref, out_ref, rope, scale):
        t = _dot(hn, w_ref[:, col0:col0 + WIDTH_B])
        for c in range(WIDTH_B // LANES):
            tc = t[:, c * LANES:(c + 1) * LANES]
            if rope:
                rot = jnp.where(first_half,
                                pltpu.roll(tc, LANES - HEAD_DIM // 2, 1),
                                pltpu.roll(tc, HEAD_DIM // 2, 1))
                tc = tc * cos + rot * sin
            if scale != 1.0:
                tc = tc * scale
            _to_residue_major(tc, c, slab_ref, out_ref)

    uv = _dot(hn, w_ref[:, :2 * WIDTH_A])
    u = jax.nn.gelu(uv[:, :WIDTH_A])
    v = jax.nn.gelu(uv[:, WIDTH_A:])
    mu = jnp.mean(v, axis=-1, keepdims=True)
    d = v - mu
    var = jnp.mean(d * d, axis=-1, keepdims=True)
    vf = d * lax.rsqrt(var + EPS) * sg_ref[...]

    project(2 * WIDTH_A, slab_ref, q_ref, True, HEAD_DIM ** -0.5 * LOG2_E)
    project(2 * WIDTH_A + WIDTH_B, slab_ref, k_ref, True, 1.0)

    row = lax.broadcasted_iota(jnp.int32, (CHUNK, CHUNK), 0)
    col = lax.broadcasted_iota(jnp.int32, (CHUNK, CHUNK), 1)
    w4 = jnp.concatenate(
        [jnp.where(col <= row, sw_ref[h], 0.0) for h in range(N_HEADS_A)], axis=0).astype(BF16)
    lane_a = lax.broadcasted_iota(jnp.int32, (CHUNK, WIDTH_A), 1)
    for c in range(PRE_ROWS // CHUNK):
        rows = slice(c * CHUNK, (c + 1) * CHUNK)
        mm = _dot(w4, vf[rows].astype(BF16))
        mixed = mm[3 * CHUNK:]
        for h in (2, 1, 0):
            mixed = jnp.where(lane_a < (h + 1) * HEAD_DIM, mm[h * CHUNK:(h + 1) * CHUNK], mixed)
        ya = u[rows] * (mixed + sb_ref[...])
        ya_ref[rows, :] = _rms(ya, na_ref[...]).astype(ya_ref.dtype)

    project(2 * WIDTH_A + 2 * WIDTH_B, slab_ref, v_ref, False, 1.0)


def _branch_chunks(dilation, idx):
    if dilation == 16:
        return [(idx * BLOCK, BLOCK)]
    if dilation == 4:
        r4, jj = idx % 4, idx // 4
        return [((4 * c + r4) * BLOCK + 32 * jj, 32) for c in range(4)]
    return [(r * BLOCK + SUBLANES * idx, SUBLANES) for r in range(N_RES)]


def _prev_block(dilation, idx):
    if dilation == 16:
        return True, idx
    n_per_res = N_RES // dilation
    if dilation == 4:
        r4, jj = idx % 4, idx // 4
        return jj == 0, r4 + 4 * ((jj + n_per_res - 1) % n_per_res)
    return idx == 0, (idx + n_per_res - 1) % n_per_res


def _sub_index(dilation, a):
    if dilation == 16:
        return a
    if dilation == 4:
        return 4 * (a % 32) + a // 32
    return N_RES * (a % SUBLANES) + a // SUBLANES


def _load_rows(ref, chunks, lanes):
    parts = [ref[s:s + n, lanes] for s, n in chunks]
    return parts[0] if len(parts) == 1 else jnp.concatenate(parts, axis=0)


def _store_rows(ref, chunks, lanes, val):
    off = 0
    for s, n in chunks:
        ref[s:s + n, lanes] = val[off:off + n]
        off += n


def _row_pieces(chunks, piece):
    flat = []
    for s, n in chunks:
        step = min(n, piece)
        flat += [(s + o, step) for o in range(0, n, step)]
    per_group = piece // flat[0][1]
    return [flat[g:g + per_group] for g in range(0, len(flat), per_group)]


def _attn_kernel(q_ref, k_ref, v_ref, kp_ref, vp_ref, o_ref, acc_ref, m_ref, l_ref, bias_ref,
                 s_ref, p_ref, mb_ref):
    j = pl.program_id(2)

    @pl.when((pl.program_id(0) == 0) & (pl.program_id(1) == 0) & (j == 0))
    def _():
        qa = lax.broadcasted_iota(jnp.int32, (BLOCK, 2 * BLOCK), 0)
        kc = lax.broadcasted_iota(jnp.int32, (BLOCK, 2 * BLOCK), 1)
        for bi, dilation in enumerate(DILATIONS):
            dist = (BLOCK + _sub_index(dilation, qa)
                    - _sub_index(dilation, kc % BLOCK) - BLOCK * (kc // BLOCK))
            band = (dist >= 0) & (dist <= BLOCK)
            bias_ref[2 * bi] = jnp.where(band, 0.0, -jnp.inf)
            bias_ref[2 * bi + 1] = jnp.where(band & (kc >= BLOCK), 0.0, -jnp.inf)

    n_blk = TILE // BLOCK
    n_units = len(DILATIONS) * n_blk
    pair_lanes = [slice(p * LANES, (p + 1) * LANES) for p in range(PAIRS_PER_STEP)]
    n_pieces = BLOCK // PIECE

    def key_block(bi, idx, cur_ref, prev_ref, lanes):
        dilation = DILATIONS[bi]
        in_prev_tile, pidx = _prev_block(dilation, idx)
        return jnp.concatenate(
            [_load_rows(prev_ref if in_prev_tile else cur_ref, _branch_chunks(dilation, pidx), lanes),
             _load_rows(cur_ref, _branch_chunks(dilation, idx), lanes)], axis=0)

    def scores(bi, idx, slot, pairs):
        q_chunks = _branch_chunks(DILATIONS[bi], idx)
        low = lax.broadcasted_iota(jnp.int32, (BLOCK, LANES), 1) < HEAD_DIM
        for p in pairs:
            lanes = pair_lanes[p]
            q = _load_rows(q_ref, q_chunks, lanes)
            lhs = jnp.concatenate([jnp.where(low, q, 0.0), jnp.where(low, 0.0, q)],
                                  axis=0).astype(BF16)
            kb = key_block(bi, idx, k_ref, kp_ref, lanes).astype(BF16)
            for h in range(2):
                s_ref[slot, p, h * BLOCK:(h + 1) * BLOCK, :] = lax.dot_general(
                    lhs[h * BLOCK:(h + 1) * BLOCK], kb, (((1,), (1,)), ((), ())),
                    preferred_element_type=F32)

    def softmax(bi, idx, slot, pairs):
        dilation = DILATIONS[bi]
        if dilation == 16:
            first = j == 0
        elif dilation == 4:
            first = (j == 0) & (idx // 4 == 0)
        else:
            first = (j == 0) & (idx == 0)
        table = 2 * bi + first.astype(jnp.int32)
        low = lax.broadcasted_iota(jnp.int32, (PIECE, LANES), 1) < HEAD_DIM
        for p in pairs:
            for c in range(n_pieces):
                rows = slice(c * PIECE, (c + 1) * PIECE)
                bias = bias_ref[table, rows, :]
                ms = []
                for h in range(2):
                    head_rows = slice(h * BLOCK + c * PIECE, h * BLOCK + (c + 1) * PIECE)
                    sh = s_ref[slot, p, head_rows, :] + bias
                    mh = jnp.max(sh, axis=-1, keepdims=True)
                    ms.append(mh)
                    p_ref[slot, p, head_rows, :] = jnp.exp2(sh - mh)
                mb_ref[slot, p, rows, :] = jnp.where(low, ms[0], ms[1])

    def combine(bi, idx, slot, pairs):
        row_groups = _row_pieces(_branch_chunks(DILATIONS[bi], idx), PIECE)
        low = lax.broadcasted_iota(jnp.int32, (PIECE, 2 * LANES), 1) % LANES < HEAD_DIM
        for p in pairs:
            lanes = pair_lanes[p]
            vb = key_block(bi, idx, v_ref, vp_ref, lanes)
            vbx = jnp.concatenate([vb, jnp.ones_like(vb)], axis=1)
            pv = jnp.concatenate([_dot(p_ref[slot, p, h * BLOCK:(h + 1) * BLOCK, :], vbx) for h in range(2)],
                                 axis=0)
            for c, chunks in enumerate(row_groups):
                rows = slice(c * PIECE, (c + 1) * PIECE)
                both = jnp.where(low, pv[c * PIECE:(c + 1) * PIECE],
                                 pv[BLOCK + c * PIECE:BLOCK + (c + 1) * PIECE])
                out, l_b = both[:, :LANES], both[:, LANES:]
                m_b = mb_ref[slot, p, rows, :]
                if bi == 0:
                    _store_rows(acc_ref, chunks, lanes, out)
                    _store_rows(m_ref, chunks, lanes, m_b)
                    _store_rows(l_ref, chunks, lanes, l_b)
                    continue
                m_o = _load_rows(m_ref, chunks, lanes)
                m_n = jnp.maximum(m_o, m_b)
                a_o = jnp.exp2(m_o - m_n)
                a_b = jnp.exp2(m_b - m_n)
                acc = a_o * _load_rows(acc_ref, chunks, lanes) + a_b * out
                l_n = a_o * _load_rows(l_ref, chunks, lanes) + a_b * l_b
                if bi == len(DILATIONS) - 1:
                    _store_rows(o_ref, chunks, lanes, acc / l_n)
                else:
                    _store_rows(acc_ref, chunks, lanes, acc)
                    _store_rows(m_ref, chunks, lanes, m_n)
                    _store_rows(l_ref, chunks, lanes, l_n)

    unit = lambda g: (g // n_blk, g % n_blk)
    for g in range(-SCORE_LEAD, n_units):
        for p in range(PAIRS_PER_STEP):
            if g >= 0:
                combine(*unit(g), g % N_SLOTS, [p])
            if 0 <= g + SCORE_LEAD < n_units:
                scores(*unit(g + SCORE_LEAD), (g + SCORE_LEAD) % N_SLOTS, [p])
            if 0 <= g + SOFTMAX_LEAD < n_units:
                softmax(*unit(g + SOFTMAX_LEAD), (g + SOFTMAX_LEAD) % N_SLOTS, [p])


def _post_kernel(x_ref, p_ref, ya_ref, yb_ref, nb_ref, wo_ref, gf_ref, wg_ref, wu_ref,
                 wd_ref, gp_ref, wpg_ref, wpp_ref, gz_ref, o_ref, slab_ref, act_ref):
    for r in range(N_RES):
        for c in range(WIDTH_B // LANES):
            slab_ref[r // SUBLANES, c, pl.ds(r % SUBLANES, M_PER_STEP, stride=SUBLANES), :] = (
                yb_ref[r, :, c * LANES:(c + 1) * LANES])

    def natural_rows(c, rows):
        half = slice(rows.start // 2, rows.stop // 2)
        tiles = [slab_ref[a, c, half, :].reshape(-1, 1, SUBLANES, LANES) for a in range(2)]
        return jnp.concatenate(tiles, axis=1).reshape(rows.stop - rows.start, LANES)
    groups = [slice(g * (ROWS // ROW_GROUPS), (g + 1) * (ROWS // ROW_GROUPS)) for g in range(ROW_GROUPS)]
    pe = [_dot(p_ref[rows, :].astype(BF16), wpp_ref[...]) for rows in groups]
    ha = [x_ref[rows, :] + _dot(ya_ref[rows, :], wo_ref[:WIDTH_A, :]) for rows in groups]
    ybn = [_rms(jnp.concatenate([natural_rows(c, rows) for c in range(WIDTH_B // LANES)], axis=1),
                nb_ref[...]).astype(BF16) for rows in groups]
    h = [hg + _dot(y, wo_ref[WIDTH_A:, :]) for hg, y in zip(ha, ybn)]

    hn = [_rms(hg, gf_ref[...]).astype(BF16) for hg in h]
    for rows, hg in zip(groups, hn):
        for c in range(D_FF // FF_CHUNK):
            cols = slice(c * FF_CHUNK, (c + 1) * FF_CHUNK)
            gate = _dot(hg, wg_ref[:, cols])
            act_ref[rows, cols] = (jax.nn.silu(gate) * _dot(hg, wu_ref[:, cols])).astype(BF16)
    h = [hg + _dot(act_ref[rows, :], wd_ref[...]) for rows, hg in zip(groups, h)]

    gate = [jax.nn.sigmoid(_dot(_rms(hg, gp_ref[...]).astype(BF16), wpg_ref[...])) for hg in h]
    for rows, hg, gg, pg in zip(groups, h, gate, pe):
        o_ref[rows, :] = _rms(hg + gg * pg, gz_ref[...])


def _params(vmem_mib, n_axes):
    return pltpu.CompilerParams(dimension_semantics=("arbitrary",) * n_axes,
                                vmem_limit_bytes=vmem_mib * 1024 * 1024)


def kernel(x, p, mix_norm_g, w_in, sgu_w, sgu_b, sgu_norm_g, out_norm_a, out_norm_b, w_out,
           ffn_norm_g, w_gate, w_up, w_down, ple_norm_g, w_ple_gate, w_ple_proj, final_norm_g):
    b, s, d = x.shape
    assert d == D_MODEL and s % TILE == 0 and w_in.shape[0] == 1
    n_tiles = s // TILE
    n_tok = b * s
    steps_per_seq = s // ROWS
    steps_per_tile = TILE // ROWS
    row = lambda a: a.reshape(1, -1)

    x2 = x.reshape(n_tok, D_MODEL)
    tok_spec = lambda width: pl.BlockSpec((ROWS, width), lambda i: (i, 0))

    half = HEAD_DIM // 2
    inv = ROPE_THETA ** (-(jnp.arange(LANES) % half).astype(F32) / half)
    sgu_bias = jnp.repeat(sgu_b[0].T, HEAD_DIM, axis=1)
    rm_shape = jax.ShapeDtypeStruct((b, n_tiles, N_RES, BLOCK, WIDTH_B), F32)
    pre_steps = s // PRE_ROWS
    pre_per_tile = TILE // PRE_ROWS
    rm_spec = pl.BlockSpec((None, None, N_RES, PRE_ROWS // N_RES, WIDTH_B),
                           lambda bi, i: (bi, i // pre_per_tile, 0, i % pre_per_tile, 0))
    seq_spec = lambda width: pl.BlockSpec((PRE_ROWS, width), lambda bi, i: (bi * pre_steps + i, 0))
    slab = lambda rows: pltpu.VMEM((2, WIDTH_B // LANES, rows // 2, LANES), F32)
    n_in = w_in.shape[2]

    def slab_spec(n_rows, n_cols):
        n_slabs = max(n for n in range(1, b * pre_steps + 1)
                      if n_rows % n == 0 and (n_rows // n) % (2 * SUBLANES) == 0)
        return pl.BlockSpec((n_rows // n_slabs, n_cols),
                            lambda bi, i: (jnp.minimum(bi * pre_steps + i, n_slabs - 1), 0))

    ffn_specs = [slab_spec(D_MODEL, D_FF), slab_spec(D_MODEL, D_FF), slab_spec(D_FF, D_MODEL)]
    y_a, q, k, v, w_gate_b, w_up_b, w_down_b = pl.pallas_call(
        _pre_kernel,
        grid=(b, pre_steps),
        in_specs=[seq_spec(D_MODEL), _resident((1, D_MODEL)), _resident((1, LANES)),
                  _resident((D_MODEL, n_in)), _resident((N_HEADS_A, CHUNK, CHUNK)),
                  _resident((CHUNK, WIDTH_A)), _resident((1, WIDTH_A)), _resident((1, WIDTH_A))] + ffn_specs,
        out_specs=[seq_spec(WIDTH_A), rm_spec, rm_spec, rm_spec] + ffn_specs,
        out_shape=[jax.ShapeDtypeStruct((n_tok, WIDTH_A), BF16), rm_shape, rm_shape, rm_shape,
                   jax.ShapeDtypeStruct((D_MODEL, D_FF), BF16), jax.ShapeDtypeStruct((D_MODEL, D_FF), BF16),
                   jax.ShapeDtypeStruct((D_FF, D_MODEL), BF16)],
        scratch_shapes=[pltpu.VMEM((PRE_ROWS, LANES), F32), pltpu.VMEM((PRE_ROWS, LANES), F32),
                        slab(PRE_ROWS)],
        compiler_params=_params(60, 2),
    )(x2, row(mix_norm_g[0]), row(inv), w_in.reshape(D_MODEL, n_in), sgu_w[0], sgu_bias,
      row(sgu_norm_g[0]), row(out_norm_a[0]),
      w_gate.reshape(D_MODEL, D_FF), w_up.reshape(D_MODEL, D_FF), w_down.reshape(D_FF, D_MODEL))

    step_lanes = PAIRS_PER_STEP * LANES
    tile_shape = (b, n_tiles, TILE, WIDTH_B)
    att_spec = pl.BlockSpec((None, None, TILE, step_lanes), lambda bi, ci, ji: (bi, ji, 0, ci))
    prev_spec = pl.BlockSpec((None, None, TILE, step_lanes),
                             lambda bi, ci, ji: (bi, jnp.maximum(ji - 1, 0), 0, ci))
    y_b = pl.pallas_call(
        _attn_kernel,
        grid=(b, WIDTH_B // step_lanes, n_tiles),
        in_specs=[att_spec, att_spec, att_spec, prev_spec, prev_spec],
        out_specs=att_spec,
        out_shape=jax.ShapeDtypeStruct(tile_shape, F32),
        scratch_shapes=[pltpu.VMEM((TILE, step_lanes), F32), pltpu.VMEM((TILE, step_lanes), F32),
                        pltpu.VMEM((TILE, step_lanes), F32),
                        pltpu.VMEM((2 * len(DILATIONS), BLOCK, 2 * BLOCK), F32),
                        pltpu.VMEM((N_SLOTS, PAIRS_PER_STEP, 2 * BLOCK, 2 * BLOCK), F32),
                        pltpu.VMEM((N_SLOTS, PAIRS_PER_STEP, 2 * BLOCK, 2 * BLOCK), F32),
                        pltpu.VMEM((N_SLOTS, PAIRS_PER_STEP, BLOCK, LANES), F32)],
        compiler_params=_params(48, 3),
    )(q.reshape(tile_shape), k.reshape(tile_shape), v.reshape(tile_shape),
      k.reshape(tile_shape), v.reshape(tile_shape))

    out = pl.pallas_call(
        _post_kernel,
        grid=(n_tok // ROWS,),
        in_specs=[
            tok_spec(D_MODEL), tok_spec(PLE_DIM), tok_spec(WIDTH_A),
            pl.BlockSpec((None, None, N_RES, M_PER_STEP, WIDTH_B),
                         lambda i: (i // steps_per_seq, (i % steps_per_seq) // steps_per_tile, 0,
                                    i % steps_per_tile, 0)),
            _resident((1, WIDTH_B)),
            _resident((D_MODEL, D_MODEL)),
            _resident((1, D_MODEL)),
            _resident((D_MODEL, D_FF)), _resident((D_MODEL, D_FF)), _resident((D_FF, D_MODEL)),
            _resident((1, D_MODEL)),
            _resident((D_MODEL, D_MODEL)), _resident((PLE_DIM, D_MODEL)),
            _resident((1, D_MODEL)),
        ],
        out_specs=tok_spec(D_MODEL),
        out_shape=jax.ShapeDtypeStruct((n_tok, D_MODEL), x.dtype),
        scratch_shapes=[slab(ROWS), pltpu.VMEM((ROWS, D_FF), BF16)],
        compiler_params=_params(56, 1),
    )(x2, p.reshape(n_tok, PLE_DIM), y_a, y_b.reshape(b, n_tiles, N_RES, BLOCK, WIDTH_B),
      row(out_norm_b[0]), w_out.reshape(D_MODEL, D_MODEL), row(ffn_norm_g[0]),
      w_gate_b, w_up_b, w_down_b,
      row(ple_norm_g[0]), w_ple_gate.reshape(D_MODEL, D_MODEL), w_ple_proj.reshape(PLE_DIM, D_MODEL),
      row(final_norm_g))
    return out.reshape(b, s, D_MODEL)
```

```python
import jax
import jax.numpy as jnp
from jax import lax
from jax.experimental import pallas as pl
from jax.experimental.pallas import tpu as pltpu

D_MODEL = 1024
HEAD_DIM = 64
N_HEADS_A = 4
WIDTH_A = 256
WIDTH_B = 768
CHUNK = 128
BLOCK = 128
DILATIONS = (1, 4, 16)
N_RES = 16
TILE = N_RES * BLOCK
D_FF = 2816
PLE_DIM = 256
EPS = 1e-6
ROPE_THETA = 10000.0
LANES = 128
SUBLANES = 8

ROWS = 512
M_PER_STEP = ROWS // N_RES
PRE_ROWS = 1024
FF_CHUNK = 256
ROW_GROUPS = 2
PAIRS_PER_STEP = 2
PIECE = 32
SCORE_LEAD = 2
SOFTMAX_LEAD = 1
N_SLOTS = max(SCORE_LEAD - SOFTMAX_LEAD, SOFTMAX_LEAD) + 1
LOG2_E = 1.4426950408889634

F32 = jnp.float32
BF16 = jnp.bfloat16


def _rms(x, g):
    return x * lax.rsqrt(jnp.mean(x * x, axis=-1, keepdims=True) + EPS) * g


def _dot(a, b):
    return jnp.dot(a, b, preferred_element_type=F32)


def _resident(shape):
    return pl.BlockSpec(shape, lambda *_: (0,) * len(shape), pipeline_mode=pl.Buffered(1))


def _to_residue_major(t, c, slab_ref, out_ref):
    n_m = t.shape[0] // N_RES
    halves = t.reshape(n_m, 2, SUBLANES, LANES)
    for a in range(2):
        slab_ref[a, c] = halves[:, a].reshape(n_m * SUBLANES, LANES)
    for r in range(N_RES):
        out_ref[r, :, c * LANES:(c + 1) * LANES] = (
            slab_ref[r // SUBLANES, c, pl.ds(r % SUBLANES, n_m, stride=SUBLANES), :])


def _pre_kernel(x_ref, g_ref, inv_ref, w_ref, wg_ref, wu_ref, wd_ref,
                q_ref, k_ref, v_ref, wg_out, wu_out, wd_out, cu_ref, su_ref, slab_ref):
    i = pl.program_id(1)
    wg_out[...] = wg_ref[...].astype(BF16)
    wu_out[...] = wu_ref[...].astype(BF16)
    wd_out[...] = wd_ref[...].astype(BF16)
    lane = lax.broadcasted_iota(jnp.int32, (1, LANES), 1)
    sign = jnp.where((lane % HEAD_DIM) < (HEAD_DIM // 2), -1.0, 1.0)

    @pl.when((pl.program_id(0) == 0) & (i == 0))
    def _():
        u = lax.broadcasted_iota(jnp.int32, (PRE_ROWS, LANES), 0).astype(F32)
        ang = u * inv_ref[...]
        cu_ref[...] = jnp.cos(ang)
        su_ref[...] = jnp.sin(ang) * sign

    ang0 = (i * PRE_ROWS).astype(F32) * inv_ref[...]
    ct = jnp.cos(ang0)
    st = jnp.sin(ang0) * sign
    cos = ct * cu_ref[...] - st * su_ref[...]
    sin = st * cu_ref[...] + ct * su_ref[...]
    first_half = (lax.broadcasted_iota(jnp.int32, (PRE_ROWS, LANES), 1) % HEAD_DIM) < (HEAD_DIM // 2)

    hn = _rms(x_ref[...], g_ref[...]).astype(BF16)

    def project(col0, slab_ref, out_ref, rope, scale):
        t = _dot(hn, w_ref[:, col0:col0 + WIDTH_B])
        for c in range(WIDTH_B // LANES):
            tc = t[:, c * LANES:(c + 1) * LANES]
            if rope:
                rot = jnp.where(first_half,
                                pltpu.roll(tc, LANES - HEAD_DIM // 2, 1),
                                pltpu.roll(tc, HEAD_DIM // 2, 1))
                tc = tc * cos + rot * sin
            if scale != 1.0:
                tc = tc * scale
            _to_residue_major(tc, c, slab_ref, out_ref)

    project(2 * WIDTH_A, slab_ref, q_ref, True, HEAD_DIM ** -0.5 * LOG2_E)
    project(2 * WIDTH_A + WIDTH_B, slab_ref, k_ref, True, 1.0)
    project(2 * WIDTH_A + 2 * WIDTH_B, slab_ref, v_ref, False, 1.0)


def _sgu_gates(x, gm_ref, wuv_ref, sg_ref):
    hm = _rms(x, gm_ref[...]).astype(BF16)
    uv = _dot(hm, wuv_ref[...])
    u = jax.nn.gelu(uv[:, :WIDTH_A])
    v = jax.nn.gelu(uv[:, WIDTH_A:])
    mu = jnp.mean(v, axis=-1, keepdims=True)
    d = v - mu
    var = jnp.mean(d * d, axis=-1, keepdims=True)
    return u, d * lax.rsqrt(var + EPS) * sg_ref[...]


def _sgu_mix(u, vf, sw_ref, sb_ref, na_ref):
    row = lax.broadcasted_iota(jnp.int32, (CHUNK, CHUNK), 0)
    col = lax.broadcasted_iota(jnp.int32, (CHUNK, CHUNK), 1)
    w4 = jnp.concatenate(
        [jnp.where(col <= row, sw_ref[h], 0.0) for h in range(N_HEADS_A)], axis=0).astype(BF16)
    lane_a = lax.broadcasted_iota(jnp.int32, (CHUNK, WIDTH_A), 1)
    out = []
    for c in range(u.shape[0] // CHUNK):
        rows = slice(c * CHUNK, (c + 1) * CHUNK)
        mm = _dot(w4, vf[rows].astype(BF16))
        mixed = mm[3 * CHUNK:]
        for h in (2, 1, 0):
            mixed = jnp.where(lane_a < (h + 1) * HEAD_DIM, mm[h * CHUNK:(h + 1) * CHUNK], mixed)
        out.append(_rms(u[rows] * (mixed + sb_ref[...]), na_ref[...]).astype(BF16))
    return jnp.concatenate(out, axis=0)


def _branch_chunks(dilation, idx):
    if dilation == 16:
        return [(idx * BLOCK, BLOCK)]
    if dilation == 4:
        r4, jj = idx % 4, idx // 4
        return [((4 * c + r4) * BLOCK + 32 * jj, 32) for c in range(4)]
    return [(r * BLOCK + SUBLANES * idx, SUBLANES) for r in range(N_RES)]


def _prev_block(dilation, idx):
    if dilation == 16:
        return True, idx
    n_per_res = N_RES // dilation
    if dilation == 4:
        r4, jj = idx % 4, idx // 4
        return jj == 0, r4 + 4 * ((jj + n_per_res - 1) % n_per_res)
    return idx == 0, (idx + n_per_res - 1) % n_per_res


def _sub_index(dilation, a):
    if dilation == 16:
        return a
    if dilation == 4:
        return 4 * (a % 32) + a // 32
    return N_RES * (a % SUBLANES) + a // SUBLANES


def _load_rows(ref, chunks, lanes):
    parts = [ref[s:s + n, lanes] for s, n in chunks]
    return parts[0] if len(parts) == 1 else jnp.concatenate(parts, axis=0)


def _store_rows(ref, chunks, lanes, val):
    off = 0
    for s, n in chunks:
        ref[s:s + n, lanes] = val[off:off + n]
        off += n


def _row_pieces(chunks, piece):
    flat = []
    for s, n in chunks:
        step = min(n, piece)
        flat += [(s + o, step) for o in range(0, n, step)]
    per_group = piece // flat[0][1]
    return [flat[g:g + per_group] for g in range(0, len(flat), per_group)]


def _attn_kernel(q_ref, k_ref, v_ref, kp_ref, vp_ref, o_ref, acc_ref, m_ref, l_ref, bias_ref,
                 s_ref, p_ref, mb_ref):
    j = pl.program_id(2)

    @pl.when((pl.program_id(0) == 0) & (pl.program_id(1) == 0) & (j == 0))
    def _():
        qa = lax.broadcasted_iota(jnp.int32, (BLOCK, 2 * BLOCK), 0)
        kc = lax.broadcasted_iota(jnp.int32, (BLOCK, 2 * BLOCK), 1)
        for bi, dilation in enumerate(DILATIONS):
            dist = (BLOCK + _sub_index(dilation, qa)
                    - _sub_index(dilation, kc % BLOCK) - BLOCK * (kc // BLOCK))
            band = (dist >= 0) & (dist <= BLOCK)
            bias_ref[2 * bi] = jnp.where(band, 0.0, -jnp.inf)
            bias_ref[2 * bi + 1] = jnp.where(band & (kc >= BLOCK), 0.0, -jnp.inf)

    n_blk = TILE // BLOCK
    n_units = len(DILATIONS) * n_blk
    pair_lanes = [slice(p * LANES, (p + 1) * LANES) for p in range(PAIRS_PER_STEP)]
    n_pieces = BLOCK // PIECE

    def key_block(bi, idx, cur_ref, prev_ref, lanes):
        dilation = DILATIONS[bi]
        in_prev_tile, pidx = _prev_block(dilation, idx)
        return jnp.concatenate(
            [_load_rows(prev_ref if in_prev_tile else cur_ref, _branch_chunks(dilation, pidx), lanes),
             _load_rows(cur_ref, _branch_chunks(dilation, idx), lanes)], axis=0)

    def scores(bi, idx, slot, pairs):
        q_chunks = _branch_chunks(DILATIONS[bi], idx)
        low = lax.broadcasted_iota(jnp.int32, (BLOCK, LANES), 1) < HEAD_DIM
        for p in pairs:
            lanes = pair_lanes[p]
            q = _load_rows(q_ref, q_chunks, lanes)
            lhs = jnp.concatenate([jnp.where(low, q, 0.0), jnp.where(low, 0.0, q)],
                                  axis=0).astype(BF16)
            kb = key_block(bi, idx, k_ref, kp_ref, lanes).astype(BF16)
            for h in range(2):
                s_ref[slot, p, h * BLOCK:(h + 1) * BLOCK, :] = lax.dot_general(
                    lhs[h * BLOCK:(h + 1) * BLOCK], kb, (((1,), (1,)), ((), ())),
                    preferred_element_type=F32)

    def softmax(bi, idx, slot, pairs):
        dilation = DILATIONS[bi]
        if dilation == 16:
            first = j == 0
        elif dilation == 4:
            first = (j == 0) & (idx // 4 == 0)
        else:
            first = (j == 0) & (idx == 0)
        table = 2 * bi + first.astype(jnp.int32)
        low = lax.broadcasted_iota(jnp.int32, (PIECE, LANES), 1) < HEAD_DIM
        for p in pairs:
            for c in range(n_pieces):
                rows = slice(c * PIECE, (c + 1) * PIECE)
                bias = bias_ref[table, rows, :]
                ms = []
                for h in range(2):
                    head_rows = slice(h * BLOCK + c * PIECE, h * BLOCK + (c + 1) * PIECE)
                    sh = s_ref[slot, p, head_rows, :] + bias
                    mh = jnp.max(sh, axis=-1, keepdims=True)
                    ms.append(mh)
                    p_ref[slot, p, head_rows, :] = jnp.exp2(sh - mh)
                mb_ref[slot, p, rows, :] = jnp.where(low, ms[0], ms[1])

    def combine(bi, idx, slot, pairs):
        row_groups = _row_pieces(_branch_chunks(DILATIONS[bi], idx), PIECE)
        low = lax.broadcasted_iota(jnp.int32, (PIECE, 2 * LANES), 1) % LANES < HEAD_DIM
        for p in pairs:
            lanes = pair_lanes[p]
            vb = key_block(bi, idx, v_ref, vp_ref, lanes)
            vbx = jnp.concatenate([vb, jnp.ones_like(vb)], axis=1)
            pv = jnp.concatenate([_dot(p_ref[slot, p, h * BLOCK:(h + 1) * BLOCK, :], vbx) for h in range(2)],
                                 axis=0)
            for c, chunks in enumerate(row_groups):
                rows = slice(c * PIECE, (c + 1) * PIECE)
                both = jnp.where(low, pv[c * PIECE:(c + 1) * PIECE],
                                 pv[BLOCK + c * PIECE:BLOCK + (c + 1) * PIECE])
                out, l_b = both[:, :LANES], both[:, LANES:]
                m_b = mb_ref[slot, p, rows, :]
                if bi == 0:
                    _store_rows(acc_ref, chunks, lanes, out)
                    _store_rows(m_ref, chunks, lanes, m_b)
                    _store_rows(l_ref, chunks, lanes, l_b)
                    continue
                m_o = _load_rows(m_ref, chunks, lanes)
                m_n = jnp.maximum(m_o, m_b)
                a_o = jnp.exp2(m_o - m_n)
                a_b = jnp.exp2(m_b - m_n)
                acc = a_o * _load_rows(acc_ref, chunks, lanes) + a_b * out
                l_n = a_o * _load_rows(l_ref, chunks, lanes) + a_b * l_b
                if bi == len(DILATIONS) - 1:
                    _store_rows(o_ref, chunks, lanes, acc / l_n)
                else:
                    _store_rows(acc_ref, chunks, lanes, acc)
                    _store_rows(m_ref, chunks, lanes, m_n)
                    _store_rows(l_ref, chunks, lanes, l_n)

    unit = lambda g: (g // n_blk, g % n_blk)
    for g in range(-SCORE_LEAD, n_units):
        for p in range(PAIRS_PER_STEP):
            if g >= 0:
                combine(*unit(g), g % N_SLOTS, [p])
            if 0 <= g + SCORE_LEAD < n_units:
                scores(*unit(g + SCORE_LEAD), (g + SCORE_LEAD) % N_SLOTS, [p])
            if 0 <= g + SOFTMAX_LEAD < n_units:
                softmax(*unit(g + SOFTMAX_LEAD), (g + SOFTMAX_LEAD) % N_SLOTS, [p])


def _post_kernel(x_ref, p_ref, yb_ref, gm_ref, wuv_ref, sw_ref, sb_ref, sg_ref, na_ref,
                 nb_ref, wo_ref, gf_ref, wg_ref, wu_ref,
                 wd_ref, gp_ref, wpg_ref, wpp_ref, gz_ref, o_ref, slab_ref, act_ref):
    for r in range(N_RES):
        for c in range(WIDTH_B // LANES):
            slab_ref[r // SUBLANES, c, pl.ds(r % SUBLANES, M_PER_STEP, stride=SUBLANES), :] = (
                yb_ref[r, :, c * LANES:(c + 1) * LANES])

    def natural_rows(c, rows):
        half = slice(rows.start // 2, rows.stop // 2)
        tiles = [slab_ref[a, c, half, :].reshape(-1, 1, SUBLANES, LANES) for a in range(2)]
        return jnp.concatenate(tiles, axis=1).reshape(rows.stop - rows.start, LANES)
    groups = [slice(g * (ROWS // ROW_GROUPS), (g + 1) * (ROWS // ROW_GROUPS)) for g in range(ROW_GROUPS)]
    gates = [_sgu_gates(x_ref[rows, :], gm_ref, wuv_ref, sg_ref) for rows in groups]
    pe = [_dot(p_ref[rows, :].astype(BF16), wpp_ref[...]) for rows in groups]
    ybn = [_rms(jnp.concatenate([natural_rows(c, rows) for c in range(WIDTH_B // LANES)], axis=1),
                nb_ref[...]).astype(BF16) for rows in groups]
    hb = [x_ref[rows, :] + _dot(y, wo_ref[WIDTH_A:, :]) for rows, y in zip(groups, ybn)]
    ya = [_sgu_mix(u, vf, sw_ref, sb_ref, na_ref) for u, vf in gates]
    h = [hg + _dot(y, wo_ref[:WIDTH_A, :]) for hg, y in zip(hb, ya)]

    hn = [_rms(hg, gf_ref[...]).astype(BF16) for hg in h]
    for rows, hg in zip(groups, hn):
        for c in range(D_FF // FF_CHUNK):
            cols = slice(c * FF_CHUNK, (c + 1) * FF_CHUNK)
            gate = _dot(hg, wg_ref[:, cols])
            act_ref[rows, cols] = (jax.nn.silu(gate) * _dot(hg, wu_ref[:, cols])).astype(BF16)
    h = [hg + _dot(act_ref[rows, :], wd_ref[...]) for rows, hg in zip(groups, h)]

    gate = [jax.nn.sigmoid(_dot(_rms(hg, gp_ref[...]).astype(BF16), wpg_ref[...])) for hg in h]
    for rows, hg, gg, pg in zip(groups, h, gate, pe):
        o_ref[rows, :] = _rms(hg + gg * pg, gz_ref[...])


def _params(vmem_mib, n_axes):
    return pltpu.CompilerParams(dimension_semantics=("arbitrary",) * n_axes,
                                vmem_limit_bytes=vmem_mib * 1024 * 1024)


def kernel(x, p, mix_norm_g, w_in, sgu_w, sgu_b, sgu_norm_g, out_norm_a, out_norm_b, w_out,
           ffn_norm_g, w_gate, w_up, w_down, ple_norm_g, w_ple_gate, w_ple_proj, final_norm_g):
    b, s, d = x.shape
    assert d == D_MODEL and s % TILE == 0 and w_in.shape[0] == 1
    n_tiles = s // TILE
    n_tok = b * s
    steps_per_seq = s // ROWS
    steps_per_tile = TILE // ROWS
    row = lambda a: a.reshape(1, -1)

    x2 = x.reshape(n_tok, D_MODEL)
    tok_spec = lambda width: pl.BlockSpec((ROWS, width), lambda i: (i, 0))

    half = HEAD_DIM // 2
    inv = ROPE_THETA ** (-(jnp.arange(LANES) % half).astype(F32) / half)
    sgu_bias = jnp.repeat(sgu_b[0].T, HEAD_DIM, axis=1)
    rm_shape = jax.ShapeDtypeStruct((b, n_tiles, N_RES, BLOCK, WIDTH_B), F32)
    pre_steps = s // PRE_ROWS
    pre_per_tile = TILE // PRE_ROWS
    rm_spec = pl.BlockSpec((None, None, N_RES, PRE_ROWS // N_RES, WIDTH_B),
                           lambda bi, i: (bi, i // pre_per_tile, 0, i % pre_per_tile, 0))
    seq_spec = lambda width: pl.BlockSpec((PRE_ROWS, width), lambda bi, i: (bi * pre_steps + i, 0))
    slab = lambda rows: pltpu.VMEM((2, WIDTH_B // LANES, rows // 2, LANES), F32)
    n_in = w_in.shape[2]

    def slab_spec(n_rows, n_cols):
        n_slabs = max(n for n in range(1, b * pre_steps + 1)
                      if n_rows % n == 0 and (n_rows // n) % (2 * SUBLANES) == 0)
        return pl.BlockSpec((n_rows // n_slabs, n_cols),
                            lambda bi, i: (jnp.minimum(bi * pre_steps + i, n_slabs - 1), 0))

    ffn_specs = [slab_spec(D_MODEL, D_FF), slab_spec(D_MODEL, D_FF), slab_spec(D_FF, D_MODEL)]
    w_in2 = w_in.reshape(D_MODEL, n_in)
    q, k, v, w_gate_b, w_up_b, w_down_b = pl.pallas_call(
        _pre_kernel,
        grid=(b, pre_steps),
        in_specs=[seq_spec(D_MODEL), _resident((1, D_MODEL)), _resident((1, LANES)),
                  _resident((D_MODEL, n_in))] + ffn_specs,
        out_specs=[rm_spec, rm_spec, rm_spec] + ffn_specs,
        out_shape=[rm_shape, rm_shape, rm_shape,
                   jax.ShapeDtypeStruct((D_MODEL, D_FF), BF16), jax.ShapeDtypeStruct((D_MODEL, D_FF), BF16),
                   jax.ShapeDtypeStruct((D_FF, D_MODEL), BF16)],
        scratch_shapes=[pltpu.VMEM((PRE_ROWS, LANES), F32), pltpu.VMEM((PRE_ROWS, LANES), F32),
                        slab(PRE_ROWS)],
        compiler_params=_params(60, 2),
    )(x2, row(mix_norm_g[0]), row(inv), w_in2,
      w_gate.reshape(D_MODEL, D_FF), w_up.reshape(D_MODEL, D_FF), w_down.reshape(D_FF, D_MODEL))

    step_lanes = PAIRS_PER_STEP * LANES
    tile_shape = (b, n_tiles, TILE, WIDTH_B)
    att_spec = pl.BlockSpec((None, None, TILE, step_lanes), lambda bi, ci, ji: (bi, ji, 0, ci))
    prev_spec = pl.BlockSpec((None, None, TILE, step_lanes),
                             lambda bi, ci, ji: (bi, jnp.maximum(ji - 1, 0), 0, ci))
    y_b = pl.pallas_call(
        _attn_kernel,
        grid=(b, WIDTH_B // step_lanes, n_tiles),
        in_specs=[att_spec, att_spec, att_spec, prev_spec, prev_spec],
        out_specs=att_spec,
        out_shape=jax.ShapeDtypeStruct(tile_shape, F32),
        scratch_shapes=[pltpu.VMEM((TILE, step_lanes), F32), pltpu.VMEM((TILE, step_lanes), F32),
                        pltpu.VMEM((TILE, step_lanes), F32),
                        pltpu.VMEM((2 * len(DILATIONS), BLOCK, 2 * BLOCK), F32),
                        pltpu.VMEM((N_SLOTS, PAIRS_PER_STEP, 2 * BLOCK, 2 * BLOCK), F32),
                        pltpu.VMEM((N_SLOTS, PAIRS_PER_STEP, 2 * BLOCK, 2 * BLOCK), F32),
                        pltpu.VMEM((N_SLOTS, PAIRS_PER_STEP, BLOCK, LANES), F32)],
        compiler_params=_params(48, 3),
    )(q.reshape(tile_shape), k.reshape(tile_shape), v.reshape(tile_shape),
      k.reshape(tile_shape), v.reshape(tile_shape))

    out = pl.pallas_call(
        _post_kernel,
        grid=(n_tok // ROWS,),
        in_specs=[
            tok_spec(D_MODEL), tok_spec(PLE_DIM),
            pl.BlockSpec((None, None, N_RES, M_PER_STEP, WIDTH_B),
                         lambda i: (i // steps_per_seq, (i % steps_per_seq) // steps_per_tile, 0,
                                    i % steps_per_tile, 0)),
            _resident((1, D_MODEL)), _resident((D_MODEL, 2 * WIDTH_A)), _resident((N_HEADS_A, CHUNK, CHUNK)),
            _resident((CHUNK, WIDTH_A)), _resident((1, WIDTH_A)), _resident((1, WIDTH_A)),
            _resident((1, WIDTH_B)),
            _resident((D_MODEL, D_MODEL)),
            _resident((1, D_MODEL)),
            _resident((D_MODEL, D_FF)), _resident((D_MODEL, D_FF)), _resident((D_FF, D_MODEL)),
            _resident((1, D_MODEL)),
            _resident((D_MODEL, D_MODEL)), _resident((PLE_DIM, D_MODEL)),
            _resident((1, D_MODEL)),
        ],
        out_specs=tok_spec(D_MODEL),
        out_shape=jax.ShapeDtypeStruct((n_tok, D_MODEL), x.dtype),
        scratch_shapes=[slab(ROWS), pltpu.VMEM((ROWS, D_FF), BF16)],
        compiler_params=_params(56, 1),
    )(x2, p.reshape(n_tok, PLE_DIM), y_b.reshape(b, n_tiles, N_RES, BLOCK, WIDTH_B),
      row(mix_norm_g[0]), w_in2, sgu_w[0], sgu_bias, row(sgu_norm_g[0]), row(out_norm_a[0]),
      row(out_norm_b[0]), w_out.reshape(D_MODEL, D_MODEL), row(ffn_norm_g[0]),
      w_gate_b, w_up_b, w_down_b,
      row(ple_norm_g[0]), w_ple_gate.reshape(D_MODEL, D_MODEL), w_ple_proj.reshape(PLE_DIM, D_MODEL),
      row(final_norm_g))
    return out.reshape(b, s, D_MODEL)
```
